```python
import math
import jax, jax.numpy as jnp
from jax import lax
import numpy as np

D_MODEL = 2048
BATCH = 4
SEQ = 2048
DEPTH = 1
DEC_BATCH = 128
DEC_SEQ = 4
PAST_LEN = 16384
PAGE_SIZE = 128

N_META = 16
D_FF = 256 * ((8 * D_MODEL // 3 + 255) // 256)
W_POOL = D_MODEL // 2
W_SSM = D_MODEL // 2
POOL_WINDOWS = (2, 4, 8, 16)
N_POOL_GROUPS = len(POOL_WINDOWS)
POOL_GW = W_POOL // N_POOL_GROUPS
POOL_HIST = max(POOL_WINDOWS) - 1
SSM_GS = 16
N_SSM_GROUPS = W_SSM // SSM_GS
SSM_STATE = 64
W_IN = W_POOL + W_SSM + 2 * D_MODEL
RMS_EPS = 1e-6

kernel_name = "gated_pool_s5_macaron_decoder_step"


def rmsnorm(x, g):
    xf = x.astype(jnp.float32)
    r = lax.rsqrt(jnp.mean(xf * xf, axis=-1, keepdims=True) + RMS_EPS)
    return (xf * r * g.astype(jnp.float32)).astype(x.dtype)


def swiglu(x, w_gate, w_up, w_down):
    return (jax.nn.silu(x @ w_gate) * (x @ w_up)) @ w_down


def pool_mix(u, hist, pos0, w_grp, scale):
    nb, L, _ = u.shape
    full = jnp.concatenate([hist.astype(u.dtype), u], axis=1)
    cs = jnp.cumsum(full.astype(jnp.float32), axis=1)
    cs = jnp.pad(cs, ((0, 0), (1, 0), (0, 0)))
    avail = jnp.arange(L, dtype=jnp.int32) + (pos0 + 1)
    means = []
    for g, w in enumerate(POOL_WINDOWS):
        sl = slice(g * POOL_GW, (g + 1) * POOL_GW)
        s = cs[:, POOL_HIST + 1:POOL_HIST + 1 + L, sl] - cs[:, POOL_HIST + 1 - w:POOL_HIST + 1 - w + L, sl]
        cnt = jnp.minimum(avail, w).astype(jnp.float32)[None, :, None]
        means.append(s / cnt)
    d = jnp.concatenate(means, axis=-1) - u.astype(jnp.float32)
    d = d.reshape(nb, L, N_POOL_GROUPS, POOL_GW)
    y = jnp.einsum('blgc,gcd->blgd', d, w_grp.astype(jnp.float32)).reshape(nb, L, W_POOL)
    y = y * scale.astype(jnp.float32)
    return y.astype(u.dtype), full[:, -POOL_HIST:]


def s5_scan(u, h0_re, h0_im, lam_re, lam_im, log_dt, b_re, b_im, c_re, c_im, d_skip):
    nb, L, _ = u.shape
    f32 = jnp.float32
    lam = lax.complex(lam_re.astype(f32), lam_im.astype(f32))
    dt = jnp.exp(log_dt.astype(f32))[:, None]
    lam_bar = jnp.exp(lam * dt)
    b_mat = lax.complex(b_re.astype(f32), b_im.astype(f32))
    b_bar = ((lam_bar - 1.0) / lam)[..., None] * b_mat
    c_mat = lax.complex(c_re.astype(f32), c_im.astype(f32))
    ug = u.astype(f32).reshape(nb, L, N_SSM_GROUPS, SSM_GS)
    bu = jnp.einsum('blgc,gpc->blgp', ug.astype(jnp.complex64), b_bar)
    h0 = lax.complex(h0_re.astype(f32), h0_im.astype(f32))
    bu = bu.at[:, 0].add(lam_bar[None] * h0)
    a = jnp.broadcast_to(lam_bar[None, None], (1, L, N_SSM_GROUPS, SSM_STATE))

    def combine(left, right):
        a1, b1 = left
        a2, b2 = right
        return a1 * a2, a2 * b1 + b2

    _, h = lax.associative_scan(combine, (a, bu), axis=1)
    y = jnp.einsum('blgp,gcp->blgc', h, c_mat).real
    y = y + d_skip.astype(f32).reshape(N_SSM_GROUPS, SSM_GS) * ug
    h_last = h[:, -1]
    return y.reshape(nb, L, W_SSM).astype(u.dtype), jnp.real(h_last), jnp.imag(h_last)


def setup_inputs(seed: int = 0) -> dict:
    key = jax.random.key(seed)
    ks = iter(jax.random.split(key, 40))
    f32 = jnp.float32

    def nrm(shape, scale):
        return jax.random.normal(next(ks), shape, f32) * scale

    def gain(shape):
        return 1.0 + 0.05 * jax.random.normal(next(ks), shape, f32)

    L_ = DEPTH
    n_idx = jnp.arange(SSM_STATE, dtype=f32)
    inp = {}
    inp["x_prompt"] = nrm((BATCH, SEQ, D_MODEL), 1.0)
    inp["x_sample"] = nrm((DEC_BATCH, DEC_SEQ, D_MODEL), 1.0)
    inp["state_pool"] = nrm((L_, DEC_BATCH, POOL_HIST, W_POOL), 1.0)
    inp["state_ssm_re"] = nrm((L_, DEC_BATCH, N_SSM_GROUPS, SSM_STATE), 0.3)
    inp["state_ssm_im"] = nrm((L_, DEC_BATCH, N_SSM_GROUPS, SSM_STATE), 0.3)
    inp["meta_tokens"] = nrm((N_META, D_MODEL), 1.0)
    inp["norm_ffn1"] = gain((L_, D_MODEL))
    inp["ffn1_w_gate"] = nrm((L_, D_MODEL, D_FF), D_MODEL ** -0.5)
    inp["ffn1_w_up"] = nrm((L_, D_MODEL, D_FF), D_MODEL ** -0.5)
    inp["ffn1_w_down"] = nrm((L_, D_FF, D_MODEL), D_FF ** -0.5)
    inp["norm_mix"] = gain((L_, D_MODEL))
    inp["w_in"] = nrm((L_, D_MODEL, W_IN), D_MODEL ** -0.5)
    inp["b_gate"] = nrm((L_, 2 * D_MODEL), 0.02)
    inp["pool_w"] = nrm((L_, N_POOL_GROUPS, POOL_GW, POOL_GW), POOL_GW ** -0.5)
    inp["pool_scale"] = gain((L_, W_POOL))
    inp["ssm_lambda_re"] = -0.5 + nrm((L_, N_SSM_GROUPS, SSM_STATE), 0.01)
    inp["ssm_lambda_im"] = math.pi * n_idx + nrm((L_, N_SSM_GROUPS, SSM_STATE), 0.01)
    inp["ssm_log_dt"] = jax.random.uniform(next(ks), (L_, N_SSM_GROUPS), f32,
                                           math.log(1e-3), math.log(1e-1))
    inp["ssm_b_re"] = nrm((L_, N_SSM_GROUPS, SSM_STATE, SSM_GS), (2 * SSM_GS) ** -0.5)
    inp["ssm_b_im"] = nrm((L_, N_SSM_GROUPS, SSM_STATE, SSM_GS), (2 * SSM_GS) ** -0.5)
    inp["ssm_c_re"] = nrm((L_, N_SSM_GROUPS, SSM_GS, SSM_STATE), (2 * SSM_STATE) ** -0.5)
    inp["ssm_c_im"] = nrm((L_, N_SSM_GROUPS, SSM_GS, SSM_STATE), (2 * SSM_STATE) ** -0.5)
    inp["ssm_d"] = nrm((L_, W_SSM), 1.0)
    inp["glu_w"] = nrm((L_, W_SSM, W_SSM), W_SSM ** -0.5)
    inp["glu_b"] = nrm((L_, W_SSM), 0.02)
    inp["w_branch_a"] = nrm((L_, W_POOL, D_MODEL), W_POOL ** -0.5)
    inp["w_branch_b"] = nrm((L_, W_SSM, D_MODEL), W_SSM ** -0.5)
    inp["w_out"] = nrm((L_, D_MODEL, D_MODEL), D_MODEL ** -0.5)
    inp["norm_ffn2"] = gain((L_, D_MODEL))
    inp["ffn2_w_gate"] = nrm((L_, D_MODEL, D_FF), D_MODEL ** -0.5)
    inp["ffn2_w_up"] = nrm((L_, D_MODEL, D_FF), D_MODEL ** -0.5)
    inp["ffn2_w_down"] = nrm((L_, D_FF, D_MODEL), D_FF ** -0.5)
    inp["final_norm"] = gain((D_MODEL,))
    return inp


def reference(x_prompt, x_sample, state_pool, state_ssm_re, state_ssm_im, meta_tokens,
              norm_ffn1, ffn1_w_gate, ffn1_w_up, ffn1_w_down,
              norm_mix, w_in, b_gate, pool_w, pool_scale,
              ssm_lambda_re, ssm_lambda_im, ssm_log_dt, ssm_b_re, ssm_b_im, ssm_c_re, ssm_c_im,
              ssm_d, glu_w, glu_b, w_branch_a, w_branch_b, w_out,
              norm_ffn2, ffn2_w_gate, ffn2_w_up, ffn2_w_down, final_norm):

    def mixer(xn, l, hist, h_re, h_im, pos0):
        z = xn @ w_in[l]
        u_pool = z[..., :W_POOL]
        u_ssm = z[..., W_POOL:W_POOL + W_SSM]
        gates = jax.nn.sigmoid(z[..., W_POOL + W_SSM:] + b_gate[l])
        gate_a = gates[..., :D_MODEL]
        gate_b = gates[..., D_MODEL:]
        a_out, hist_new = pool_mix(u_pool, hist, pos0, pool_w[l], pool_scale[l])
        s_out, hre_new, him_new = s5_scan(u_ssm, h_re, h_im, ssm_lambda_re[l], ssm_lambda_im[l],
                                          ssm_log_dt[l], ssm_b_re[l], ssm_b_im[l],
                                          ssm_c_re[l], ssm_c_im[l], ssm_d[l])
        s_out = jax.nn.gelu(s_out)
        s_out = s_out * jax.nn.sigmoid(s_out @ glu_w[l] + glu_b[l])
        merged = gate_a * (a_out @ w_branch_a[l]) + gate_b * (s_out @ w_branch_b[l])
        return merged @ w_out[l], hist_new, hre_new, him_new

    def run_group(x, hists, hres, hims, pos0):
        new_h, new_re, new_im = [], [], []
        for l in range(DEPTH):
            x = x + 0.5 * swiglu(rmsnorm(x, norm_ffn1[l]), ffn1_w_gate[l], ffn1_w_up[l], ffn1_w_down[l])
            m, hn, rn, im_ = mixer(rmsnorm(x, norm_mix[l]), l, hists[l], hres[l], hims[l], pos0)
            x = x + m
            x = x + 0.5 * swiglu(rmsnorm(x, norm_ffn2[l]), ffn2_w_gate[l], ffn2_w_up[l], ffn2_w_down[l])
            new_h.append(hn)
            new_re.append(rn)
            new_im.append(im_)
        y = rmsnorm(x, final_norm)
        return y, jnp.stack(new_h, 0), jnp.stack(new_re, 0), jnp.stack(new_im, 0)

    nbp = x_prompt.shape[0]
    meta = jnp.broadcast_to(meta_tokens.astype(x_prompt.dtype)[None], (nbp, N_META, D_MODEL))
    xp = jnp.concatenate([meta, x_prompt], axis=1)
    zero_hist = jnp.zeros((DEPTH, nbp, POOL_HIST, W_POOL), x_prompt.dtype)
    zero_h = jnp.zeros((DEPTH, nbp, N_SSM_GROUPS, SSM_STATE), jnp.float32)
    yp_full, pool_p, ssm_re_p, ssm_im_p = run_group(xp, zero_hist, zero_h, zero_h, 0)
    y_prompt = yp_full[:, N_META:]

    y_sample, pool_s, ssm_re_s, ssm_im_s = run_group(x_sample, state_pool, state_ssm_re,
                                                     state_ssm_im, PAST_LEN)
    return (y_prompt, y_sample, pool_p, pool_s, ssm_re_p, ssm_im_p, ssm_re_s, ssm_im_s)
```

```python
import functools
import math

import jax
import jax.numpy as jnp
from jax import lax
from jax.experimental import pallas as pl
from jax.experimental.pallas import tpu as pltpu

F32 = jnp.float32
BF16 = jnp.bfloat16

D_MODEL = 2048
BATCH = 4
SEQ = 2048
DEC_BATCH = 128
DEC_SEQ = 4
N_META = 16
D_FF = 5632
W_POOL = 1024
W_SSM = 1024
POOL_WINDOWS = (2, 4, 8, 16)
POOL_GW = 256
POOL_HIST = 15
SSM_GS = 16
N_GROUPS = 64
SSM_STATE = 64
RMS_EPS = 1e-6

N_PROMPT_ROWS = BATCH * SEQ
SAMPLE_ROW0 = N_PROMPT_ROWS
N_SAMPLE_ROWS = DEC_BATCH * DEC_SEQ
META_ROW0 = SAMPLE_ROW0 + N_SAMPLE_ROWS
TM = 512
ROWS = 18 * TM
N_TILES = ROWS // TM
TF = 512
TN_OUT = 512

CHUNK = 16
N_CHUNKS = (SEQ + N_META) // CHUNK
FOLD_ROWS = 528
GROUPS_PER_STEP = 8

VMEM_LIMIT = 60 * 1024 * 1024


def _rms(x, g):
    r = lax.rsqrt(jnp.mean(x * x, axis=-1, keepdims=True) + RMS_EPS)
    return x * r * g


def _ffn_kernel(x_ref, g_ref, wg_ref, wu_ref, wd_ref, fg_ref, o_ref, xn_ref, *, final_norm):
    j = pl.program_id(1)

    @pl.when(j == 0)
    def _():
        x = x_ref[...]
        xn_ref[...] = _rms(x, g_ref[...]).astype(BF16)
        o_ref[...] = x

    xn = xn_ref[...]
    gate = jnp.dot(xn, wg_ref[...], preferred_element_type=F32)
    up = jnp.dot(xn, wu_ref[...], preferred_element_type=F32)
    h = (gate * jax.nn.sigmoid(gate) * up * 0.5).astype(BF16)
    o_ref[...] += jnp.dot(h, wd_ref[...], preferred_element_type=F32)

    if final_norm:
        @pl.when(j == pl.num_programs(1) - 1)
        def _():
            o_ref[...] = _rms(o_ref[...], fg_ref[...])


def _ffn(x, g, wg, wu, wd, fg, final_norm):
    return pl.pallas_call(
        functools.partial(_ffn_kernel, final_norm=final_norm),
        grid=(N_TILES, D_FF // TF),
        in_specs=[
            pl.BlockSpec((TM, D_MODEL), lambda i, j: (i, 0)),
            pl.BlockSpec((1, D_MODEL), lambda i, j: (0, 0)),
            pl.BlockSpec((D_MODEL, TF), lambda i, j: (0, j)),
            pl.BlockSpec((D_MODEL, TF), lambda i, j: (0, j)),
            pl.BlockSpec((TF, D_MODEL), lambda i, j: (j, 0)),
            pl.BlockSpec((1, D_MODEL), lambda i, j: (0, 0)),
        ],
        out_specs=pl.BlockSpec((TM, D_MODEL), lambda i, j: (i, 0)),
        out_shape=jax.ShapeDtypeStruct((ROWS, D_MODEL), F32),
        scratch_shapes=[pltpu.VMEM((TM, D_MODEL), BF16)],
        compiler_params=pltpu.CompilerParams(
            dimension_semantics=("arbitrary", "arbitrary"), vmem_limit_bytes=VMEM_LIMIT),
        name="ffn_final" if final_norm else "ffn",
    )(x, g, wg, wu, wd, fg)


def _inproj_kernel(x_ref, g_ref, w_ref, o_ref, xn_ref):
    @pl.when(pl.program_id(1) == 0)
    def _():
        xn_ref[...] = _rms(x_ref[...], g_ref[...]).astype(BF16)

    o_ref[...] = jnp.dot(xn_ref[...], w_ref[...], preferred_element_type=F32)


def _inproj(x, g, w):
    n = w.shape[1]
    return pl.pallas_call(
        _inproj_kernel,
        grid=(N_TILES, n // 1024),
        in_specs=[
            pl.BlockSpec((TM, D_MODEL), lambda i, j: (i, 0)),
            pl.BlockSpec((1, D_MODEL), lambda i, j: (0, 0)),
            pl.BlockSpec((D_MODEL, 1024), lambda i, j: (0, j)),
        ],
        out_specs=pl.BlockSpec((TM, 1024), lambda i, j: (i, j)),
        out_shape=jax.ShapeDtypeStruct((ROWS, n), F32),
        scratch_shapes=[pltpu.VMEM((TM, D_MODEL), BF16)],
        compiler_params=pltpu.CompilerParams(
            dimension_semantics=("arbitrary", "arbitrary"), vmem_limit_bytes=VMEM_LIMIT),
        name="inproj",
    )(x, g, w)


N_PROMPT_TILES = N_PROMPT_ROWS // TM
TILES_PER_SEQ = SEQ // TM
HALO = 16


def _pool_kernel(u_ref, meta_ref, hist_ref, pw_ref, scale_ref, o_ref, full_ref, mean_ref):
    i = pl.program_id(0)

    @pl.when(i < N_PROMPT_TILES)
    def _():
        @pl.when(i % TILES_PER_SEQ == 0)
        def _():
            full_ref[0:HALO, :] = meta_ref[...]

        full_ref[HALO:HALO + TM, :] = u_ref[...]
        for g, w in enumerate(POOL_WINDOWS):
            cols = slice(g * POOL_GW, (g + 1) * POOL_GW)
            acc = full_ref[HALO:HALO + TM, cols]
            for k in range(1, w):
                acc = acc + full_ref[HALO - k:HALO - k + TM, cols]
            mean_ref[:, cols] = acc * (1.0 / w)
        full_ref[0:HALO, :] = full_ref[TM:TM + HALO, :]

    @pl.when(i == N_PROMPT_TILES)
    def _():
        for g, w in enumerate(POOL_WINDOWS):
            cols = slice(g * POOL_GW, (g + 1) * POOL_GW)
            for t in range(DEC_SEQ):
                acc = None
                for k in range(w):
                    p = POOL_HIST + t - k
                    if p >= POOL_HIST:
                        q = p - POOL_HIST
                        term = u_ref[q * DEC_BATCH:(q + 1) * DEC_BATCH, cols]
                    else:
                        term = hist_ref[p, :, cols]
                    acc = term if acc is None else acc + term
                mean_ref[t * DEC_BATCH:(t + 1) * DEC_BATCH, cols] = acc * (1.0 / w)

    @pl.when(i > N_PROMPT_TILES)
    def _():
        mean_ref[...] = u_ref[...]

    for g in range(len(POOL_WINDOWS)):
        cols = slice(g * POOL_GW, (g + 1) * POOL_GW)
        d = (mean_ref[:, cols] - u_ref[:, cols]).astype(BF16)
        y = jnp.dot(d, pw_ref[g], preferred_element_type=F32) * scale_ref[:, cols]
        o_ref[:, cols] = y.astype(BF16)


def _pool(u, hist_t, pool_w, pool_scale):
    return pl.pallas_call(
        _pool_kernel,
        grid=(N_TILES,),
        in_specs=[
            pl.BlockSpec((TM, W_POOL), lambda i: (i, 0)),
            pl.BlockSpec((N_META, W_POOL), lambda i: (META_ROW0 // N_META, 0)),
            pl.BlockSpec((POOL_HIST, DEC_BATCH, W_POOL), lambda i: (0, 0, 0)),
            pl.BlockSpec((len(POOL_WINDOWS), POOL_GW, POOL_GW), lambda i: (0, 0, 0)),
            pl.BlockSpec((1, W_POOL), lambda i: (0, 0)),
        ],
        out_specs=pl.BlockSpec((TM, W_POOL), lambda i: (i, 0)),
        out_shape=jax.ShapeDtypeStruct((ROWS, W_POOL), BF16),
        scratch_shapes=[pltpu.VMEM((HALO + TM, W_POOL), F32), pltpu.VMEM((TM, W_POOL), F32)],
        compiler_params=pltpu.CompilerParams(
            dimension_semantics=("arbitrary",), vmem_limit_bytes=VMEM_LIMIT),
        name="pool",
    )(u, u, hist_t, pool_w, pool_scale)


def _s5_discretise(lam_re, lam_im, log_dt, b_re, b_im):
    dt = jnp.exp(log_dt)[:, None]
    mag = jnp.exp(lam_re * dt)
    ang = lam_im * dt
    lb_re, lb_im = mag * jnp.cos(ang), mag * jnp.sin(ang)
    den = lam_re * lam_re + lam_im * lam_im
    q_re = ((lb_re - 1.0) * lam_re + lb_im * lam_im) / den
    q_im = (lb_im * lam_re - (lb_re - 1.0) * lam_im) / den
    bb_re = q_re[..., None] * b_re - q_im[..., None] * b_im
    bb_im = q_re[..., None] * b_im + q_im[..., None] * b_re
    return dt, lb_re, lb_im, bb_re, bb_im


def _s5_prompt_tables(lam_re, lam_im, log_dt, b_re, b_im, c_re, c_im, d_skip):
    hi = lax.Precision.HIGHEST
    dt, _, _, bb_re, bb_im = _s5_discretise(lam_re, lam_im, log_dt, b_re, b_im)
    k = jnp.arange(CHUNK + 1, dtype=F32)[:, None, None]
    mag = jnp.exp(k * (lam_re * dt)[None])
    ang = k * (lam_im * dt)[None]
    pw_re, pw_im = mag * jnp.cos(ang), mag * jnp.sin(ang)
    cp_re = c_re[None] * pw_re[:, :, None, :] - c_im[None] * pw_im[:, :, None, :]
    cp_im = c_re[None] * pw_im[:, :, None, :] + c_im[None] * pw_re[:, :, None, :]
    kk = (jnp.einsum("tgcp,gpd->tgcd", cp_re[:CHUNK], bb_re, precision=hi)
          - jnp.einsum("tgcp,gpd->tgcd", cp_im[:CHUNK], bb_im, precision=hi))
    s_idx = jnp.arange(CHUNK)[:, None]
    t_idx = jnp.arange(CHUNK)[None, :]
    lag = t_idx - s_idx
    kst = jnp.where((lag >= 0)[:, :, None, None, None], kk[jnp.clip(lag, 0, CHUNK - 1)], 0.0)
    d_g = d_skip.reshape(N_GROUPS, SSM_GS)
    eye_c = jnp.eye(SSM_GS, dtype=F32)
    kst = kst + (lag == 0)[:, :, None, None, None] * (d_g[:, :, None] * eye_c[None])[None, None]
    m = jnp.transpose(kst, (2, 0, 4, 1, 3)).reshape(N_GROUPS, CHUNK * SSM_GS, CHUNK * SSM_GS)
    pr = pw_re[:CHUNK][::-1][:, :, :, None]
    pi = pw_im[:CHUNK][::-1][:, :, :, None]
    f_re = jnp.transpose(pr * bb_re[None] - pi * bb_im[None], (1, 0, 3, 2)).reshape(N_GROUPS, 256, SSM_STATE)
    f_im = jnp.transpose(pr * bb_im[None] + pi * bb_re[None], (1, 0, 3, 2)).reshape(N_GROUPS, 256, SSM_STATE)
    e_re = jnp.transpose(cp_re[1:], (1, 3, 0, 2)).reshape(N_GROUPS, SSM_STATE, 256)
    e_im = -jnp.transpose(cp_im[1:], (1, 3, 0, 2)).reshape(N_GROUPS, SSM_STATE, 256)

    z_f = jnp.zeros((N_GROUPS // 2, 256, SSM_STATE), F32)
    f0r, f1r, f0i, f1i = f_re[0::2], f_re[1::2], f_im[0::2], f_im[1::2]
    f_pair = jnp.concatenate([
        jnp.concatenate([f0r, z_f, f0i, z_f], axis=2),
        jnp.concatenate([z_f, f1r, z_f, f1i], axis=2)], axis=1)
    z_e = jnp.zeros((N_GROUPS // 2, SSM_STATE, 256), F32)
    e_pair = jnp.concatenate([
        jnp.concatenate([e_re[0::2], z_e], axis=2),
        jnp.concatenate([z_e, e_re[1::2]], axis=2),
        jnp.concatenate([e_im[0::2], z_e], axis=2),
        jnp.concatenate([z_e, e_im[1::2]], axis=2)], axis=1)
    l16_re = pw_re[CHUNK].reshape(1, N_GROUPS * SSM_STATE)
    l16_im = pw_im[CHUNK].reshape(1, N_GROUPS * SSM_STATE)
    return m.astype(BF16), f_pair.astype(BF16), e_pair.astype(BF16), l16_re, l16_im


def _s5_sample_tables(lam_re, lam_im, log_dt, b_re, b_im, c_re, c_im):
    _, lb_re, lb_im, bb_re, bb_im = _s5_discretise(lam_re, lam_im, log_dt, b_re, b_im)
    eye = jnp.eye(8, dtype=F32)

    def bd_b(bb):
        x = bb.reshape(8, 8, SSM_STATE, SSM_GS)
        return jnp.einsum("qgpc,gh->qgchp", x, eye).reshape(8, 128, 512)

    def bd_c(cc):
        x = cc.reshape(8, 8, SSM_GS, SSM_STATE)
        return jnp.einsum("qgcp,gh->qgphc", x, eye).reshape(8, 512, 128)

    b_tab = jnp.concatenate([bd_b(bb_re), bd_b(bb_im)], axis=2)
    c_tab = jnp.concatenate([bd_c(c_re), -bd_c(c_im)], axis=1)
    return (b_tab.astype(BF16), c_tab.astype(BF16),
            lb_re.reshape(1, -1), lb_im.reshape(1, -1))


def _s5_prompt_kernel(u_ref, m_ref, f_ref, e_ref, lre_ref, lim_ref, y_ref, hre_ref, him_ref, vre, vim):
    n_pairs = GROUPS_PER_STEP // 2
    for q in range(n_pairs):
        v = jnp.dot(u_ref[:, q * 512:(q + 1) * 512], f_ref[q], preferred_element_type=F32)
        vre[:, q * 128:(q + 1) * 128] = v[:, :128]
        vim[:, q * 128:(q + 1) * 128] = v[:, 128:]

    a_re = jnp.broadcast_to(lre_ref[...], (BATCH, GROUPS_PER_STEP * SSM_STATE))
    a_im = jnp.broadcast_to(lim_ref[...], (BATCH, GROUPS_PER_STEP * SSM_STATE))
    h_re = jnp.zeros((BATCH, GROUPS_PER_STEP * SSM_STATE), F32)
    h_im = jnp.zeros((BATCH, GROUPS_PER_STEP * SSM_STATE), F32)
    for j in range(N_CHUNKS):
        rows = slice(BATCH * j, BATCH * (j + 1))
        v_re = vre[rows, :]
        v_im = vim[rows, :]
        vre[rows, :] = h_re
        vim[rows, :] = h_im
        h_re, h_im = (a_re * h_re - a_im * h_im + v_re,
                      a_re * h_im + a_im * h_re + v_im)
    hre_ref[...] = h_re
    him_ref[...] = h_im

    for q in range(n_pairs):
        h_in = jnp.concatenate([vre[:, q * 128:(q + 1) * 128], vim[:, q * 128:(q + 1) * 128]], axis=1)
        ys = jnp.dot(h_in.astype(BF16), e_ref[q], preferred_element_type=F32)
        for gi in range(2):
            g = 2 * q + gi
            cols = slice(g * 256, (g + 1) * 256)
            y_ref[:, cols] = (jnp.dot(u_ref[:, cols], m_ref[g], preferred_element_type=F32)
                              + ys[:, gi * 256:(gi + 1) * 256])


def _s5_prompt(u_fold, m, f_pair, e_pair, l16_re, l16_im):
    gps = GROUPS_PER_STEP
    return pl.pallas_call(
        _s5_prompt_kernel,
        grid=(N_GROUPS // gps,),
        in_specs=[
            pl.BlockSpec((FOLD_ROWS, gps * 256), lambda i: (0, i)),
            pl.BlockSpec((gps, 256, 256), lambda i: (i, 0, 0)),
            pl.BlockSpec((gps // 2, 512, 256), lambda i: (i, 0, 0)),
            pl.BlockSpec((gps // 2, 256, 512), lambda i: (i, 0, 0)),
            pl.BlockSpec((1, gps * SSM_STATE), lambda i: (0, i)),
            pl.BlockSpec((1, gps * SSM_STATE), lambda i: (0, i)),
        ],
        out_specs=[
            pl.BlockSpec((FOLD_ROWS, gps * 256), lambda i: (0, i)),
            pl.BlockSpec((BATCH, gps * SSM_STATE), lambda i: (0, i)),
            pl.BlockSpec((BATCH, gps * SSM_STATE), lambda i: (0, i)),
        ],
        out_shape=[
            jax.ShapeDtypeStruct((FOLD_ROWS, N_GROUPS * 256), F32),
            jax.ShapeDtypeStruct((BATCH, N_GROUPS * SSM_STATE), F32),
            jax.ShapeDtypeStruct((BATCH, N_GROUPS * SSM_STATE), F32),
        ],
        scratch_shapes=[pltpu.VMEM((FOLD_ROWS, gps * SSM_STATE), F32),
                        pltpu.VMEM((FOLD_ROWS, gps * SSM_STATE), F32)],
        compiler_params=pltpu.CompilerParams(
            dimension_semantics=("arbitrary",), vmem_limit_bytes=VMEM_LIMIT),
        name="s5_prompt",
    )(u_fold, m, f_pair, e_pair, l16_re, l16_im)


def _s5_sample_kernel(u_ref, h0re_ref, h0im_ref, b_ref, c_ref, lre_ref, lim_ref, d_ref,
                      y_ref, hre_ref, him_ref, hs_ref):
    u = u_ref[...]
    bu = jnp.dot(u.astype(BF16), b_ref[0], preferred_element_type=F32)
    a_re = lre_ref[...]
    a_im = lim_ref[...]
    h_re = h0re_ref[...]
    h_im = h0im_ref[...]
    for t in range(DEC_SEQ):
        rows = slice(t * DEC_BATCH, (t + 1) * DEC_BATCH)
        h_re, h_im = (a_re * h_re - a_im * h_im + bu[rows, :512],
                      a_re * h_im + a_im * h_re + bu[rows, 512:])
        hs_ref[rows, :512] = h_re.astype(BF16)
        hs_ref[rows, 512:] = h_im.astype(BF16)
    hre_ref[...] = h_re
    him_ref[...] = h_im
    y_ref[...] = jnp.dot(hs_ref[...], c_ref[0], preferred_element_type=F32) + d_ref[...] * u


def _s5_sample(u, h0_re, h0_im, b_tab, c_tab, lb_re, lb_im, d_skip):
    ucol0 = W_POOL // 128
    return pl.pallas_call(
        _s5_sample_kernel,
        grid=(8,),
        in_specs=[
            pl.BlockSpec((N_SAMPLE_ROWS, 128), lambda i: (SAMPLE_ROW0 // N_SAMPLE_ROWS, ucol0 + i)),
            pl.BlockSpec((DEC_BATCH, 512), lambda i: (0, i)),
            pl.BlockSpec((DEC_BATCH, 512), lambda i: (0, i)),
            pl.BlockSpec((1, 128, 1024), lambda i: (i, 0, 0)),
            pl.BlockSpec((1, 1024, 128), lambda i: (i, 0, 0)),
            pl.BlockSpec((1, 512), lambda i: (0, i)),
            pl.BlockSpec((1, 512), lambda i: (0, i)),
            pl.BlockSpec((1, 128), lambda i: (0, i)),
        ],
        out_specs=[
            pl.BlockSpec((N_SAMPLE_ROWS, 128), lambda i: (0, i)),
            pl.BlockSpec((DEC_BATCH, 512), lambda i: (0, i)),
            pl.BlockSpec((DEC_BATCH, 512), lambda i: (0, i)),
        ],
        out_shape=[
            jax.ShapeDtypeStruct((N_SAMPLE_ROWS, W_SSM), F32),
            jax.ShapeDtypeStruct((DEC_BATCH, N_GROUPS * SSM_STATE), F32),
            jax.ShapeDtypeStruct((DEC_BATCH, N_GROUPS * SSM_STATE), F32),
        ],
        scratch_shapes=[pltpu.VMEM((N_SAMPLE_ROWS, 1024), BF16)],
        compiler_params=pltpu.CompilerParams(
            dimension_semantics=("arbitrary",), vmem_limit_bytes=VMEM_LIMIT),
        name="s5_sample",
    )(u, h0_re, h0_im, b_tab, c_tab, lb_re, lb_im, d_skip)


def _out_kernel(x_ref, a_ref, sy_ref, g_ref, wga_ref, wgb_ref, bga_ref, bgb_ref, gluw_ref, glub_ref,
                wa_ref, wb_ref, wo_ref, o_ref, xn_ref, s_ref):
    j = pl.program_id(1)

    @pl.when(j == 0)
    def _():
        x = x_ref[...]
        xn_ref[...] = _rms(x, g_ref[...]).astype(BF16)
        o_ref[...] = x
        s = jax.nn.gelu(sy_ref[...])
        z = jnp.dot(s.astype(BF16), gluw_ref[...], preferred_element_type=F32) + glub_ref[...]
        s_ref[...] = (s * jax.nn.sigmoid(z)).astype(BF16)

    xn = xn_ref[...]
    ga = jax.nn.sigmoid(jnp.dot(xn, wga_ref[...], preferred_element_type=F32) + bga_ref[...])
    gb = jax.nn.sigmoid(jnp.dot(xn, wgb_ref[...], preferred_element_type=F32) + bgb_ref[...])
    merged = (ga * jnp.dot(a_ref[...], wa_ref[...], preferred_element_type=F32)
              + gb * jnp.dot(s_ref[...], wb_ref[...], preferred_element_type=F32))
    o_ref[...] += jnp.dot(merged.astype(BF16), wo_ref[...], preferred_element_type=F32)


def _out_proj(x, a, sy, g, w_gates, b_gate, glu_w, glu_b, w_a, w_b, w_out):
    nb = D_MODEL // TN_OUT
    return pl.pallas_call(
        _out_kernel,
        grid=(N_TILES, nb),
        in_specs=[
            pl.BlockSpec((TM, D_MODEL), lambda i, j: (i, 0)),
            pl.BlockSpec((TM, W_POOL), lambda i, j: (i, 0)),
            pl.BlockSpec((TM, W_SSM), lambda i, j: (i, 0)),
            pl.BlockSpec((1, D_MODEL), lambda i, j: (0, 0)),
            pl.BlockSpec((D_MODEL, TN_OUT), lambda i, j: (0, j)),
            pl.BlockSpec((D_MODEL, TN_OUT), lambda i, j: (0, nb + j)),
            pl.BlockSpec((1, TN_OUT), lambda i, j: (0, j)),
            pl.BlockSpec((1, TN_OUT), lambda i, j: (0, nb + j)),
            pl.BlockSpec((W_SSM, W_SSM), lambda i, j: (0, 0)),
            pl.BlockSpec((1, W_SSM), lambda i, j: (0, 0)),
            pl.BlockSpec((W_POOL, TN_OUT), lambda i, j: (0, j)),
            pl.BlockSpec((W_SSM, TN_OUT), lambda i, j: (0, j)),
            pl.BlockSpec((TN_OUT, D_MODEL), lambda i, j: (j, 0)),
        ],
        out_specs=pl.BlockSpec((TM, D_MODEL), lambda i, j: (i, 0)),
        out_shape=jax.ShapeDtypeStruct((ROWS, D_MODEL), F32),
        scratch_shapes=[pltpu.VMEM((TM, D_MODEL), BF16), pltpu.VMEM((TM, W_SSM), BF16)],
        compiler_params=pltpu.CompilerParams(
            dimension_semantics=("arbitrary", "arbitrary"), vmem_limit_bytes=VMEM_LIMIT),
        name="out_proj",
    )(x, a, sy, g, w_gates, w_gates, b_gate, b_gate, glu_w, glu_b, w_a, w_b, w_out)


def kernel(x_prompt, x_sample, state_pool, state_ssm_re, state_ssm_im, meta_tokens, norm_ffn1, ffn1_w_gate, ffn1_w_up, ffn1_w_down, norm_mix, w_in, b_gate, pool_w, pool_scale, ssm_lambda_re, ssm_lambda_im, ssm_log_dt, ssm_b_re, ssm_b_im, ssm_c_re, ssm_c_im, ssm_d, glu_w, glu_b, w_branch_a, w_branch_b, w_out, norm_ffn2, ffn2_w_gate, ffn2_w_up, ffn2_w_down, final_norm):
    l = 0
    bf = lambda w: w.astype(BF16)
    row = lambda v: v.reshape(1, -1).astype(F32)

    x_all = jnp.concatenate([
        x_prompt.reshape(N_PROMPT_ROWS, D_MODEL),
        jnp.transpose(x_sample, (1, 0, 2)).reshape(N_SAMPLE_ROWS, D_MODEL),
        meta_tokens.astype(F32),
        jnp.zeros((ROWS - META_ROW0 - N_META, D_MODEL), F32)], axis=0)

    fg = row(final_norm)
    x1 = _ffn(x_all, row(norm_ffn1[l]), bf(ffn1_w_gate[l]), bf(ffn1_w_up[l]), bf(ffn1_w_down[l]), fg, False)

    w_in_b = bf(w_in[l])
    u = _inproj(x1, row(norm_mix[l]), w_in_b[:, :W_POOL + W_SSM])

    hist_t = jnp.transpose(state_pool[l], (1, 0, 2))
    a_out = _pool(u, hist_t, bf(pool_w[l]), row(pool_scale[l]))

    tabs = (ssm_lambda_re[l], ssm_lambda_im[l], ssm_log_dt[l], ssm_b_re[l], ssm_b_im[l], ssm_c_re[l], ssm_c_im[l])
    m_tab, f_tab, e_tab, l16_re, l16_im = _s5_prompt_tables(*tabs, ssm_d[l])
    u_ssm_p = u[:N_PROMPT_ROWS, W_POOL:].reshape(BATCH, SEQ, W_SSM)
    u_ssm_m = jnp.broadcast_to(u[META_ROW0:META_ROW0 + N_META, W_POOL:][None], (BATCH, N_META, W_SSM))
    u_seq = jnp.concatenate([u_ssm_m, u_ssm_p], axis=1)
    u_fold = u_seq.reshape(BATCH, N_CHUNKS, CHUNK, N_GROUPS, SSM_GS)
    u_fold = jnp.transpose(u_fold, (1, 0, 3, 2, 4)).reshape(N_CHUNKS * BATCH, N_GROUPS * 256)
    u_fold = jnp.pad(u_fold.astype(BF16), ((0, FOLD_ROWS - N_CHUNKS * BATCH), (0, 0)))
    y_fold, hp_re, hp_im = _s5_prompt(u_fold, m_tab, f_tab, e_tab, l16_re, l16_im)
    y_p = y_fold[BATCH:N_CHUNKS * BATCH].reshape(N_CHUNKS - 1, BATCH, N_GROUPS, CHUNK, SSM_GS)
    y_p = jnp.transpose(y_p, (1, 0, 3, 2, 4)).reshape(N_PROMPT_ROWS, W_SSM)

    b_tab, c_tab, lb_re, lb_im = _s5_sample_tables(*tabs)
    y_s, hs_re, hs_im = _s5_sample(
        u, state_ssm_re[l].reshape(DEC_BATCH, -1), state_ssm_im[l].reshape(DEC_BATCH, -1),
        b_tab, c_tab, lb_re, lb_im, row(ssm_d[l]))

    sy = jnp.concatenate([y_p, y_s, jnp.zeros((ROWS - META_ROW0, W_SSM), F32)], axis=0)

    x2 = _out_proj(x1, a_out, sy, row(norm_mix[l]), w_in_b[:, W_POOL + W_SSM:], row(b_gate[l]),
                   bf(glu_w[l]), row(glu_b[l]), bf(w_branch_a[l]), bf(w_branch_b[l]), bf(w_out[l]))
    y = _ffn(x2, row(norm_ffn2[l]), bf(ffn2_w_gate[l]), bf(ffn2_w_up[l]), bf(ffn2_w_down[l]), fg, True)

    y_prompt = y[:N_PROMPT_ROWS].reshape(BATCH, SEQ, D_MODEL)
    y_sample = jnp.transpose(y[SAMPLE_ROW0:META_ROW0].reshape(DEC_SEQ, DEC_BATCH, D_MODEL), (1, 0, 2))
    u_pool_p = u[:N_PROMPT_ROWS, :W_POOL].reshape(BATCH, SEQ, W_POOL)
    pool_p = u_pool_p[:, SEQ - POOL_HIST:][None]
    u_pool_s = jnp.transpose(u[SAMPLE_ROW0:META_ROW0, :W_POOL].reshape(DEC_SEQ, DEC_BATCH, W_POOL), (1, 0, 2))
    pool_s = jnp.concatenate([state_pool[l][:, DEC_SEQ:], u_pool_s], axis=1)[None]
    shp_p = (1, BATCH, N_GROUPS, SSM_STATE)
    shp_s = (1, DEC_BATCH, N_GROUPS, SSM_STATE)
    return (y_prompt, y_sample, pool_p, pool_s,
            hp_re.reshape(shp_p), hp_im.reshape(shp_p), hs_re.reshape(shp_s), hs_im.reshape(shp_s))
```

```python
import functools

import jax
import jax.numpy as jnp
from jax import lax
from jax.experimental import pallas as pl
from jax.experimental.pallas import tpu as pltpu

F32 = jnp.float32
BF16 = jnp.bfloat16

D_MODEL = 2048
BATCH = 4
SEQ = 2048
DEC_BATCH = 128
DEC_SEQ = 4
N_META = 16
D_FF = 5632
W_POOL = 1024
W_SSM = 1024
POOL_WINDOWS = (2, 4, 8, 16)
POOL_GW = 256
POOL_HIST = 15
SSM_GS = 16
N_GROUPS = 64
SSM_STATE = 64
RMS_EPS = 1e-6

N_PROMPT_ROWS = BATCH * SEQ
SAMPLE_ROW0 = N_PROMPT_ROWS
N_SAMPLE_ROWS = DEC_BATCH * DEC_SEQ
META_ROW0 = SAMPLE_ROW0 + N_SAMPLE_ROWS
TM = 512
ROWS = 18 * TM
N_TILES = ROWS // TM
TF = 512
TN_OUT = 512

LANES = 128
N_SLABS = W_SSM // LANES
SLAB_STATE = (LANES // SSM_GS) * SSM_STATE
CHUNK = 16
N_CHUNKS = (SEQ + N_META) // CHUNK
SEQ_STRIDE = 144
FOLD_ROW0 = 16
FOLD_ROWS = BATCH * SEQ_STRIDE

VMEM_LIMIT = 60 * 1024 * 1024


def _rms(x, g):
    r = lax.rsqrt(jnp.mean(x * x, axis=-1, keepdims=True) + RMS_EPS)
    return x * r * g


def _ffn_kernel(x_ref, g_ref, wg_ref, wu_ref, wd_ref, fg_ref, o_ref, xn_ref, *, final_norm):
    j = pl.program_id(1)

    @pl.when(j == 0)
    def _():
        x = x_ref[...]
        xn_ref[...] = _rms(x, g_ref[...]).astype(BF16)
        o_ref[...] = x

    xn = xn_ref[...]
    gate = jnp.dot(xn, wg_ref[...], preferred_element_type=F32)
    up = jnp.dot(xn, wu_ref[...], preferred_element_type=F32)
    h = (gate * jax.nn.sigmoid(gate) * up * 0.5).astype(BF16)
    o_ref[...] += jnp.dot(h, wd_ref[...], preferred_element_type=F32)

    if final_norm:
        @pl.when(j == pl.num_programs(1) - 1)
        def _():
            o_ref[...] = _rms(o_ref[...], fg_ref[...])


def _ffn(x, g, wg, wu, wd, fg, final_norm):
    return pl.pallas_call(
        functools.partial(_ffn_kernel, final_norm=final_norm),
        grid=(N_TILES, D_FF // TF),
        in_specs=[
            pl.BlockSpec((TM, D_MODEL), lambda i, j: (i, 0)),
            pl.BlockSpec((1, D_MODEL), lambda i, j: (0, 0)),
            pl.BlockSpec((D_MODEL, TF), lambda i, j: (0, j)),
            pl.BlockSpec((D_MODEL, TF), lambda i, j: (0, j)),
            pl.BlockSpec((TF, D_MODEL), lambda i, j: (j, 0)),
            pl.BlockSpec((1, D_MODEL), lambda i, j: (0, 0)),
        ],
        out_specs=pl.BlockSpec((TM, D_MODEL), lambda i, j: (i, 0)),
        out_shape=jax.ShapeDtypeStruct((ROWS, D_MODEL), F32),
        scratch_shapes=[pltpu.VMEM((TM, D_MODEL), BF16)],
        compiler_params=pltpu.CompilerParams(
            dimension_semantics=("arbitrary", "arbitrary"), vmem_limit_bytes=VMEM_LIMIT),
        name="ffn_final" if final_norm else "ffn",
    )(x, g, wg, wu, wd, fg)


def _inproj_kernel(x_ref, g_ref, w_ref, o_ref, xn_ref):
    @pl.when(pl.program_id(1) == 0)
    def _():
        xn_ref[...] = _rms(x_ref[...], g_ref[...]).astype(BF16)

    o_ref[...] = jnp.dot(xn_ref[...], w_ref[...], preferred_element_type=F32)


def _inproj(x, g, w_in):
    n = W_POOL + W_SSM
    return pl.pallas_call(
        _inproj_kernel,
        grid=(N_TILES, n // 1024),
        in_specs=[
            pl.BlockSpec((TM, D_MODEL), lambda i, j: (i, 0)),
            pl.BlockSpec((1, D_MODEL), lambda i, j: (0, 0)),
            pl.BlockSpec((D_MODEL, 1024), lambda i, j: (0, j)),
        ],
        out_specs=pl.BlockSpec((TM, 1024), lambda i, j: (i, j)),
        out_shape=jax.ShapeDtypeStruct((ROWS, n), F32),
        scratch_shapes=[pltpu.VMEM((TM, D_MODEL), BF16)],
        compiler_params=pltpu.CompilerParams(
            dimension_semantics=("arbitrary", "arbitrary"), vmem_limit_bytes=VMEM_LIMIT),
        name="inproj",
    )(x, g, w_in)


N_PROMPT_TILES = N_PROMPT_ROWS // TM
TILES_PER_SEQ = SEQ // TM
HALO = 16


def _pool_kernel(u_ref, meta_ref, hist_ref, pw_ref, scale_ref, o_ref, full_ref, mean_ref):
    i = pl.program_id(0)

    @pl.when(i < N_PROMPT_TILES)
    def _():
        @pl.when(i % TILES_PER_SEQ == 0)
        def _():
            full_ref[0:HALO, :] = meta_ref[...]

        full_ref[HALO:HALO + TM, :] = u_ref[...]
        for g, w in enumerate(POOL_WINDOWS):
            cols = slice(g * POOL_GW, (g + 1) * POOL_GW)
            acc = full_ref[HALO:HALO + TM, cols]
            for k in range(1, w):
                acc = acc + full_ref[HALO - k:HALO - k + TM, cols]
            mean_ref[:, cols] = acc * (1.0 / w)
        full_ref[0:HALO, :] = full_ref[TM:TM + HALO, :]

    @pl.when(i == N_PROMPT_TILES)
    def _():
        for g, w in enumerate(POOL_WINDOWS):
            cols = slice(g * POOL_GW, (g + 1) * POOL_GW)
            for t in range(DEC_SEQ):
                acc = None
                for k in range(w):
                    p = POOL_HIST + t - k
                    if p >= POOL_HIST:
                        q = p - POOL_HIST
                        term = u_ref[q * DEC_BATCH:(q + 1) * DEC_BATCH, cols]
                    else:
                        term = hist_ref[p, :, cols]
                    acc = term if acc is None else acc + term
                mean_ref[t * DEC_BATCH:(t + 1) * DEC_BATCH, cols] = acc * (1.0 / w)

    @pl.when(i > N_PROMPT_TILES)
    def _():
        mean_ref[...] = u_ref[...]

    for g in range(len(POOL_WINDOWS)):
        cols = slice(g * POOL_GW, (g + 1) * POOL_GW)
        d = (mean_ref[:, cols] - u_ref[:, cols]).astype(BF16)
        y = jnp.dot(d, pw_ref[g], preferred_element_type=F32) * scale_ref[:, cols]
        o_ref[:, cols] = y.astype(BF16)


def _pool(u, hist_t, pool_w, pool_scale):
    return pl.pallas_call(
        _pool_kernel,
        grid=(N_TILES,),
        in_specs=[
            pl.BlockSpec((TM, W_POOL), lambda i: (i, 0)),
            pl.BlockSpec((N_META, W_POOL), lambda i: (META_ROW0 // N_META, 0)),
            pl.BlockSpec((POOL_HIST, DEC_BATCH, W_POOL), lambda i: (0, 0, 0)),
            pl.BlockSpec((len(POOL_WINDOWS), POOL_GW, POOL_GW), lambda i: (0, 0, 0)),
            pl.BlockSpec((1, W_POOL), lambda i: (0, 0)),
        ],
        out_specs=pl.BlockSpec((TM, W_POOL), lambda i: (i, 0)),
        out_shape=jax.ShapeDtypeStruct((ROWS, W_POOL), BF16),
        scratch_shapes=[pltpu.VMEM((HALO + TM, W_POOL), F32), pltpu.VMEM((TM, W_POOL), F32)],
        compiler_params=pltpu.CompilerParams(
            dimension_semantics=("arbitrary",), vmem_limit_bytes=VMEM_LIMIT),
        name="pool",
    )(u, u, hist_t, pool_w, pool_scale)


def _s5_tables(lam_re, lam_im, log_dt, b_re, b_im, c_re, c_im, d_skip):
    hi = lax.Precision.HIGHEST
    dt = jnp.exp(log_dt)[:, None]
    k = jnp.arange(CHUNK + 1, dtype=F32)[:, None, None]
    mag = jnp.exp(k * (lam_re * dt)[None])
    ang = k * (lam_im * dt)[None]
    pw_re, pw_im = mag * jnp.cos(ang), mag * jnp.sin(ang)
    lb_re, lb_im = pw_re[1], pw_im[1]
    den = lam_re * lam_re + lam_im * lam_im
    q_re = ((lb_re - 1.0) * lam_re + lb_im * lam_im) / den
    q_im = (lb_im * lam_re - (lb_re - 1.0) * lam_im) / den
    bb_re = q_re[..., None] * b_re - q_im[..., None] * b_im
    bb_im = q_re[..., None] * b_im + q_im[..., None] * b_re
    cp_re = c_re[None] * pw_re[:CHUNK, :, None, :] - c_im[None] * pw_im[:CHUNK, :, None, :]
    cp_im = c_re[None] * pw_im[:CHUNK, :, None, :] + c_im[None] * pw_re[:CHUNK, :, None, :]
    kk = (jnp.einsum("tgcp,gpd->tgcd", cp_re, bb_re, precision=hi)
          - jnp.einsum("tgcp,gpd->tgcd", cp_im, bb_im, precision=hi))
    d_diag = d_skip.reshape(N_GROUPS, SSM_GS)[:, :, None] * jnp.eye(SSM_GS, dtype=F32)[None]
    kk = kk.at[0].add(d_diag)
    eye = jnp.eye(LANES // SSM_GS, dtype=F32)
    gl = LANES // SSM_GS
    kb = jnp.einsum("tqgcd,gh->qtgdhc", kk.reshape(CHUNK, N_SLABS, gl, SSM_GS, SSM_GS), eye)
    kb = kb.reshape(N_SLABS, CHUNK, LANES, LANES).astype(BF16)

    def bd_b(bb):
        return jnp.einsum("qgpc,gh->qgchp", bb.reshape(N_SLABS, gl, SSM_STATE, SSM_GS), eye).reshape(
            N_SLABS, LANES, SLAB_STATE)

    def bd_c(cc):
        return jnp.einsum("qgcp,gh->qgphc", cc.reshape(N_SLABS, gl, SSM_GS, SSM_STATE), eye).reshape(
            N_SLABS, SLAB_STATE, LANES)

    b_t = jnp.concatenate([bd_b(bb_re), bd_b(bb_im)], axis=2)
    c_t = jnp.concatenate([bd_c(c_re), bd_c(c_im)], axis=1)
    pr = pw_re.reshape(CHUNK + 1, N_SLABS, SLAB_STATE)
    pi = pw_im.reshape(CHUNK + 1, N_SLABS, SLAB_STATE)
    pwl = jnp.concatenate([jnp.transpose(pr, (1, 0, 2)), jnp.transpose(pi, (1, 0, 2))], axis=2)
    pwc = jnp.concatenate([jnp.transpose(pr, (1, 2, 0)), jnp.transpose(pi, (1, 2, 0))], axis=2)
    return kb, b_t, c_t, pwl, pwc


def _s5_kernel(up_ref, um_ref, us_ref, h0re_ref, h0im_ref, kb_ref, b_ref, c_ref, pwl_ref, pwc_ref, d_ref,
               y_ref, hpre_ref, hpim_ref, hsre_ref, hsim_ref,
               l_ref, f_ref, m_ref, e_ref, vre_ref, vim_ref, yf_ref, hs_ref):
    ns = SLAB_STATE
    b_re = b_ref[0][:, :ns]
    b_im = b_ref[0][:, ns:]

    l_ref[...] = jnp.zeros_like(l_ref)
    for b in range(BATCH):
        r0 = b * SEQ_STRIDE + FOLD_ROW0
        for s in range(CHUNK):
            cols = slice(s * LANES, (s + 1) * LANES)
            l_ref[r0:r0 + SEQ // CHUNK, cols] = up_ref[pl.ds(b * SEQ + s, SEQ // CHUNK, stride=CHUNK), :]
            l_ref[r0 - 1:r0, cols] = um_ref[s:s + 1, :]
    lb = l_ref[...].astype(BF16)

    for s in range(CHUNK):
        p = pwl_ref[0, CHUNK - 1 - s:CHUNK - s, :]
        p_re, p_im = p[:, :ns], p[:, ns:]
        rows = slice(s * LANES, (s + 1) * LANES)
        f_ref[rows, :ns] = (b_re * p_re - b_im * p_im).astype(BF16)
        f_ref[rows, ns:] = (b_re * p_im + b_im * p_re).astype(BF16)
    v = jnp.dot(lb, f_ref[...], preferred_element_type=F32)
    nk = ns // LANES
    for k in range(nk):
        vre_ref[k] = v[:, k * LANES:(k + 1) * LANES]
        vim_ref[k] = v[:, ns + k * LANES:ns + (k + 1) * LANES]

    a = pwl_ref[0, CHUNK:CHUNK + 1, :]
    a_re = [jnp.broadcast_to(a[:, k * LANES:(k + 1) * LANES], (BATCH, LANES)) for k in range(nk)]
    a_im = [jnp.broadcast_to(a[:, ns + k * LANES:ns + (k + 1) * LANES], (BATCH, LANES)) for k in range(nk)]
    h_re = [jnp.zeros((BATCH, LANES), F32) for _ in range(nk)]
    h_im = [jnp.zeros((BATCH, LANES), F32) for _ in range(nk)]
    for j in range(N_CHUNKS):
        rows = pl.ds(FOLD_ROW0 - 1 + j, BATCH, stride=SEQ_STRIDE)
        for k in range(nk):
            v_re = vre_ref[k, rows, :]
            v_im = vim_ref[k, rows, :]
            vre_ref[k, rows, :] = h_re[k]
            vim_ref[k, rows, :] = h_im[k]
            h_re[k], h_im[k] = (a_re[k] * h_re[k] - a_im[k] * h_im[k] + v_re,
                                a_re[k] * h_im[k] + a_im[k] * h_re[k] + v_im)
    for k in range(nk):
        hpre_ref[:, k * LANES:(k + 1) * LANES] = h_re[k]
        hpim_ref[:, k * LANES:(k + 1) * LANES] = h_im[k]

    half = CHUNK // 2
    for s in range(CHUNK):
        for t in range(half, CHUNK):
            blk = kb_ref[0, t - s] if t >= s else jnp.zeros((LANES, LANES), BF16)
            m_ref[s * LANES:(s + 1) * LANES, (t - half) * LANES:(t - half + 1) * LANES] = blk
    y_left = jnp.dot(lb[:, :half * LANES], m_ref[half * LANES:, :], preferred_element_type=F32)
    y_right = jnp.dot(lb, m_ref[...], preferred_element_type=F32)

    c_re = c_ref[0][:ns, :]
    c_im = c_ref[0][ns:, :]
    for t in range(CHUNK):
        p_re = pwc_ref[0][:, t + 1:t + 2]
        p_im = pwc_ref[0][:, CHUNK + 1 + t + 1:CHUNK + 1 + t + 2]
        cols = slice(t * LANES, (t + 1) * LANES)
        e_ref[:ns, cols] = (c_re * p_re - c_im * p_im).astype(BF16)
        e_ref[ns:, cols] = (-(c_re * p_im + c_im * p_re)).astype(BF16)
    h_in = jnp.concatenate([vre_ref[k] for k in range(nk)] + [vim_ref[k] for k in range(nk)],
                           axis=1).astype(BF16)
    ys = jnp.dot(h_in, e_ref[...], preferred_element_type=F32)
    yf_ref[:, :half * LANES] = y_left + ys[:, :half * LANES]
    yf_ref[:, half * LANES:] = y_right + ys[:, half * LANES:]
    for b in range(BATCH):
        r0 = b * SEQ_STRIDE + FOLD_ROW0
        for t in range(CHUNK):
            y_ref[pl.ds(b * SEQ + t, SEQ // CHUNK, stride=CHUNK), :] = (
                yf_ref[r0:r0 + SEQ // CHUNK, t * LANES:(t + 1) * LANES])

    u = us_ref[...]
    bu = jnp.dot(u.astype(BF16), b_ref[0].astype(BF16), preferred_element_type=F32)
    l1 = pwl_ref[0, 1:2, :]
    l_re, l_im = l1[:, :ns], l1[:, ns:]
    g_re = h0re_ref[...]
    g_im = h0im_ref[...]
    for t in range(DEC_SEQ):
        rows = slice(t * DEC_BATCH, (t + 1) * DEC_BATCH)
        g_re, g_im = (l_re * g_re - l_im * g_im + bu[rows, :ns],
                      l_re * g_im + l_im * g_re + bu[rows, ns:])
        hs_ref[rows, :ns] = g_re.astype(BF16)
        hs_ref[rows, ns:] = (-g_im).astype(BF16)
    hsre_ref[...] = g_re
    hsim_ref[...] = g_im
    y_ref[SAMPLE_ROW0:META_ROW0, :] = (
        jnp.dot(hs_ref[...], c_ref[0].astype(BF16), preferred_element_type=F32) + d_ref[...] * u)
    y_ref[META_ROW0:, :] = jnp.zeros((ROWS - META_ROW0, LANES), F32)


def _s5(u, h0_re, h0_im, kb, b_t, c_t, pwl, pwc, d_skip):
    c0 = W_POOL // LANES
    ns = SLAB_STATE
    return pl.pallas_call(
        _s5_kernel,
        grid=(N_SLABS,),
        in_specs=[
            pl.BlockSpec((N_PROMPT_ROWS, LANES), lambda i: (0, c0 + i)),
            pl.BlockSpec((N_META, LANES), lambda i: (META_ROW0 // N_META, c0 + i)),
            pl.BlockSpec((N_SAMPLE_ROWS, LANES), lambda i: (SAMPLE_ROW0 // N_SAMPLE_ROWS, c0 + i)),
            pl.BlockSpec((DEC_BATCH, ns), lambda i: (0, i)),
            pl.BlockSpec((DEC_BATCH, ns), lambda i: (0, i)),
            pl.BlockSpec((1, CHUNK, LANES, LANES), lambda i: (i, 0, 0, 0)),
            pl.BlockSpec((1, LANES, 2 * ns), lambda i: (i, 0, 0)),
            pl.BlockSpec((1, 2 * ns, LANES), lambda i: (i, 0, 0)),
            pl.BlockSpec((1, CHUNK + 1, 2 * ns), lambda i: (i, 0, 0)),
            pl.BlockSpec((1, ns, 2 * (CHUNK + 1)), lambda i: (i, 0, 0)),
            pl.BlockSpec((1, LANES), lambda i: (0, i)),
        ],
        out_specs=[
            pl.BlockSpec((ROWS, LANES), lambda i: (0, i)),
            pl.BlockSpec((BATCH, ns), lambda i: (0, i)),
            pl.BlockSpec((BATCH, ns), lambda i: (0, i)),
            pl.BlockSpec((DEC_BATCH, ns), lambda i: (0, i)),
            pl.BlockSpec((DEC_BATCH, ns), lambda i: (0, i)),
        ],
        out_shape=[
            jax.ShapeDtypeStruct((ROWS, W_SSM), F32),
            jax.ShapeDtypeStruct((BATCH, N_GROUPS * SSM_STATE), F32),
            jax.ShapeDtypeStruct((BATCH, N_GROUPS * SSM_STATE), F32),
            jax.ShapeDtypeStruct((DEC_BATCH, N_GROUPS * SSM_STATE), F32),
            jax.ShapeDtypeStruct((DEC_BATCH, N_GROUPS * SSM_STATE), F32),
        ],
        scratch_shapes=[
            pltpu.VMEM((FOLD_ROWS, CHUNK * LANES), F32),
            pltpu.VMEM((CHUNK * LANES, 2 * ns), BF16),
            pltpu.VMEM((CHUNK * LANES, CHUNK * LANES // 2), BF16),
            pltpu.VMEM((2 * ns, CHUNK * LANES), BF16),
            pltpu.VMEM((ns // LANES, FOLD_ROWS, LANES), F32),
            pltpu.VMEM((ns // LANES, FOLD_ROWS, LANES), F32),
            pltpu.VMEM((FOLD_ROWS, CHUNK * LANES), F32),
            pltpu.VMEM((N_SAMPLE_ROWS, 2 * ns), BF16),
        ],
        compiler_params=pltpu.CompilerParams(
            dimension_semantics=("arbitrary",), vmem_limit_bytes=VMEM_LIMIT),
        name="s5",
    )(u, u, u, h0_re, h0_im, kb, b_t, c_t, pwl, pwc, d_skip)


def _out_kernel(x_ref, a_ref, sy_ref, g_ref, wga_ref, wgb_ref, bga_ref, bgb_ref, gluw_ref, glub_ref,
                wa_ref, wb_ref, wo_ref, o_ref, xn_ref, s_ref):
    j = pl.program_id(1)

    @pl.when(j == 0)
    def _():
        x = x_ref[...]
        xn_ref[...] = _rms(x, g_ref[...]).astype(BF16)
        o_ref[...] = x
        s = jax.nn.gelu(sy_ref[...])
        z = jnp.dot(s.astype(BF16), gluw_ref[...], preferred_element_type=F32) + glub_ref[...]
        s_ref[...] = (s * jax.nn.sigmoid(z)).astype(BF16)

    xn = xn_ref[...]
    ga = jax.nn.sigmoid(jnp.dot(xn, wga_ref[...], preferred_element_type=F32) + bga_ref[...])
    gb = jax.nn.sigmoid(jnp.dot(xn, wgb_ref[...], preferred_element_type=F32) + bgb_ref[...])
    merged = (ga * jnp.dot(a_ref[...], wa_ref[...], preferred_element_type=F32)
              + gb * jnp.dot(s_ref[...], wb_ref[...], preferred_element_type=F32))
    o_ref[...] += jnp.dot(merged.astype(BF16), wo_ref[...], preferred_element_type=F32)


def _out_proj(x, a, sy, g, w_in, b_gate, glu_w, glu_b, w_a, w_b, w_out):
    nb = D_MODEL // TN_OUT
    g0 = (W_POOL + W_SSM) // TN_OUT
    return pl.pallas_call(
        _out_kernel,
        grid=(N_TILES, nb),
        in_specs=[
            pl.BlockSpec((TM, D_MODEL), lambda i, j: (i, 0)),
            pl.BlockSpec((TM, W_POOL), lambda i, j: (i, 0)),
            pl.BlockSpec((TM, W_SSM), lambda i, j: (i, 0)),
            pl.BlockSpec((1, D_MODEL), lambda i, j: (0, 0)),
            pl.BlockSpec((D_MODEL, TN_OUT), lambda i, j: (0, g0 + j)),
            pl.BlockSpec((D_MODEL, TN_OUT), lambda i, j: (0, g0 + nb + j)),
            pl.BlockSpec((1, TN_OUT), lambda i, j: (0, j)),
            pl.BlockSpec((1, TN_OUT), lambda i, j: (0, nb + j)),
            pl.BlockSpec((W_SSM, W_SSM), lambda i, j: (0, 0)),
            pl.BlockSpec((1, W_SSM), lambda i, j: (0, 0)),
            pl.BlockSpec((W_POOL, TN_OUT), lambda i, j: (0, j)),
            pl.BlockSpec((W_SSM, TN_OUT), lambda i, j: (0, j)),
            pl.BlockSpec((TN_OUT, D_MODEL), lambda i, j: (j, 0)),
        ],
        out_specs=pl.BlockSpec((TM, D_MODEL), lambda i, j: (i, 0)),
        out_shape=jax.ShapeDtypeStruct((ROWS, D_MODEL), F32),
        scratch_shapes=[pltpu.VMEM((TM, D_MODEL), BF16), pltpu.VMEM((TM, W_SSM), BF16)],
        compiler_params=pltpu.CompilerParams(
            dimension_semantics=("arbitrary", "arbitrary"), vmem_limit_bytes=VMEM_LIMIT),
        name="out_proj",
    )(x, a, sy, g, w_in, w_in, b_gate, b_gate, glu_w, glu_b, w_a, w_b, w_out)


def kernel(x_prompt, x_sample, state_pool, state_ssm_re, state_ssm_im, meta_tokens, norm_ffn1, ffn1_w_gate, ffn1_w_up, ffn1_w_down, norm_mix, w_in, b_gate, pool_w, pool_scale, ssm_lambda_re, ssm_lambda_im, ssm_log_dt, ssm_b_re, ssm_b_im, ssm_c_re, ssm_c_im, ssm_d, glu_w, glu_b, w_branch_a, w_branch_b, w_out, norm_ffn2, ffn2_w_gate, ffn2_w_up, ffn2_w_down, final_norm):
    l = 0
    bf = lambda w: w.astype(BF16)
    row = lambda v: v.reshape(1, -1).astype(F32)

    x_all = jnp.concatenate([
        x_prompt.reshape(N_PROMPT_ROWS, D_MODEL),
        jnp.transpose(x_sample, (1, 0, 2)).reshape(N_SAMPLE_ROWS, D_MODEL),
        meta_tokens.astype(F32),
        jnp.zeros((ROWS - META_ROW0 - N_META, D_MODEL), F32)], axis=0)

    fg = row(final_norm)
    x1 = _ffn(x_all, row(norm_ffn1[l]), bf(ffn1_w_gate[l]), bf(ffn1_w_up[l]), bf(ffn1_w_down[l]), fg, False)

    w_in_b = bf(w_in[l])
    u = _inproj(x1, row(norm_mix[l]), w_in_b)

    hist_t = jnp.transpose(state_pool[l], (1, 0, 2))
    a_out = _pool(u, hist_t, bf(pool_w[l]), row(pool_scale[l]))

    kb, b_t, c_t, pwl, pwc = _s5_tables(ssm_lambda_re[l], ssm_lambda_im[l], ssm_log_dt[l], ssm_b_re[l],
                                        ssm_b_im[l], ssm_c_re[l], ssm_c_im[l], ssm_d[l])
    sy, hp_re, hp_im, hs_re, hs_im = _s5(
        u, state_ssm_re[l].reshape(DEC_BATCH, -1), state_ssm_im[l].reshape(DEC_BATCH, -1),
        kb, b_t, c_t, pwl, pwc, row(ssm_d[l]))

    x2 = _out_proj(x1, a_out, sy, row(norm_mix[l]), w_in_b, row(b_gate[l]),
                   bf(glu_w[l]), row(glu_b[l]), bf(w_branch_a[l]), bf(w_branch_b[l]), bf(w_out[l]))
    y = _ffn(x2, row(norm_ffn2[l]), bf(ffn2_w_gate[l]), bf(ffn2_w_up[l]), bf(ffn2_w_down[l]), fg, True)

    y_prompt = y[:N_PROMPT_ROWS].reshape(BATCH, SEQ, D_MODEL)
    y_sample = jnp.transpose(y[SAMPLE_ROW0:META_ROW0].reshape(DEC_SEQ, DEC_BATCH, D_MODEL), (1, 0, 2))
    u_pool_p = u[:N_PROMPT_ROWS, :W_POOL].reshape(BATCH, SEQ, W_POOL)
    pool_p = u_pool_p[:, SEQ - POOL_HIST:][None]
    u_pool_s = jnp.transpose(u[SAMPLE_ROW0:META_ROW0, :W_POOL].reshape(DEC_SEQ, DEC_BATCH, W_POOL), (1, 0, 2))
    pool_s = jnp.concatenate([state_pool[l][:, DEC_SEQ:], u_pool_s], axis=1)[None]
    shp_p = (1, BATCH, N_GROUPS, SSM_STATE)
    shp_s = (1, DEC_BATCH, N_GROUPS, SSM_STATE)
    return (y_prompt, y_sample, pool_p, pool_s,
            hp_re.reshape(shp_p), hp_im.reshape(shp_p), hs_re.reshape(shp_s), hs_im.reshape(shp_s))
```

```python
import functools

import jax
import jax.numpy as jnp
from jax import lax
from jax.experimental import pallas as pl
from jax.experimental.pallas import tpu as pltpu

F32 = jnp.float32
BF16 = jnp.bfloat16

D_MODEL = 2048
BATCH = 4
SEQ = 2048
DEC_BATCH = 128
DEC_SEQ = 4
N_META = 16
D_FF = 5632
W_POOL = 1024
W_SSM = 1024
POOL_WINDOWS = (2, 4, 8, 16)
POOL_GW = 256
POOL_HIST = 15
SSM_GS = 16
N_GROUPS = 64
SSM_STATE = 64
RMS_EPS = 1e-6

N_PROMPT_ROWS = BATCH * SEQ
SAMPLE_ROW0 = N_PROMPT_ROWS
N_SAMPLE_ROWS = DEC_BATCH * DEC_SEQ
META_ROW0 = SAMPLE_ROW0 + N_SAMPLE_ROWS
TM = 512
ROWS = 18 * TM
N_TILES = ROWS // TM
TF = 512
TN_OUT = 512

LANES = 128
N_SLABS = W_SSM // LANES
SLAB_STATE = (LANES // SSM_GS) * SSM_STATE
CHUNK = 16
N_CHUNKS = (SEQ + N_META) // CHUNK
SEQ_STRIDE = 144
FOLD_ROW0 = 16
FOLD_ROWS = BATCH * SEQ_STRIDE

VMEM_LIMIT = 60 * 1024 * 1024


def _rms(x, g):
    r = lax.rsqrt(jnp.mean(x * x, axis=-1, keepdims=True) + RMS_EPS)
    return x * r * g


N_PROMPT_TILES = N_PROMPT_ROWS // TM
N_TAIL_ROWS = ROWS - N_PROMPT_ROWS


def _ffn_kernel(*refs, split_in, split_out):
    refs = list(refs)
    x_refs = [refs.pop(0) for _ in range(2 if split_in else 1)]
    g_ref, wg_ref, wu_ref, wd_ref, fg_ref = [refs.pop(0) for _ in range(5)]
    o_refs = [refs.pop(0) for _ in range(2 if split_out else 1)]
    xn_ref = refs.pop(0)
    acc_ref = refs.pop(0) if split_out else o_refs[0]
    i = pl.program_id(0)
    j = pl.program_id(1)
    in_prompt = i < N_PROMPT_TILES

    def init(x_ref):
        x = x_ref[...]
        xn_ref[...] = _rms(x, g_ref[...]).astype(BF16)
        acc_ref[...] = x

    if split_in:
        pl.when((j == 0) & in_prompt)(lambda: init(x_refs[0]))
        pl.when((j == 0) & jnp.logical_not(in_prompt))(lambda: init(x_refs[1]))
    else:
        pl.when(j == 0)(lambda: init(x_refs[0]))

    xn = xn_ref[...]
    gate = jnp.dot(xn, wg_ref[...], preferred_element_type=F32)
    up = jnp.dot(xn, wu_ref[...], preferred_element_type=F32)
    h = (gate * jax.nn.sigmoid(gate) * up * 0.5).astype(BF16)
    acc_ref[...] += jnp.dot(h, wd_ref[...], preferred_element_type=F32)

    if split_out:
        last = j == pl.num_programs(1) - 1

        def emit(o_ref):
            o_ref[...] = _rms(acc_ref[...], fg_ref[...])

        pl.when(last & in_prompt)(lambda: emit(o_refs[0]))
        pl.when(last & jnp.logical_not(in_prompt))(lambda: emit(o_refs[1]))


def _ffn(xs, g, wg, wu, wd, fg, split_out):
    split_in = len(xs) == 2
    row_spec = pl.BlockSpec((TM, D_MODEL), lambda i, j: (i, 0))
    prompt_spec = pl.BlockSpec((TM, D_MODEL), lambda i, j: (jnp.minimum(i, N_PROMPT_TILES - 1), 0))
    tail_spec = pl.BlockSpec((TM, D_MODEL), lambda i, j: (jnp.maximum(i - N_PROMPT_TILES, 0), 0))
    split_shapes = [jax.ShapeDtypeStruct((N_PROMPT_ROWS, D_MODEL), F32),
                    jax.ShapeDtypeStruct((N_TAIL_ROWS, D_MODEL), F32)]
    scratch = [pltpu.VMEM((TM, D_MODEL), BF16)]
    if split_out:
        scratch.append(pltpu.VMEM((TM, D_MODEL), F32))
    return pl.pallas_call(
        functools.partial(_ffn_kernel, split_in=split_in, split_out=split_out),
        grid=(N_TILES, D_FF // TF),
        in_specs=([prompt_spec, tail_spec] if split_in else [row_spec]) + [
            pl.BlockSpec((1, D_MODEL), lambda i, j: (0, 0)),
            pl.BlockSpec((D_MODEL, TF), lambda i, j: (0, j)),
            pl.BlockSpec((D_MODEL, TF), lambda i, j: (0, j)),
            pl.BlockSpec((TF, D_MODEL), lambda i, j: (j, 0)),
            pl.BlockSpec((1, D_MODEL), lambda i, j: (0, 0)),
        ],
        out_specs=[prompt_spec, tail_spec] if split_out else row_spec,
        out_shape=split_shapes if split_out else jax.ShapeDtypeStruct((ROWS, D_MODEL), F32),
        scratch_shapes=scratch,
        compiler_params=pltpu.CompilerParams(
            dimension_semantics=("arbitrary", "arbitrary"), vmem_limit_bytes=VMEM_LIMIT),
        name="ffn_final" if split_out else "ffn",
    )(*xs, g, wg, wu, wd, fg)


def _inproj_kernel(x_ref, g_ref, w_ref, o_ref, xn_ref):
    @pl.when(pl.program_id(1) == 0)
    def _():
        xn_ref[...] = _rms(x_ref[...], g_ref[...]).astype(BF16)

    o_ref[...] = jnp.dot(xn_ref[...], w_ref[...], preferred_element_type=F32)


def _inproj(x, g, w_in):
    n = W_POOL + W_SSM
    return pl.pallas_call(
        _inproj_kernel,
        grid=(N_TILES, n // 1024),
        in_specs=[
            pl.BlockSpec((TM, D_MODEL), lambda i, j: (i, 0)),
            pl.BlockSpec((1, D_MODEL), lambda i, j: (0, 0)),
            pl.BlockSpec((D_MODEL, 1024), lambda i, j: (0, j)),
        ],
        out_specs=pl.BlockSpec((TM, 1024), lambda i, j: (i, j)),
        out_shape=jax.ShapeDtypeStruct((ROWS, n), F32),
        scratch_shapes=[pltpu.VMEM((TM, D_MODEL), BF16)],
        compiler_params=pltpu.CompilerParams(
            dimension_semantics=("arbitrary", "arbitrary"), vmem_limit_bytes=VMEM_LIMIT),
        name="inproj",
    )(x, g, w_in)


TILES_PER_SEQ = SEQ // TM
HALO = 16


def _pool_kernel(u_ref, meta_ref, hist_ref, pw_ref, scale_ref, o_ref, full_ref, mean_ref):
    i = pl.program_id(0)

    @pl.when(i < N_PROMPT_TILES)
    def _():
        @pl.when(i % TILES_PER_SEQ == 0)
        def _():
            full_ref[0:HALO, :] = meta_ref[...]

        full_ref[HALO:HALO + TM, :] = u_ref[...]
        for g, w in enumerate(POOL_WINDOWS):
            cols = slice(g * POOL_GW, (g + 1) * POOL_GW)
            acc = full_ref[HALO:HALO + TM, cols]
            for k in range(1, w):
                acc = acc + full_ref[HALO - k:HALO - k + TM, cols]
            mean_ref[:, cols] = acc * (1.0 / w)
        full_ref[0:HALO, :] = full_ref[TM:TM + HALO, :]

    @pl.when(i == N_PROMPT_TILES)
    def _():
        for g, w in enumerate(POOL_WINDOWS):
            cols = slice(g * POOL_GW, (g + 1) * POOL_GW)
            for t in range(DEC_SEQ):
                acc = None
                for k in range(w):
                    p = POOL_HIST + t - k
                    if p >= POOL_HIST:
                        q = p - POOL_HIST
                        term = u_ref[q * DEC_BATCH:(q + 1) * DEC_BATCH, cols]
                    else:
                        term = hist_ref[p, :, cols]
                    acc = term if acc is None else acc + term
                mean_ref[t * DEC_BATCH:(t + 1) * DEC_BATCH, cols] = acc * (1.0 / w)

    @pl.when(i > N_PROMPT_TILES)
    def _():
        mean_ref[...] = u_ref[...]

    for g in range(len(POOL_WINDOWS)):
        cols = slice(g * POOL_GW, (g + 1) * POOL_GW)
        d = (mean_ref[:, cols] - u_ref[:, cols]).astype(BF16)
        y = jnp.dot(d, pw_ref[g], preferred_element_type=F32) * scale_ref[:, cols]
        o_ref[:, cols] = y.astype(BF16)


def _pool(u, hist_t, pool_w, pool_scale):
    return pl.pallas_call(
        _pool_kernel,
        grid=(N_TILES,),
        in_specs=[
            pl.BlockSpec((TM, W_POOL), lambda i: (i, 0)),
            pl.BlockSpec((N_META, W_POOL), lambda i: (META_ROW0 // N_META, 0)),
            pl.BlockSpec((POOL_HIST, DEC_BATCH, W_POOL), lambda i: (0, 0, 0)),
            pl.BlockSpec((len(POOL_WINDOWS), POOL_GW, POOL_GW), lambda i: (0, 0, 0)),
            pl.BlockSpec((1, W_POOL), lambda i: (0, 0)),
        ],
        out_specs=pl.BlockSpec((TM, W_POOL), lambda i: (i, 0)),
        out_shape=jax.ShapeDtypeStruct((ROWS, W_POOL), BF16),
        scratch_shapes=[pltpu.VMEM((HALO + TM, W_POOL), F32), pltpu.VMEM((TM, W_POOL), F32)],
        compiler_params=pltpu.CompilerParams(
            dimension_semantics=("arbitrary",), vmem_limit_bytes=VMEM_LIMIT),
        name="pool",
    )(u, u, hist_t, pool_w, pool_scale)


def _s5_tables(lam_re, lam_im, log_dt, b_re, b_im, c_re, c_im):
    dt = jnp.exp(log_dt)[:, None]
    k = jnp.arange(CHUNK + 1, dtype=F32)[:, None, None]
    mag = jnp.exp(k * (lam_re * dt)[None])
    ang = k * (lam_im * dt)[None]
    pw_re, pw_im = mag * jnp.cos(ang), mag * jnp.sin(ang)
    lb_re, lb_im = pw_re[1], pw_im[1]
    den = lam_re * lam_re + lam_im * lam_im
    q_re = ((lb_re - 1.0) * lam_re + lb_im * lam_im) / den
    q_im = (lb_im * lam_re - (lb_re - 1.0) * lam_im) / den
    bb_re = q_re[..., None] * b_re - q_im[..., None] * b_im
    bb_im = q_re[..., None] * b_im + q_im[..., None] * b_re
    eye = jnp.eye(LANES // SSM_GS, dtype=F32)
    gl = LANES // SSM_GS

    def bd_b(bb):
        return jnp.einsum("qgpc,gh->qgchp", bb.reshape(N_SLABS, gl, SSM_STATE, SSM_GS), eye).reshape(
            N_SLABS, LANES, SLAB_STATE)

    def bd_c(cc):
        return jnp.einsum("qgcp,gh->qgphc", cc.reshape(N_SLABS, gl, SSM_GS, SSM_STATE), eye).reshape(
            N_SLABS, SLAB_STATE, LANES)

    b_t = jnp.concatenate([bd_b(bb_re), bd_b(bb_im)], axis=2)
    c_t = jnp.concatenate([bd_c(c_re), bd_c(c_im)], axis=1)
    pr = pw_re.reshape(CHUNK + 1, N_SLABS, SLAB_STATE)
    pi = pw_im.reshape(CHUNK + 1, N_SLABS, SLAB_STATE)
    pwl = jnp.concatenate([jnp.transpose(pr, (1, 0, 2)), jnp.transpose(pi, (1, 0, 2))], axis=2)
    pwc = jnp.concatenate([jnp.transpose(pr, (1, 2, 0)), jnp.transpose(pi, (1, 2, 0))], axis=2)
    return b_t, c_t, pwl, pwc


def _s5_kernel(up_ref, um_ref, us_ref, h0re_ref, h0im_ref, b_ref, c_ref, pwl_ref, pwc_ref, d_ref,
               y_ref, hpre_ref, hpim_ref, hsre_ref, hsim_ref,
               l_ref, f_ref, kb_ref, m_ref, e_ref, vre_ref, vim_ref, yf_ref, hs_ref):
    ns = SLAB_STATE
    b_re = b_ref[0][:, :ns]
    b_im = b_ref[0][:, ns:]

    l_ref[...] = jnp.zeros_like(l_ref)
    for b in range(BATCH):
        r0 = b * SEQ_STRIDE + FOLD_ROW0
        for s in range(CHUNK):
            cols = slice(s * LANES, (s + 1) * LANES)
            l_ref[r0:r0 + SEQ // CHUNK, cols] = up_ref[pl.ds(b * SEQ + s, SEQ // CHUNK, stride=CHUNK), :]
            l_ref[r0 - 1:r0, cols] = um_ref[s:s + 1, :]
    lb = l_ref[...].astype(BF16)

    c_re = c_ref[0][:ns, :]
    c_im = c_ref[0][ns:, :]
    c_neg = jnp.concatenate([c_re, -c_im], axis=0)
    for s in range(CHUNK):
        p = pwl_ref[0, CHUNK - 1 - s:CHUNK - s, :]
        p_re, p_im = p[:, :ns], p[:, ns:]
        rows = slice(s * LANES, (s + 1) * LANES)
        fs = jnp.concatenate([b_re * p_re - b_im * p_im, b_re * p_im + b_im * p_re], axis=1)
        f_ref[rows, :] = fs.astype(BF16)
        kb = jnp.dot(fs, c_neg, preferred_element_type=F32, precision=lax.Precision.HIGHEST)
        if s == CHUNK - 1:
            ri = lax.broadcasted_iota(jnp.int32, (LANES, LANES), 0)
            ci = lax.broadcasted_iota(jnp.int32, (LANES, LANES), 1)
            kb = kb + jnp.where(ri == ci, jnp.broadcast_to(d_ref[...], (LANES, LANES)), 0.0)
        kb_ref[CHUNK - 1 - s] = kb.astype(BF16)
    v = jnp.dot(lb, f_ref[...], preferred_element_type=F32)
    nk = ns // LANES
    for k in range(nk):
        vre_ref[k] = v[:, k * LANES:(k + 1) * LANES]
        vim_ref[k] = v[:, ns + k * LANES:ns + (k + 1) * LANES]

    a = pwl_ref[0, CHUNK:CHUNK + 1, :]
    a_re = [jnp.broadcast_to(a[:, k * LANES:(k + 1) * LANES], (BATCH, LANES)) for k in range(nk)]
    a_im = [jnp.broadcast_to(a[:, ns + k * LANES:ns + (k + 1) * LANES], (BATCH, LANES)) for k in range(nk)]
    h_re = [jnp.zeros((BATCH, LANES), F32) for _ in range(nk)]
    h_im = [jnp.zeros((BATCH, LANES), F32) for _ in range(nk)]
    for j in range(N_CHUNKS):
        rows = pl.ds(FOLD_ROW0 - 1 + j, BATCH, stride=SEQ_STRIDE)
        for k in range(nk):
            v_re = vre_ref[k, rows, :]
            v_im = vim_ref[k, rows, :]
            vre_ref[k, rows, :] = h_re[k]
            vim_ref[k, rows, :] = h_im[k]
            h_re[k], h_im[k] = (a_re[k] * h_re[k] - a_im[k] * h_im[k] + v_re,
                                a_re[k] * h_im[k] + a_im[k] * h_re[k] + v_im)
    for k in range(nk):
        hpre_ref[:, k * LANES:(k + 1) * LANES] = h_re[k]
        hpim_ref[:, k * LANES:(k + 1) * LANES] = h_im[k]

    half = CHUNK // 2
    for s in range(CHUNK):
        for t in range(half, CHUNK):
            blk = kb_ref[t - s] if t >= s else jnp.zeros((LANES, LANES), BF16)
            m_ref[s * LANES:(s + 1) * LANES, (t - half) * LANES:(t - half + 1) * LANES] = blk
    y_left = jnp.dot(lb[:, :half * LANES], m_ref[half * LANES:, :], preferred_element_type=F32)
    y_right = jnp.dot(lb, m_ref[...], preferred_element_type=F32)

    for t in range(CHUNK):
        p_re = pwc_ref[0][:, t + 1:t + 2]
        p_im = pwc_ref[0][:, CHUNK + 1 + t + 1:CHUNK + 1 + t + 2]
        cols = slice(t * LANES, (t + 1) * LANES)
        e_ref[:ns, cols] = (c_re * p_re - c_im * p_im).astype(BF16)
        e_ref[ns:, cols] = (-(c_re * p_im + c_im * p_re)).astype(BF16)
    h_in = jnp.concatenate([vre_ref[k] for k in range(nk)] + [vim_ref[k] for k in range(nk)],
                           axis=1).astype(BF16)
    ys = jnp.dot(h_in, e_ref[...], preferred_element_type=F32)
    yf_ref[:, :half * LANES] = y_left + ys[:, :half * LANES]
    yf_ref[:, half * LANES:] = y_right + ys[:, half * LANES:]
    for b in range(BATCH):
        r0 = b * SEQ_STRIDE + FOLD_ROW0
        for t in range(CHUNK):
            y_ref[pl.ds(b * SEQ + t, SEQ // CHUNK, stride=CHUNK), :] = (
                yf_ref[r0:r0 + SEQ // CHUNK, t * LANES:(t + 1) * LANES])

    u = us_ref[...]
    bu = jnp.dot(u.astype(BF16), b_ref[0].astype(BF16), preferred_element_type=F32)
    l1 = pwl_ref[0, 1:2, :]
    l_re, l_im = l1[:, :ns], l1[:, ns:]
    g_re = h0re_ref[...]
    g_im = h0im_ref[...]
    for t in range(DEC_SEQ):
        rows = slice(t * DEC_BATCH, (t + 1) * DEC_BATCH)
        g_re, g_im = (l_re * g_re - l_im * g_im + bu[rows, :ns],
                      l_re * g_im + l_im * g_re + bu[rows, ns:])
        hs_ref[rows, :ns] = g_re.astype(BF16)
        hs_ref[rows, ns:] = (-g_im).astype(BF16)
    hsre_ref[...] = g_re
    hsim_ref[...] = g_im
    y_ref[SAMPLE_ROW0:META_ROW0, :] = (
        jnp.dot(hs_ref[...], c_ref[0].astype(BF16), preferred_element_type=F32) + d_ref[...] * u)
    y_ref[META_ROW0:, :] = jnp.zeros((ROWS - META_ROW0, LANES), F32)


def _s5(u, h0_re, h0_im, b_t, c_t, pwl, pwc, d_skip):
    c0 = W_POOL // LANES
    ns = SLAB_STATE
    return pl.pallas_call(
        _s5_kernel,
        grid=(N_SLABS,),
        in_specs=[
            pl.BlockSpec((N_PROMPT_ROWS, LANES), lambda i: (0, c0 + i)),
            pl.BlockSpec((N_META, LANES), lambda i: (META_ROW0 // N_META, c0 + i)),
            pl.BlockSpec((N_SAMPLE_ROWS, LANES), lambda i: (SAMPLE_ROW0 // N_SAMPLE_ROWS, c0 + i)),
            pl.BlockSpec((DEC_BATCH, ns), lambda i: (0, i)),
            pl.BlockSpec((DEC_BATCH, ns), lambda i: (0, i)),
            pl.BlockSpec((1, LANES, 2 * ns), lambda i: (i, 0, 0)),
            pl.BlockSpec((1, 2 * ns, LANES), lambda i: (i, 0, 0)),
            pl.BlockSpec((1, CHUNK + 1, 2 * ns), lambda i: (i, 0, 0)),
            pl.BlockSpec((1, ns, 2 * (CHUNK + 1)), lambda i: (i, 0, 0)),
            pl.BlockSpec((1, LANES), lambda i: (0, i)),
        ],
        out_specs=[
            pl.BlockSpec((ROWS, LANES), lambda i: (0, i)),
            pl.BlockSpec((BATCH, ns), lambda i: (0, i)),
            pl.BlockSpec((BATCH, ns), lambda i: (0, i)),
            pl.BlockSpec((DEC_BATCH, ns), lambda i: (0, i)),
            pl.BlockSpec((DEC_BATCH, ns), lambda i: (0, i)),
        ],
        out_shape=[
            jax.ShapeDtypeStruct((ROWS, W_SSM), F32),
            jax.ShapeDtypeStruct((BATCH, N_GROUPS * SSM_STATE), F32),
            jax.ShapeDtypeStruct((BATCH, N_GROUPS * SSM_STATE), F32),
            jax.ShapeDtypeStruct((DEC_BATCH, N_GROUPS * SSM_STATE), F32),
            jax.ShapeDtypeStruct((DEC_BATCH, N_GROUPS * SSM_STATE), F32),
        ],
        scratch_shapes=[
            pltpu.VMEM((FOLD_ROWS, CHUNK * LANES), F32),
            pltpu.VMEM((CHUNK * LANES, 2 * ns), BF16),
            pltpu.VMEM((CHUNK, LANES, LANES), BF16),
            pltpu.VMEM((CHUNK * LANES, CHUNK * LANES // 2), BF16),
            pltpu.VMEM((2 * ns, CHUNK * LANES), BF16),
            pltpu.VMEM((ns // LANES, FOLD_ROWS, LANES), F32),
            pltpu.VMEM((ns // LANES, FOLD_ROWS, LANES), F32),
            pltpu.VMEM((FOLD_ROWS, CHUNK * LANES), F32),
            pltpu.VMEM((N_SAMPLE_ROWS, 2 * ns), BF16),
        ],
        compiler_params=pltpu.CompilerParams(
            dimension_semantics=("arbitrary",), vmem_limit_bytes=VMEM_LIMIT),
        name="s5",
    )(u, u, u, h0_re, h0_im, b_t, c_t, pwl, pwc, d_skip)


def _out_kernel(x_ref, a_ref, sy_ref, g_ref, wga_ref, wgb_ref, bga_ref, bgb_ref, gluw_ref, glub_ref,
                wa_ref, wb_ref, wo_ref, o_ref, xn_ref, s_ref):
    j = pl.program_id(1)

    @pl.when(j == 0)
    def _():
        x = x_ref[...]
        xn_ref[...] = _rms(x, g_ref[...]).astype(BF16)
        o_ref[...] = x
        s = jax.nn.gelu(sy_ref[...])
        z = jnp.dot(s.astype(BF16), gluw_ref[...], preferred_element_type=F32) + glub_ref[...]
        s_ref[...] = (s * jax.nn.sigmoid(z)).astype(BF16)

    xn = xn_ref[...]
    ga = jax.nn.sigmoid(jnp.dot(xn, wga_ref[...], preferred_element_type=F32) + bga_ref[...])
    gb = jax.nn.sigmoid(jnp.dot(xn, wgb_ref[...], preferred_element_type=F32) + bgb_ref[...])
    merged = (ga * jnp.dot(a_ref[...], wa_ref[...], preferred_element_type=F32)
              + gb * jnp.dot(s_ref[...], wb_ref[...], preferred_element_type=F32))
    o_ref[...] += jnp.dot(merged.astype(BF16), wo_ref[...], preferred_element_type=F32)


def _out_proj(x, a, sy, g, w_in, b_gate, glu_w, glu_b, w_a, w_b, w_out):
    nb = D_MODEL // TN_OUT
    g0 = (W_POOL + W_SSM) // TN_OUT
    return pl.pallas_call(
        _out_kernel,
        grid=(N_TILES, nb),
        in_specs=[
            pl.BlockSpec((TM, D_MODEL), lambda i, j: (i, 0)),
            pl.BlockSpec((TM, W_POOL), lambda i, j: (i, 0)),
            pl.BlockSpec((TM, W_SSM), lambda i, j: (i, 0)),
            pl.BlockSpec((1, D_MODEL), lambda i, j: (0, 0)),
            pl.BlockSpec((D_MODEL, TN_OUT), lambda i, j: (0, g0 + j)),
            pl.BlockSpec((D_MODEL, TN_OUT), lambda i, j: (0, g0 + nb + j)),
            pl.BlockSpec((1, TN_OUT), lambda i, j: (0, j)),
            pl.BlockSpec((1, TN_OUT), lambda i, j: (0, nb + j)),
            pl.BlockSpec((W_SSM, W_SSM), lambda i, j: (0, 0)),
            pl.BlockSpec((1, W_SSM), lambda i, j: (0, 0)),
            pl.BlockSpec((W_POOL, TN_OUT), lambda i, j: (0, j)),
            pl.BlockSpec((W_SSM, TN_OUT), lambda i, j: (0, j)),
            pl.BlockSpec((TN_OUT, D_MODEL), lambda i, j: (j, 0)),
        ],
        out_specs=pl.BlockSpec((TM, D_MODEL), lambda i, j: (i, 0)),
        out_shape=jax.ShapeDtypeStruct((ROWS, D_MODEL), F32),
        scratch_shapes=[pltpu.VMEM((TM, D_MODEL), BF16), pltpu.VMEM((TM, W_SSM), BF16)],
        compiler_params=pltpu.CompilerParams(
            dimension_semantics=("arbitrary", "arbitrary"), vmem_limit_bytes=VMEM_LIMIT),
        name="out_proj",
    )(x, a, sy, g, w_in, w_in, b_gate, b_gate, glu_w, glu_b, w_a, w_b, w_out)


def kernel(x_prompt, x_sample, state_pool, state_ssm_re, state_ssm_im, meta_tokens, norm_ffn1, ffn1_w_gate, ffn1_w_up, ffn1_w_down, norm_mix, w_in, b_gate, pool_w, pool_scale, ssm_lambda_re, ssm_lambda_im, ssm_log_dt, ssm_b_re, ssm_b_im, ssm_c_re, ssm_c_im, ssm_d, glu_w, glu_b, w_branch_a, w_branch_b, w_out, norm_ffn2, ffn2_w_gate, ffn2_w_up, ffn2_w_down, final_norm):
    l = 0
    bf = lambda w: w.astype(BF16)
    row = lambda v: v.reshape(1, -1).astype(F32)

    x_tail = jnp.concatenate([
        jnp.transpose(x_sample, (1, 0, 2)).reshape(N_SAMPLE_ROWS, D_MODEL),
        meta_tokens.astype(F32),
        jnp.zeros((ROWS - META_ROW0 - N_META, D_MODEL), F32)], axis=0)

    fg = row(final_norm)
    x1 = _ffn([x_prompt.reshape(N_PROMPT_ROWS, D_MODEL), x_tail], row(norm_ffn1[l]),
              bf(ffn1_w_gate[l]), bf(ffn1_w_up[l]), bf(ffn1_w_down[l]), fg, False)

    w_in_b = bf(w_in[l])
    u = _inproj(x1, row(norm_mix[l]), w_in_b)

    hist_t = jnp.transpose(state_pool[l], (1, 0, 2))
    a_out = _pool(u, hist_t, bf(pool_w[l]), row(pool_scale[l]))

    b_t, c_t, pwl, pwc = _s5_tables(ssm_lambda_re[l], ssm_lambda_im[l], ssm_log_dt[l], ssm_b_re[l],
                                    ssm_b_im[l], ssm_c_re[l], ssm_c_im[l])
    sy, hp_re, hp_im, hs_re, hs_im = _s5(
        u, state_ssm_re[l].reshape(DEC_BATCH, -1), state_ssm_im[l].reshape(DEC_BATCH, -1),
        b_t, c_t, pwl, pwc, row(ssm_d[l]))

    x2 = _out_proj(x1, a_out, sy, row(norm_mix[l]), w_in_b, row(b_gate[l]),
                   bf(glu_w[l]), row(glu_b[l]), bf(w_branch_a[l]), bf(w_branch_b[l]), bf(w_out[l]))
    y_p, y_t = _ffn([x2], row(norm_ffn2[l]), bf(ffn2_w_gate[l]), bf(ffn2_w_up[l]), bf(ffn2_w_down[l]), fg, True)

    y_prompt = y_p.reshape(BATCH, SEQ, D_MODEL)
    y_sample = jnp.transpose(y_t[:N_SAMPLE_ROWS].reshape(DEC_SEQ, DEC_BATCH, D_MODEL), (1, 0, 2))
    pool_p = jnp.stack([u[(b + 1) * SEQ - POOL_HIST:(b + 1) * SEQ, :W_POOL] for b in range(BATCH)])[None]
    u_pool_s = jnp.transpose(u[SAMPLE_ROW0:META_ROW0, :W_POOL].reshape(DEC_SEQ, DEC_BATCH, W_POOL), (1, 0, 2))
    pool_s = jnp.concatenate([state_pool[l][:, DEC_SEQ:], u_pool_s], axis=1)[None]
    shp_p = (1, BATCH, N_GROUPS, SSM_STATE)
    shp_s = (1, DEC_BATCH, N_GROUPS, SSM_STATE)
    return (y_prompt, y_sample, pool_p, pool_s,
            hp_re.reshape(shp_p), hp_im.reshape(shp_p), hs_re.reshape(shp_s), hs_im.reshape(shp_s))
```

```python
import functools

import jax
import jax.numpy as jnp
from jax import lax
from jax.experimental import pallas as pl
from jax.experimental.pallas import tpu as pltpu

F32 = jnp.float32
BF16 = jnp.bfloat16

D_MODEL = 2048
BATCH = 4
SEQ = 2048
DEC_BATCH = 128
DEC_SEQ = 4
N_META = 16
D_FF = 5632
W_POOL = 1024
W_SSM = 1024
POOL_WINDOWS = (2, 4, 8, 16)
POOL_GW = 256
POOL_HIST = 15
SSM_GS = 16
N_GROUPS = 64
SSM_STATE = 64
RMS_EPS = 1e-6

N_PROMPT_ROWS = BATCH * SEQ
N_SAMPLE_ROWS = DEC_BATCH * DEC_SEQ
N_TAIL_ROWS = N_SAMPLE_ROWS + N_META
TM_FFN = 1024
TM = 512
TF = 256
TN_OUT = 512
DOWN_CHUNK = 512

LANES = 128
N_SLABS = W_SSM // LANES
SLAB_STATE = (LANES // SSM_GS) * SSM_STATE
CHUNK = 16
N_CHUNKS = (SEQ + N_META) // CHUNK
SEQ_STRIDE = 144
FOLD_ROW0 = 16
FOLD_ROWS = BATCH * SEQ_STRIDE

VMEM_LIMIT = 60 * 1024 * 1024


def _rms(x, g):
    r = lax.rsqrt(jnp.mean(x * x, axis=-1, keepdims=True) + RMS_EPS)
    return x * r * g


def _params(n_axes):
    return pltpu.CompilerParams(dimension_semantics=("arbitrary",) * n_axes, vmem_limit_bytes=VMEM_LIMIT)


def _ffn_kernel(x_ref, g_ref, wg_ref, wu_ref, wd_ref, fg_ref, o_ref, xn_ref, *, final_norm):
    j = pl.program_id(1)

    @pl.when(j == 0)
    def _():
        x = x_ref[...]
        xn_ref[...] = _rms(x, g_ref[...]).astype(BF16)
        o_ref[...] = x

    xn = xn_ref[...]
    gate = jnp.dot(xn, wg_ref[...].astype(BF16), preferred_element_type=F32)
    up = jnp.dot(xn, wu_ref[...].astype(BF16), preferred_element_type=F32)
    h = (gate * jax.nn.sigmoid(gate) * up * 0.5).astype(BF16)
    for n in range(D_MODEL // DOWN_CHUNK):
        cols = slice(n * DOWN_CHUNK, (n + 1) * DOWN_CHUNK)
        o_ref[:, cols] += jnp.dot(h, wd_ref[:, cols].astype(BF16), preferred_element_type=F32)

    if final_norm:
        @pl.when(j == pl.num_programs(1) - 1)
        def _():
            o_ref[...] = _rms(o_ref[...], fg_ref[...])


def _ffn(x, g, wg, wu, wd, fg, tm, final_norm):
    rows = x.shape[0]
    return pl.pallas_call(
        functools.partial(_ffn_kernel, final_norm=final_norm),
        grid=(rows // tm, D_FF // TF),
        in_specs=[
            pl.BlockSpec((tm, D_MODEL), lambda i, j: (i, 0)),
            pl.BlockSpec((1, D_MODEL), lambda i, j: (0, 0)),
            pl.BlockSpec((D_MODEL, TF), lambda i, j: (0, j)),
            pl.BlockSpec((D_MODEL, TF), lambda i, j: (0, j)),
            pl.BlockSpec((TF, D_MODEL), lambda i, j: (j, 0)),
            pl.BlockSpec((1, D_MODEL), lambda i, j: (0, 0)),
        ],
        out_specs=pl.BlockSpec((tm, D_MODEL), lambda i, j: (i, 0)),
        out_shape=jax.ShapeDtypeStruct((rows, D_MODEL), F32),
        scratch_shapes=[pltpu.VMEM((tm, D_MODEL), BF16)],
        compiler_params=_params(2),
        name="ffn_final" if final_norm else "ffn",
    )(x, g, wg, wu, wd, fg)


def _inproj_kernel(x_ref, g_ref, w_ref, o_ref, xn_ref):
    @pl.when(pl.program_id(1) == 0)
    def _():
        xn_ref[...] = _rms(x_ref[...], g_ref[...]).astype(BF16)

    o_ref[...] = jnp.dot(xn_ref[...], w_ref[...], preferred_element_type=F32)


def _inproj(x, g, w_in, tm):
    rows = x.shape[0]
    n = W_POOL + W_SSM
    return pl.pallas_call(
        _inproj_kernel,
        grid=(rows // tm, n // 1024),
        in_specs=[
            pl.BlockSpec((tm, D_MODEL), lambda i, j: (i, 0)),
            pl.BlockSpec((1, D_MODEL), lambda i, j: (0, 0)),
            pl.BlockSpec((D_MODEL, 1024), lambda i, j: (0, j)),
        ],
        out_specs=pl.BlockSpec((tm, 1024), lambda i, j: (i, j)),
        out_shape=jax.ShapeDtypeStruct((rows, n), F32),
        scratch_shapes=[pltpu.VMEM((tm, D_MODEL), BF16)],
        compiler_params=_params(2),
        name="inproj",
    )(x, g, w_in)


TILES_PER_SEQ = SEQ // TM
HALO = 16


def _pool_project(mean_ref, u_ref, pw_ref, scale_ref, o_ref):
    for g in range(len(POOL_WINDOWS)):
        cols = slice(g * POOL_GW, (g + 1) * POOL_GW)
        d = (mean_ref[:, cols] - u_ref[:, cols]).astype(BF16)
        y = jnp.dot(d, pw_ref[g], preferred_element_type=F32) * scale_ref[:, cols]
        o_ref[:, cols] = y.astype(BF16)


def _pool_prompt_kernel(u_ref, meta_ref, pw_ref, scale_ref, o_ref, full_ref, mean_ref):
    @pl.when(pl.program_id(0) % TILES_PER_SEQ == 0)
    def _():
        full_ref[0:HALO, :] = meta_ref[...]

    full_ref[HALO:HALO + TM, :] = u_ref[...]
    for g, w in enumerate(POOL_WINDOWS):
        cols = slice(g * POOL_GW, (g + 1) * POOL_GW)
        acc = full_ref[HALO:HALO + TM, cols]
        for k in range(1, w):
            acc = acc + full_ref[HALO - k:HALO - k + TM, cols]
        mean_ref[:, cols] = acc * (1.0 / w)
    full_ref[0:HALO, :] = full_ref[TM:TM + HALO, :]
    _pool_project(mean_ref, u_ref, pw_ref, scale_ref, o_ref)


def _pool_tail_kernel(u_ref, hist_ref, pw_ref, scale_ref, o_ref, mean_ref):
    for g, w in enumerate(POOL_WINDOWS):
        cols = slice(g * POOL_GW, (g + 1) * POOL_GW)
        for t in range(DEC_SEQ):
            acc = None
            for k in range(w):
                p = POOL_HIST + t - k
                if p >= POOL_HIST:
                    q = p - POOL_HIST
                    term = u_ref[q * DEC_BATCH:(q + 1) * DEC_BATCH, cols]
                else:
                    term = hist_ref[p, :, cols]
                acc = term if acc is None else acc + term
            mean_ref[t * DEC_BATCH:(t + 1) * DEC_BATCH, cols] = acc * (1.0 / w)
    mean_ref[N_SAMPLE_ROWS:, :] = u_ref[N_SAMPLE_ROWS:, :]
    _pool_project(mean_ref, u_ref, pw_ref, scale_ref, o_ref)


def _pool(u_p, u_t, hist_t, pool_w, pool_scale):
    pw_spec = pl.BlockSpec((len(POOL_WINDOWS), POOL_GW, POOL_GW), lambda i: (0, 0, 0))
    sc_spec = pl.BlockSpec((1, W_POOL), lambda i: (0, 0))
    a_p = pl.pallas_call(
        _pool_prompt_kernel,
        grid=(N_PROMPT_ROWS // TM,),
        in_specs=[
            pl.BlockSpec((TM, W_POOL), lambda i: (i, 0)),
            pl.BlockSpec((N_META, W_POOL), lambda i: (N_SAMPLE_ROWS // N_META, 0)),
            pw_spec, sc_spec,
        ],
        out_specs=pl.BlockSpec((TM, W_POOL), lambda i: (i, 0)),
        out_shape=jax.ShapeDtypeStruct((N_PROMPT_ROWS, W_POOL), BF16),
        scratch_shapes=[pltpu.VMEM((HALO + TM, W_POOL), F32), pltpu.VMEM((TM, W_POOL), F32)],
        compiler_params=_params(1),
        name="pool_prompt",
    )(u_p, u_t, pool_w, pool_scale)
    a_t = pl.pallas_call(
        _pool_tail_kernel,
        grid=(1,),
        in_specs=[
            pl.BlockSpec((N_TAIL_ROWS, W_POOL), lambda i: (0, 0)),
            pl.BlockSpec((POOL_HIST, DEC_BATCH, W_POOL), lambda i: (0, 0, 0)),
            pw_spec, sc_spec,
        ],
        out_specs=pl.BlockSpec((N_TAIL_ROWS, W_POOL), lambda i: (0, 0)),
        out_shape=jax.ShapeDtypeStruct((N_TAIL_ROWS, W_POOL), BF16),
        scratch_shapes=[pltpu.VMEM((N_TAIL_ROWS, W_POOL), F32)],
        compiler_params=_params(1),
        name="pool_tail",
    )(u_t, hist_t, pool_w, pool_scale)
    return a_p, a_t


def _s5_tables(lam_re, lam_im, log_dt, b_re, b_im, c_re, c_im):
    dt = jnp.exp(log_dt)[:, None]
    k = jnp.arange(CHUNK + 1, dtype=F32)[:, None, None]
    mag = jnp.exp(k * (lam_re * dt)[None])
    ang = k * (lam_im * dt)[None]
    pw_re, pw_im = mag * jnp.cos(ang), mag * jnp.sin(ang)
    lb_re, lb_im = pw_re[1], pw_im[1]
    den = lam_re * lam_re + lam_im * lam_im
    q_re = ((lb_re - 1.0) * lam_re + lb_im * lam_im) / den
    q_im = (lb_im * lam_re - (lb_re - 1.0) * lam_im) / den
    bb_re = q_re[..., None] * b_re - q_im[..., None] * b_im
    bb_im = q_re[..., None] * b_im + q_im[..., None] * b_re
    eye = jnp.eye(LANES // SSM_GS, dtype=F32)
    gl = LANES // SSM_GS

    def bd_b(bb):
        return jnp.einsum("qgpc,gh->qgchp", bb.reshape(N_SLABS, gl, SSM_STATE, SSM_GS), eye).reshape(
            N_SLABS, LANES, SLAB_STATE)

    def bd_c(cc):
        return jnp.einsum("qgcp,gh->qgphc", cc.reshape(N_SLABS, gl, SSM_GS, SSM_STATE), eye).reshape(
            N_SLABS, SLAB_STATE, LANES)

    b_t = jnp.concatenate([bd_b(bb_re), bd_b(bb_im)], axis=2)
    c_t = jnp.concatenate([bd_c(c_re), bd_c(c_im)], axis=1)
    pr = pw_re.reshape(CHUNK + 1, N_SLABS, SLAB_STATE)
    pi = pw_im.reshape(CHUNK + 1, N_SLABS, SLAB_STATE)
    pwl = jnp.concatenate([jnp.transpose(pr, (1, 0, 2)), jnp.transpose(pi, (1, 0, 2))], axis=2)
    pwc = jnp.concatenate([jnp.transpose(pr, (1, 2, 0)), jnp.transpose(pi, (1, 2, 0))], axis=2)
    return b_t, c_t, pwl, pwc


def _s5_kernel(up_ref, um_ref, us_ref, h0re_ref, h0im_ref, b_ref, c_ref, pwl_ref, pwc_ref, d_ref,
               yp_ref, yt_ref, hpre_ref, hpim_ref, hsre_ref, hsim_ref,
               l_ref, f_ref, m_ref, e_ref, vre_ref, vim_ref, yf_ref, hs_ref):
    ns = SLAB_STATE
    b_re = b_ref[0][:, :ns]
    b_im = b_ref[0][:, ns:]
    c_re = c_ref[0][:ns, :]
    c_im = c_ref[0][ns:, :]

    l_ref[...] = jnp.zeros_like(l_ref)
    for b in range(BATCH):
        r0 = b * SEQ_STRIDE + FOLD_ROW0
        for s in range(CHUNK):
            cols = slice(s * LANES, (s + 1) * LANES)
            l_ref[r0:r0 + SEQ // CHUNK, cols] = up_ref[pl.ds(b * SEQ + s, SEQ // CHUNK, stride=CHUNK), :]
            l_ref[r0 - 1:r0, cols] = um_ref[s:s + 1, :]
    lb = l_ref[...].astype(BF16)

    for s in range(CHUNK):
        p = pwl_ref[0, CHUNK - 1 - s:CHUNK - s, :]
        p_re, p_im = p[:, :ns], p[:, ns:]
        rows = slice(s * LANES, (s + 1) * LANES)
        f_ref[rows, :ns] = (b_re * p_re - b_im * p_im).astype(BF16)
        f_ref[rows, ns:] = (b_re * p_im + b_im * p_re).astype(BF16)
    v = jnp.dot(lb, f_ref[...], preferred_element_type=F32)
    nk = ns // LANES
    for k in range(nk):
        vre_ref[k] = v[:, k * LANES:(k + 1) * LANES]
        vim_ref[k] = v[:, ns + k * LANES:ns + (k + 1) * LANES]

    a = pwl_ref[0, CHUNK:CHUNK + 1, :]
    a_re = [jnp.broadcast_to(a[:, k * LANES:(k + 1) * LANES], (BATCH, LANES)) for k in range(nk)]
    a_im = [jnp.broadcast_to(a[:, ns + k * LANES:ns + (k + 1) * LANES], (BATCH, LANES)) for k in range(nk)]
    h_re = [jnp.zeros((BATCH, LANES), F32) for _ in range(nk)]
    h_im = [jnp.zeros((BATCH, LANES), F32) for _ in range(nk)]
    for j in range(N_CHUNKS):
        rows = pl.ds(FOLD_ROW0 - 1 + j, BATCH, stride=SEQ_STRIDE)
        for k in range(nk):
            v_re = vre_ref[k, rows, :]
            v_im = vim_ref[k, rows, :]
            vre_ref[k, rows, :] = h_re[k]
            vim_ref[k, rows, :] = h_im[k]
            h_re[k], h_im[k] = (a_re[k] * h_re[k] - a_im[k] * h_im[k] + v_re,
                                a_re[k] * h_im[k] + a_im[k] * h_re[k] + v_im)
    for k in range(nk):
        hpre_ref[:, k * LANES:(k + 1) * LANES] = h_re[k]
        hpim_ref[:, k * LANES:(k + 1) * LANES] = h_im[k]

    c_neg = jnp.concatenate([c_re, -c_im], axis=0).astype(BF16)
    kb_all = jnp.dot(f_ref[...], c_neg, preferred_element_type=F32)
    ri = lax.broadcasted_iota(jnp.int32, (LANES, LANES), 0)
    ci = lax.broadcasted_iota(jnp.int32, (LANES, LANES), 1)
    d_diag = jnp.where(ri == ci, jnp.broadcast_to(d_ref[...], (LANES, LANES)), 0.0)
    kb = [None] * CHUNK
    for s in range(CHUNK):
        blk = kb_all[s * LANES:(s + 1) * LANES, :]
        if s == CHUNK - 1:
            blk = blk + d_diag
        kb[CHUNK - 1 - s] = blk.astype(BF16)

    half = CHUNK // 2
    for s in range(CHUNK):
        for t in range(half, CHUNK):
            blk = kb[t - s] if t >= s else jnp.zeros((LANES, LANES), BF16)
            m_ref[s * LANES:(s + 1) * LANES, (t - half) * LANES:(t - half + 1) * LANES] = blk
    y_left = jnp.dot(lb[:, :half * LANES], m_ref[half * LANES:, :], preferred_element_type=F32)
    y_right = jnp.dot(lb, m_ref[...], preferred_element_type=F32)

    for t in range(CHUNK):
        p_re = pwc_ref[0][:, t + 1:t + 2]
        p_im = pwc_ref[0][:, CHUNK + 1 + t + 1:CHUNK + 1 + t + 2]
        cols = slice(t * LANES, (t + 1) * LANES)
        e_ref[:ns, cols] = (c_re * p_re - c_im * p_im).astype(BF16)
        e_ref[ns:, cols] = (-(c_re * p_im + c_im * p_re)).astype(BF16)
    h_in = jnp.concatenate([vre_ref[k] for k in range(nk)] + [vim_ref[k] for k in range(nk)],
                           axis=1).astype(BF16)
    ys = jnp.dot(h_in, e_ref[...], preferred_element_type=F32)
    yf_ref[:, :half * LANES] = y_left + ys[:, :half * LANES]
    yf_ref[:, half * LANES:] = y_right + ys[:, half * LANES:]
    for b in range(BATCH):
        r0 = b * SEQ_STRIDE + FOLD_ROW0
        for t in range(CHUNK):
            yp_ref[pl.ds(b * SEQ + t, SEQ // CHUNK, stride=CHUNK), :] = (
                yf_ref[r0:r0 + SEQ // CHUNK, t * LANES:(t + 1) * LANES])

    u = us_ref[...]
    bu = jnp.dot(u.astype(BF16), b_ref[0].astype(BF16), preferred_element_type=F32)
    l1 = pwl_ref[0, 1:2, :]
    l_re, l_im = l1[:, :ns], l1[:, ns:]
    g_re = h0re_ref[...]
    g_im = h0im_ref[...]
    for t in range(DEC_SEQ):
        rows = slice(t * DEC_BATCH, (t + 1) * DEC_BATCH)
        g_re, g_im = (l_re * g_re - l_im * g_im + bu[rows, :ns],
                      l_re * g_im + l_im * g_re + bu[rows, ns:])
        hs_ref[rows, :ns] = g_re.astype(BF16)
        hs_ref[rows, ns:] = g_im.astype(BF16)
    hsre_ref[...] = g_re
    hsim_ref[...] = g_im
    yt_ref[0:N_SAMPLE_ROWS, :] = jnp.dot(hs_ref[...], c_neg, preferred_element_type=F32) + d_ref[...] * u
    yt_ref[N_SAMPLE_ROWS:, :] = jnp.zeros((N_META, LANES), F32)


def _s5(u_p, u_t, h0_re, h0_im, b_t, c_t, pwl, pwc, d_skip):
    c0 = W_POOL // LANES
    ns = SLAB_STATE
    state = lambda rows: pl.BlockSpec((rows, ns), lambda i: (0, i))
    return pl.pallas_call(
        _s5_kernel,
        grid=(N_SLABS,),
        in_specs=[
            pl.BlockSpec((N_PROMPT_ROWS, LANES), lambda i: (0, c0 + i)),
            pl.BlockSpec((N_META, LANES), lambda i: (N_SAMPLE_ROWS // N_META, c0 + i)),
            pl.BlockSpec((N_SAMPLE_ROWS, LANES), lambda i: (0, c0 + i)),
            state(DEC_BATCH), state(DEC_BATCH),
            pl.BlockSpec((1, LANES, 2 * ns), lambda i: (i, 0, 0)),
            pl.BlockSpec((1, 2 * ns, LANES), lambda i: (i, 0, 0)),
            pl.BlockSpec((1, CHUNK + 1, 2 * ns), lambda i: (i, 0, 0)),
            pl.BlockSpec((1, ns, 2 * (CHUNK + 1)), lambda i: (i, 0, 0)),
            pl.BlockSpec((1, LANES), lambda i: (0, i)),
        ],
        out_specs=[
            pl.BlockSpec((N_PROMPT_ROWS, LANES), lambda i: (0, i)),
            pl.BlockSpec((N_TAIL_ROWS, LANES), lambda i: (0, i)),
            state(BATCH), state(BATCH), state(DEC_BATCH), state(DEC_BATCH),
        ],
        out_shape=[
            jax.ShapeDtypeStruct((N_PROMPT_ROWS, W_SSM), F32),
            jax.ShapeDtypeStruct((N_TAIL_ROWS, W_SSM), F32),
            jax.ShapeDtypeStruct((BATCH, N_GROUPS * SSM_STATE), F32),
            jax.ShapeDtypeStruct((BATCH, N_GROUPS * SSM_STATE), F32),
            jax.ShapeDtypeStruct((DEC_BATCH, N_GROUPS * SSM_STATE), F32),
            jax.ShapeDtypeStruct((DEC_BATCH, N_GROUPS * SSM_STATE), F32),
        ],
        scratch_shapes=[
            pltpu.VMEM((FOLD_ROWS, CHUNK * LANES), F32),
            pltpu.VMEM((CHUNK * LANES, 2 * ns), BF16),
            pltpu.VMEM((CHUNK * LANES, CHUNK * LANES // 2), BF16),
            pltpu.VMEM((2 * ns, CHUNK * LANES), BF16),
            pltpu.VMEM((ns // LANES, FOLD_ROWS, LANES), F32),
            pltpu.VMEM((ns // LANES, FOLD_ROWS, LANES), F32),
            pltpu.VMEM((FOLD_ROWS, CHUNK * LANES), F32),
            pltpu.VMEM((N_SAMPLE_ROWS, 2 * ns), BF16),
        ],
        compiler_params=_params(1),
        name="s5",
    )(u_p, u_t, u_t, h0_re, h0_im, b_t, c_t, pwl, pwc, d_skip)


def _out_kernel(x_ref, a_ref, sy_ref, g_ref, wga_ref, wgb_ref, bga_ref, bgb_ref, gluw_ref, glub_ref,
                wa_ref, wb_ref, wo_ref, o_ref, xn_ref, s_ref):
    j = pl.program_id(1)

    @pl.when(j == 0)
    def _():
        x = x_ref[...]
        xn_ref[...] = _rms(x, g_ref[...]).astype(BF16)
        o_ref[...] = x
        s = jax.nn.gelu(sy_ref[...])
        z = jnp.dot(s.astype(BF16), gluw_ref[...], preferred_element_type=F32) + glub_ref[...]
        s_ref[...] = (s * jax.nn.sigmoid(z)).astype(BF16)

    xn = xn_ref[...]
    ga = jax.nn.sigmoid(jnp.dot(xn, wga_ref[...], preferred_element_type=F32) + bga_ref[...])
    gb = jax.nn.sigmoid(jnp.dot(xn, wgb_ref[...], preferred_element_type=F32) + bgb_ref[...])
    merged = (ga * jnp.dot(a_ref[...], wa_ref[...], preferred_element_type=F32)
              + gb * jnp.dot(s_ref[...], wb_ref[...], preferred_element_type=F32))
    o_ref[...] += jnp.dot(merged.astype(BF16), wo_ref[...], preferred_element_type=F32)


def _out_proj(x, a, sy, g, w_in, b_gate, glu_w, glu_b, w_a, w_b, w_out, tm):
    rows = x.shape[0]
    nb = D_MODEL // TN_OUT
    g0 = (W_POOL + W_SSM) // TN_OUT
    return pl.pallas_call(
        _out_kernel,
        grid=(rows // tm, nb),
        in_specs=[
            pl.BlockSpec((tm, D_MODEL), lambda i, j: (i, 0)),
            pl.BlockSpec((tm, W_POOL), lambda i, j: (i, 0)),
            pl.BlockSpec((tm, W_SSM), lambda i, j: (i, 0)),
            pl.BlockSpec((1, D_MODEL), lambda i, j: (0, 0)),
            pl.BlockSpec((D_MODEL, TN_OUT), lambda i, j: (0, g0 + j)),
            pl.BlockSpec((D_MODEL, TN_OUT), lambda i, j: (0, g0 + nb + j)),
            pl.BlockSpec((1, TN_OUT), lambda i, j: (0, j)),
            pl.BlockSpec((1, TN_OUT), lambda i, j: (0, nb + j)),
            pl.BlockSpec((W_SSM, W_SSM), lambda i, j: (0, 0)),
            pl.BlockSpec((1, W_SSM), lambda i, j: (0, 0)),
            pl.BlockSpec((W_POOL, TN_OUT), lambda i, j: (0, j)),
            pl.BlockSpec((W_SSM, TN_OUT), lambda i, j: (0, j)),
            pl.BlockSpec((TN_OUT, D_MODEL), lambda i, j: (j, 0)),
        ],
        out_specs=pl.BlockSpec((tm, D_MODEL), lambda i, j: (i, 0)),
        out_shape=jax.ShapeDtypeStruct((rows, D_MODEL), F32),
        scratch_shapes=[pltpu.VMEM((tm, D_MODEL), BF16), pltpu.VMEM((tm, W_SSM), BF16)],
        compiler_params=_params(2),
        name="out_proj",
    )(x, a, sy, g, w_in, w_in, b_gate, b_gate, glu_w, glu_b, w_a, w_b, w_out)


def kernel(x_prompt, x_sample, state_pool, state_ssm_re, state_ssm_im, meta_tokens, norm_ffn1, ffn1_w_gate, ffn1_w_up, ffn1_w_down, norm_mix, w_in, b_gate, pool_w, pool_scale, ssm_lambda_re, ssm_lambda_im, ssm_log_dt, ssm_b_re, ssm_b_im, ssm_c_re, ssm_c_im, ssm_d, glu_w, glu_b, w_branch_a, w_branch_b, w_out, norm_ffn2, ffn2_w_gate, ffn2_w_up, ffn2_w_down, final_norm):
    l = 0
    bf = lambda w: w.astype(BF16)
    row = lambda v: v.reshape(1, -1).astype(F32)
    tiles = ((TM_FFN, TM), (N_TAIL_ROWS, N_TAIL_ROWS))

    xs = (x_prompt.reshape(N_PROMPT_ROWS, D_MODEL),
          jnp.concatenate([jnp.transpose(x_sample, (1, 0, 2)).reshape(N_SAMPLE_ROWS, D_MODEL),
                           meta_tokens.astype(F32)], axis=0))

    fg = row(final_norm)
    w_in_b = bf(w_in[l])
    x1, u = [], []
    for x, (tm_ffn, tm) in zip(xs, tiles):
        x1.append(_ffn(x, row(norm_ffn1[l]), ffn1_w_gate[l], ffn1_w_up[l], ffn1_w_down[l], fg, tm_ffn, False))
        u.append(_inproj(x1[-1], row(norm_mix[l]), w_in_b, tm))
    u_p, u_t = u

    hist_t = jnp.transpose(state_pool[l], (1, 0, 2))
    a_out = _pool(u_p, u_t, hist_t, bf(pool_w[l]), row(pool_scale[l]))

    b_t, c_t, pwl, pwc = _s5_tables(ssm_lambda_re[l], ssm_lambda_im[l], ssm_log_dt[l], ssm_b_re[l],
                                    ssm_b_im[l], ssm_c_re[l], ssm_c_im[l])
    sy_p, sy_t, hp_re, hp_im, hs_re, hs_im = _s5(
        u_p, u_t, state_ssm_re[l].reshape(DEC_BATCH, -1), state_ssm_im[l].reshape(DEC_BATCH, -1),
        b_t, c_t, pwl, pwc, row(ssm_d[l]))

    mix_w = (bf(glu_w[l]), row(glu_b[l]), bf(w_branch_a[l]), bf(w_branch_b[l]), bf(w_out[l]))
    y = []
    for x, a, sy, (tm_ffn, tm) in zip(x1, a_out, (sy_p, sy_t), tiles):
        x2 = _out_proj(x, a, sy, row(norm_mix[l]), w_in_b, row(b_gate[l]), *mix_w, tm)
        y.append(_ffn(x2, row(norm_ffn2[l]), ffn2_w_gate[l], ffn2_w_up[l], ffn2_w_down[l], fg, tm_ffn, True))
    y_p, y_t = y

    y_prompt = y_p.reshape(BATCH, SEQ, D_MODEL)
    y_sample = jnp.transpose(y_t[:N_SAMPLE_ROWS].reshape(DEC_SEQ, DEC_BATCH, D_MODEL), (1, 0, 2))
    pool_p = jnp.stack([u_p[(b + 1) * SEQ - POOL_HIST:(b + 1) * SEQ, :W_POOL] for b in range(BATCH)])[None]
    u_pool_s = jnp.transpose(u_t[:N_SAMPLE_ROWS, :W_POOL].reshape(DEC_SEQ, DEC_BATCH, W_POOL), (1, 0, 2))
    pool_s = jnp.concatenate([state_pool[l][:, DEC_SEQ:], u_pool_s], axis=1)[None]
    shp_p = (1, BATCH, N_GROUPS, SSM_STATE)
    shp_s = (1, DEC_BATCH, N_GROUPS, SSM_STATE)
    return (y_prompt, y_sample, pool_p, pool_s,
            hp_re.reshape(shp_p), hp_im.reshape(shp_p), hs_re.reshape(shp_s), hs_im.reshape(shp_s))
```

```python
import functools

import jax
import jax.numpy as jnp
from jax import lax
from jax.experimental import pallas as pl
from jax.experimental.pallas import tpu as pltpu

F32 = jnp.float32
BF16 = jnp.bfloat16

D_MODEL = 2048
BATCH = 4
SEQ = 2048
DEC_BATCH = 128
DEC_SEQ = 4
N_META = 16
D_FF = 5632
W_POOL = 1024
W_SSM = 1024
POOL_WINDOWS = (2, 4, 8, 16)
POOL_GW = 256
POOL_HIST = 15
SSM_GS = 16
N_GROUPS = 64
SSM_STATE = 64
RMS_EPS = 1e-6

N_PROMPT_ROWS = BATCH * SEQ
N_SAMPLE_ROWS = DEC_BATCH * DEC_SEQ
N_TAIL_ROWS = N_SAMPLE_ROWS + N_META
TM_FFN = 1024
TM = 512
TF = 256
TF_TAIL = 512
TN_OUT = 512
DOWN_CHUNK = 512

LANES = 128
N_SLABS = W_SSM // LANES
SLAB_STATE = (LANES // SSM_GS) * SSM_STATE
CHUNK = 16
N_CHUNKS = (SEQ + N_META) // CHUNK
SEQ_STRIDE = 136
FOLD_ROW0 = 8
FOLD_ROWS = BATCH * SEQ_STRIDE

VMEM_LIMIT = 60 * 1024 * 1024


def _rms(x, g):
    r = lax.rsqrt(jnp.mean(x * x, axis=-1, keepdims=True) + RMS_EPS)
    return x * r * g


def _params(n_axes):
    return pltpu.CompilerParams(dimension_semantics=("arbitrary",) * n_axes, vmem_limit_bytes=VMEM_LIMIT)


def _ffn_kernel(x_ref, g_ref, wg_ref, wu_ref, wd_ref, fg_ref, o_ref, xn_ref, *, final_norm):
    j = pl.program_id(1)

    @pl.when(j == 0)
    def _():
        x = x_ref[...]
        xn_ref[...] = _rms(x, g_ref[...]).astype(BF16)
        o_ref[...] = x

    xn = xn_ref[...]
    gate = jnp.dot(xn, wg_ref[...].astype(BF16), preferred_element_type=F32)
    up = jnp.dot(xn, wu_ref[...].astype(BF16), preferred_element_type=F32)
    h = (gate * jax.nn.sigmoid(gate) * up * 0.5).astype(BF16)
    for n in range(D_MODEL // DOWN_CHUNK):
        cols = slice(n * DOWN_CHUNK, (n + 1) * DOWN_CHUNK)
        o_ref[:, cols] += jnp.dot(h, wd_ref[:, cols].astype(BF16), preferred_element_type=F32)

    if final_norm:
        @pl.when(j == pl.num_programs(1) - 1)
        def _():
            o_ref[...] = _rms(o_ref[...], fg_ref[...])


def _ffn(x, g, wg, wu, wd, fg, tm, tf, final_norm):
    rows = x.shape[0]
    return pl.pallas_call(
        functools.partial(_ffn_kernel, final_norm=final_norm),
        grid=(rows // tm, D_FF // tf),
        in_specs=[
            pl.BlockSpec((tm, D_MODEL), lambda i, j: (i, 0)),
            pl.BlockSpec((1, D_MODEL), lambda i, j: (0, 0)),
            pl.BlockSpec((D_MODEL, tf), lambda i, j: (0, j)),
            pl.BlockSpec((D_MODEL, tf), lambda i, j: (0, j)),
            pl.BlockSpec((tf, D_MODEL), lambda i, j: (j, 0)),
            pl.BlockSpec((1, D_MODEL), lambda i, j: (0, 0)),
        ],
        out_specs=pl.BlockSpec((tm, D_MODEL), lambda i, j: (i, 0)),
        out_shape=jax.ShapeDtypeStruct((rows, D_MODEL), F32),
        scratch_shapes=[pltpu.VMEM((tm, D_MODEL), BF16)],
        compiler_params=_params(2),
        name="ffn_final" if final_norm else "ffn",
    )(x, g, wg, wu, wd, fg)


def _inproj_kernel(x_ref, g_ref, w_ref, o_ref):
    xn = _rms(x_ref[...], g_ref[...]).astype(BF16)
    o_ref[...] = jnp.dot(xn, w_ref[...], preferred_element_type=F32)


def _inproj(x, g, w_in, tm):
    rows = x.shape[0]
    n = W_POOL + W_SSM
    return pl.pallas_call(
        _inproj_kernel,
        grid=(rows // tm,),
        in_specs=[
            pl.BlockSpec((tm, D_MODEL), lambda i: (i, 0)),
            pl.BlockSpec((1, D_MODEL), lambda i: (0, 0)),
            pl.BlockSpec((D_MODEL, n), lambda i: (0, 0)),
        ],
        out_specs=pl.BlockSpec((tm, n), lambda i: (i, 0)),
        out_shape=jax.ShapeDtypeStruct((rows, n), F32),
        compiler_params=_params(1),
        name="inproj",
    )(x, g, w_in)


TILES_PER_SEQ = SEQ // TM
HALO = 16


def _pool_project(mean_ref, u_ref, pw_ref, scale_ref, o_ref):
    for g in range(len(POOL_WINDOWS)):
        cols = slice(g * POOL_GW, (g + 1) * POOL_GW)
        d = (mean_ref[:, cols] - u_ref[:, cols]).astype(BF16)
        y = jnp.dot(d, pw_ref[g], preferred_element_type=F32) * scale_ref[:, cols]
        o_ref[:, cols] = y.astype(BF16)


def _pool_prompt_kernel(u_ref, meta_ref, pw_ref, scale_ref, o_ref, full_ref, mean_ref):
    @pl.when(pl.program_id(0) % TILES_PER_SEQ == 0)
    def _():
        full_ref[0:HALO, :] = meta_ref[...]

    full_ref[HALO:HALO + TM, :] = u_ref[...]
    for g, w in enumerate(POOL_WINDOWS):
        cols = slice(g * POOL_GW, (g + 1) * POOL_GW)
        acc = full_ref[HALO:HALO + TM, cols]
        for k in range(1, w):
            acc = acc + full_ref[HALO - k:HALO - k + TM, cols]
        mean_ref[:, cols] = acc * (1.0 / w)
    full_ref[0:HALO, :] = full_ref[TM:TM + HALO, :]
    _pool_project(mean_ref, u_ref, pw_ref, scale_ref, o_ref)


def _pool_tail_kernel(u_ref, hist_ref, pw_ref, scale_ref, o_ref, mean_ref):
    for g, w in enumerate(POOL_WINDOWS):
        cols = slice(g * POOL_GW, (g + 1) * POOL_GW)
        for t in range(DEC_SEQ):
            acc = None
            for k in range(w):
                p = POOL_HIST + t - k
                if p >= POOL_HIST:
                    q = p - POOL_HIST
                    term = u_ref[q * DEC_BATCH:(q + 1) * DEC_BATCH, cols]
                else:
                    term = hist_ref[p, :, cols]
                acc = term if acc is None else acc + term
            mean_ref[t * DEC_BATCH:(t + 1) * DEC_BATCH, cols] = acc * (1.0 / w)
    mean_ref[N_SAMPLE_ROWS:, :] = u_ref[N_SAMPLE_ROWS:, :]
    _pool_project(mean_ref, u_ref, pw_ref, scale_ref, o_ref)


def _pool(u_p, u_t, hist_t, pool_w, pool_scale):
    pw_spec = pl.BlockSpec((len(POOL_WINDOWS), POOL_GW, POOL_GW), lambda i: (0, 0, 0))
    sc_spec = pl.BlockSpec((1, W_POOL), lambda i: (0, 0))
    a_p = pl.pallas_call(
        _pool_prompt_kernel,
        grid=(N_PROMPT_ROWS // TM,),
        in_specs=[
            pl.BlockSpec((TM, W_POOL), lambda i: (i, 0)),
            pl.BlockSpec((N_META, W_POOL), lambda i: (N_SAMPLE_ROWS // N_META, 0)),
            pw_spec, sc_spec,
        ],
        out_specs=pl.BlockSpec((TM, W_POOL), lambda i: (i, 0)),
        out_shape=jax.ShapeDtypeStruct((N_PROMPT_ROWS, W_POOL), BF16),
        scratch_shapes=[pltpu.VMEM((HALO + TM, W_POOL), F32), pltpu.VMEM((TM, W_POOL), F32)],
        compiler_params=_params(1),
        name="pool_prompt",
    )(u_p, u_t, pool_w, pool_scale)
    a_t = pl.pallas_call(
        _pool_tail_kernel,
        grid=(1,),
        in_specs=[
            pl.BlockSpec((N_TAIL_ROWS, W_POOL), lambda i: (0, 0)),
            pl.BlockSpec((POOL_HIST, DEC_BATCH, W_POOL), lambda i: (0, 0, 0)),
            pw_spec, sc_spec,
        ],
        out_specs=pl.BlockSpec((N_TAIL_ROWS, W_POOL), lambda i: (0, 0)),
        out_shape=jax.ShapeDtypeStruct((N_TAIL_ROWS, W_POOL), BF16),
        scratch_shapes=[pltpu.VMEM((N_TAIL_ROWS, W_POOL), F32)],
        compiler_params=_params(1),
        name="pool_tail",
    )(u_t, hist_t, pool_w, pool_scale)
    return a_p, a_t


def _s5_tables(lam_re, lam_im, log_dt, b_re, b_im, c_re, c_im):
    dt = jnp.exp(log_dt)[:, None]
    k = jnp.arange(CHUNK + 1, dtype=F32)[:, None, None]
    mag = jnp.exp(k * (lam_re * dt)[None])
    ang = k * (lam_im * dt)[None]
    pw_re, pw_im = mag * jnp.cos(ang), mag * jnp.sin(ang)
    lb_re, lb_im = pw_re[1], pw_im[1]
    den = lam_re * lam_re + lam_im * lam_im
    q_re = ((lb_re - 1.0) * lam_re + lb_im * lam_im) / den
    q_im = (lb_im * lam_re - (lb_re - 1.0) * lam_im) / den
    bb_re = q_re[..., None] * b_re - q_im[..., None] * b_im
    bb_im = q_re[..., None] * b_im + q_im[..., None] * b_re
    eye = jnp.eye(LANES // SSM_GS, dtype=F32)
    gl = LANES // SSM_GS

    def bd_b(bb):
        return jnp.einsum("qgpc,gh->qgchp", bb.reshape(N_SLABS, gl, SSM_STATE, SSM_GS), eye).reshape(
            N_SLABS, LANES, SLAB_STATE)

    def bd_ct(cc):
        return jnp.einsum("qgcp,gh->qgchp", cc.reshape(N_SLABS, gl, SSM_GS, SSM_STATE), eye).reshape(
            N_SLABS, LANES, SLAB_STATE)

    b_t = jnp.concatenate([bd_b(bb_re), bd_b(bb_im)], axis=2)
    ct_t = jnp.concatenate([bd_ct(c_re), bd_ct(c_im)], axis=2)
    pr = pw_re.reshape(CHUNK + 1, N_SLABS, SLAB_STATE)
    pi = pw_im.reshape(CHUNK + 1, N_SLABS, SLAB_STATE)
    pwl = jnp.concatenate([jnp.transpose(pr, (1, 0, 2)), jnp.transpose(pi, (1, 0, 2))], axis=2)
    return b_t, ct_t, pwl


_NT = (((1,), (1,)), ((), ()))


def _s5_kernel(up_ref, um_ref, us_ref, h0re_ref, h0im_ref, b_ref, ct_ref, pwl_ref, d_ref,
               yp_ref, yt_ref, hpre_ref, hpim_ref, hsre_ref, hsim_ref,
               l_ref, f_ref, m_ref, et_ref, vre_ref, vim_ref, yf_ref, hs_ref):
    ns = SLAB_STATE
    b_re = b_ref[0][:, :ns]
    b_im = b_ref[0][:, ns:]
    ct_re = ct_ref[0][:, :ns]
    ct_im = ct_ref[0][:, ns:]
    ct_neg = jnp.concatenate([ct_re, -ct_im], axis=1).astype(BF16)

    l_ref[...] = jnp.zeros_like(l_ref)
    for b in range(BATCH):
        r0 = b * SEQ_STRIDE + FOLD_ROW0
        for s in range(CHUNK):
            cols = slice(s * LANES, (s + 1) * LANES)
            l_ref[r0:r0 + SEQ // CHUNK, cols] = up_ref[pl.ds(b * SEQ + s, SEQ // CHUNK, stride=CHUNK), :]
            l_ref[r0 - 1:r0, cols] = um_ref[s:s + 1, :]
    lb = l_ref[...].astype(BF16)

    for s in range(CHUNK):
        p = pwl_ref[0, CHUNK - 1 - s:CHUNK - s, :]
        p_re, p_im = p[:, :ns], p[:, ns:]
        rows = slice(s * LANES, (s + 1) * LANES)
        f_ref[rows, :ns] = (b_re * p_re - b_im * p_im).astype(BF16)
        f_ref[rows, ns:] = (b_re * p_im + b_im * p_re).astype(BF16)
    v = jnp.dot(lb, f_ref[...], preferred_element_type=F32)
    nk = ns // LANES
    for k in range(nk):
        vre_ref[k] = v[:, k * LANES:(k + 1) * LANES]
        vim_ref[k] = v[:, ns + k * LANES:ns + (k + 1) * LANES]

    a = pwl_ref[0, CHUNK:CHUNK + 1, :]
    a_re = [jnp.broadcast_to(a[:, k * LANES:(k + 1) * LANES], (BATCH, LANES)) for k in range(nk)]
    a_im = [jnp.broadcast_to(a[:, ns + k * LANES:ns + (k + 1) * LANES], (BATCH, LANES)) for k in range(nk)]
    h_re = [jnp.zeros((BATCH, LANES), F32) for _ in range(nk)]
    h_im = [jnp.zeros((BATCH, LANES), F32) for _ in range(nk)]
    for j in range(N_CHUNKS):
        rows = pl.ds(FOLD_ROW0 - 1 + j, BATCH, stride=SEQ_STRIDE)
        for k in range(nk):
            v_re = vre_ref[k, rows, :]
            v_im = vim_ref[k, rows, :]
            vre_ref[k, rows, :] = h_re[k]
            vim_ref[k, rows, :] = h_im[k]
            h_re[k], h_im[k] = (a_re[k] * h_re[k] - a_im[k] * h_im[k] + v_re,
                                a_re[k] * h_im[k] + a_im[k] * h_re[k] + v_im)
    for k in range(nk):
        hpre_ref[:, k * LANES:(k + 1) * LANES] = h_re[k]
        hpim_ref[:, k * LANES:(k + 1) * LANES] = h_im[k]

    kb_all = lax.dot_general(f_ref[...], ct_neg, _NT, preferred_element_type=F32)
    ri = lax.broadcasted_iota(jnp.int32, (LANES, LANES), 0)
    ci = lax.broadcasted_iota(jnp.int32, (LANES, LANES), 1)
    d_diag = jnp.where(ri == ci, jnp.broadcast_to(d_ref[...], (LANES, LANES)), 0.0)
    kb = [None] * CHUNK
    for s in range(CHUNK):
        blk = kb_all[s * LANES:(s + 1) * LANES, :]
        if s == CHUNK - 1:
            blk = blk + d_diag
        kb[CHUNK - 1 - s] = blk.astype(BF16)

    half = CHUNK // 2
    for s in range(CHUNK):
        for t in range(half, CHUNK):
            blk = kb[t - s] if t >= s else jnp.zeros((LANES, LANES), BF16)
            m_ref[s * LANES:(s + 1) * LANES, (t - half) * LANES:(t - half + 1) * LANES] = blk
    y_left = jnp.dot(lb[:, :half * LANES], m_ref[half * LANES:, :], preferred_element_type=F32)
    y_right = jnp.dot(lb, m_ref[...], preferred_element_type=F32)

    for t in range(CHUNK):
        p = pwl_ref[0, t + 1:t + 2, :]
        p_re, p_im = p[:, :ns], p[:, ns:]
        rows = slice(t * LANES, (t + 1) * LANES)
        et_ref[rows, :ns] = (ct_re * p_re - ct_im * p_im).astype(BF16)
        et_ref[rows, ns:] = (-(ct_re * p_im + ct_im * p_re)).astype(BF16)
    h_in = jnp.concatenate([vre_ref[k] for k in range(nk)] + [vim_ref[k] for k in range(nk)],
                           axis=1).astype(BF16)
    ys = lax.dot_general(h_in, et_ref[...], _NT, preferred_element_type=F32)
    yf_ref[:, :half * LANES] = y_left + ys[:, :half * LANES]
    yf_ref[:, half * LANES:] = y_right + ys[:, half * LANES:]
    for b in range(BATCH):
        r0 = b * SEQ_STRIDE + FOLD_ROW0
        for t in range(CHUNK):
            yp_ref[pl.ds(b * SEQ + t, SEQ // CHUNK, stride=CHUNK), :] = (
                yf_ref[r0:r0 + SEQ // CHUNK, t * LANES:(t + 1) * LANES])

    u = us_ref[...]
    bu = jnp.dot(u.astype(BF16), b_ref[0].astype(BF16), preferred_element_type=F32)
    l1 = pwl_ref[0, 1:2, :]
    l_re, l_im = l1[:, :ns], l1[:, ns:]
    g_re = h0re_ref[...]
    g_im = h0im_ref[...]
    for t in range(DEC_SEQ):
        rows = slice(t * DEC_BATCH, (t + 1) * DEC_BATCH)
        g_re, g_im = (l_re * g_re - l_im * g_im + bu[rows, :ns],
                      l_re * g_im + l_im * g_re + bu[rows, ns:])
        hs_ref[rows, :ns] = g_re.astype(BF16)
        hs_ref[rows, ns:] = g_im.astype(BF16)
    hsre_ref[...] = g_re
    hsim_ref[...] = g_im
    yt_ref[0:N_SAMPLE_ROWS, :] = (lax.dot_general(hs_ref[...], ct_neg, _NT, preferred_element_type=F32)
                                  + d_ref[...] * u)
    yt_ref[N_SAMPLE_ROWS:, :] = jnp.zeros((N_META, LANES), F32)


def _s5(u_p, u_t, h0_re, h0_im, b_t, ct_t, pwl, d_skip):
    c0 = W_POOL // LANES
    ns = SLAB_STATE
    state = lambda rows: pl.BlockSpec((rows, ns), lambda i: (0, i))
    return pl.pallas_call(
        _s5_kernel,
        grid=(N_SLABS,),
        in_specs=[
            pl.BlockSpec((N_PROMPT_ROWS, LANES), lambda i: (0, c0 + i)),
            pl.BlockSpec((N_META, LANES), lambda i: (N_SAMPLE_ROWS // N_META, c0 + i)),
            pl.BlockSpec((N_SAMPLE_ROWS, LANES), lambda i: (0, c0 + i)),
            state(DEC_BATCH), state(DEC_BATCH),
            pl.BlockSpec((1, LANES, 2 * ns), lambda i: (i, 0, 0)),
            pl.BlockSpec((1, LANES, 2 * ns), lambda i: (i, 0, 0)),
            pl.BlockSpec((1, CHUNK + 1, 2 * ns), lambda i: (i, 0, 0)),
            pl.BlockSpec((1, LANES), lambda i: (0, i)),
        ],
        out_specs=[
            pl.BlockSpec((N_PROMPT_ROWS, LANES), lambda i: (0, i)),
            pl.BlockSpec((N_TAIL_ROWS, LANES), lambda i: (0, i)),
            state(BATCH), state(BATCH), state(DEC_BATCH), state(DEC_BATCH),
        ],
        out_shape=[
            jax.ShapeDtypeStruct((N_PROMPT_ROWS, W_SSM), F32),
            jax.ShapeDtypeStruct((N_TAIL_ROWS, W_SSM), F32),
            jax.ShapeDtypeStruct((BATCH, N_GROUPS * SSM_STATE), F32),
            jax.ShapeDtypeStruct((BATCH, N_GROUPS * SSM_STATE), F32),
            jax.ShapeDtypeStruct((DEC_BATCH, N_GROUPS * SSM_STATE), F32),
            jax.ShapeDtypeStruct((DEC_BATCH, N_GROUPS * SSM_STATE), F32),
        ],
        scratch_shapes=[
            pltpu.VMEM((FOLD_ROWS, CHUNK * LANES), F32),
            pltpu.VMEM((CHUNK * LANES, 2 * ns), BF16),
            pltpu.VMEM((CHUNK * LANES, CHUNK * LANES // 2), BF16),
            pltpu.VMEM((CHUNK * LANES, 2 * ns), BF16),
            pltpu.VMEM((ns // LANES, FOLD_ROWS, LANES), F32),
            pltpu.VMEM((ns // LANES, FOLD_ROWS, LANES), F32),
            pltpu.VMEM((FOLD_ROWS, CHUNK * LANES), F32),
            pltpu.VMEM((N_SAMPLE_ROWS, 2 * ns), BF16),
        ],
        compiler_params=_params(1),
        name="s5",
    )(u_p, u_t, u_t, h0_re, h0_im, b_t, ct_t, pwl, d_skip)


def _out_kernel(x_ref, a_ref, sy_ref, g_ref, wga_ref, wgb_ref, bga_ref, bgb_ref, gluw_ref, glub_ref,
                wa_ref, wb_ref, wo_ref, o_ref, xn_ref, s_ref):
    j = pl.program_id(1)

    @pl.when(j == 0)
    def _():
        x = x_ref[...]
        xn_ref[...] = _rms(x, g_ref[...]).astype(BF16)
        o_ref[...] = x
        s = jax.nn.gelu(sy_ref[...])
        z = jnp.dot(s.astype(BF16), gluw_ref[...], preferred_element_type=F32) + glub_ref[...]
        s_ref[...] = (s * jax.nn.sigmoid(z)).astype(BF16)

    xn = xn_ref[...]
    ga = jax.nn.sigmoid(jnp.dot(xn, wga_ref[...], preferred_element_type=F32) + bga_ref[...])
    gb = jax.nn.sigmoid(jnp.dot(xn, wgb_ref[...], preferred_element_type=F32) + bgb_ref[...])
    merged = (ga * jnp.dot(a_ref[...], wa_ref[...], preferred_element_type=F32)
              + gb * jnp.dot(s_ref[...], wb_ref[...], preferred_element_type=F32))
    o_ref[...] += jnp.dot(merged.astype(BF16), wo_ref[...], preferred_element_type=F32)


def _out_proj(x, a, sy, g, w_in, b_gate, glu_w, glu_b, w_a, w_b, w_out, tm):
    rows = x.shape[0]
    nb = D_MODEL // TN_OUT
    g0 = (W_POOL + W_SSM) // TN_OUT
    return pl.pallas_call(
        _out_kernel,
        grid=(rows // tm, nb),
        in_specs=[
            pl.BlockSpec((tm, D_MODEL), lambda i, j: (i, 0)),
            pl.BlockSpec((tm, W_POOL), lambda i, j: (i, 0)),
            pl.BlockSpec((tm, W_SSM), lambda i, j: (i, 0)),
            pl.BlockSpec((1, D_MODEL), lambda i, j: (0, 0)),
            pl.BlockSpec((D_MODEL, TN_OUT), lambda i, j: (0, g0 + j)),
            pl.BlockSpec((D_MODEL, TN_OUT), lambda i, j: (0, g0 + nb + j)),
            pl.BlockSpec((1, TN_OUT), lambda i, j: (0, j)),
            pl.BlockSpec((1, TN_OUT), lambda i, j: (0, nb + j)),
            pl.BlockSpec((W_SSM, W_SSM), lambda i, j: (0, 0)),
            pl.BlockSpec((1, W_SSM), lambda i, j: (0, 0)),
            pl.BlockSpec((W_POOL, TN_OUT), lambda i, j: (0, j)),
            pl.BlockSpec((W_SSM, TN_OUT), lambda i, j: (0, j)),
            pl.BlockSpec((TN_OUT, D_MODEL), lambda i, j: (j, 0)),
        ],
        out_specs=pl.BlockSpec((tm, D_MODEL), lambda i, j: (i, 0)),
        out_shape=jax.ShapeDtypeStruct((rows, D_MODEL), F32),
        scratch_shapes=[pltpu.VMEM((tm, D_MODEL), BF16), pltpu.VMEM((tm, W_SSM), BF16)],
        compiler_params=_params(2),
        name="out_proj",
    )(x, a, sy, g, w_in, w_in, b_gate, b_gate, glu_w, glu_b, w_a, w_b, w_out)


def kernel(x_prompt, x_sample, state_pool, state_ssm_re, state_ssm_im, meta_tokens, norm_ffn1, ffn1_w_gate, ffn1_w_up, ffn1_w_down, norm_mix, w_in, b_gate, pool_w, pool_scale, ssm_lambda_re, ssm_lambda_im, ssm_log_dt, ssm_b_re, ssm_b_im, ssm_c_re, ssm_c_im, ssm_d, glu_w, glu_b, w_branch_a, w_branch_b, w_out, norm_ffn2, ffn2_w_gate, ffn2_w_up, ffn2_w_down, final_norm):
    l = 0
    bf = lambda w: w.astype(BF16)
    row = lambda v: v.reshape(1, -1).astype(F32)
    tiles = (((TM_FFN, TF), TM), ((N_TAIL_ROWS, TF_TAIL), N_TAIL_ROWS))

    xs = (x_prompt.reshape(N_PROMPT_ROWS, D_MODEL),
          jnp.concatenate([jnp.transpose(x_sample, (1, 0, 2)).reshape(N_SAMPLE_ROWS, D_MODEL),
                           meta_tokens.astype(F32)], axis=0))

    fg = row(final_norm)
    w_in_b = bf(w_in[l])
    x1, u = [], []
    for x, (tm_ffn, tm) in zip(xs, tiles):
        x1.append(_ffn(x, row(norm_ffn1[l]), ffn1_w_gate[l], ffn1_w_up[l], ffn1_w_down[l], fg, *tm_ffn, False))
        u.append(_inproj(x1[-1], row(norm_mix[l]), w_in_b, tm))
    u_p, u_t = u

    hist_t = jnp.transpose(state_pool[l], (1, 0, 2))
    a_out = _pool(u_p, u_t, hist_t, bf(pool_w[l]), row(pool_scale[l]))

    b_t, ct_t, pwl = _s5_tables(ssm_lambda_re[l], ssm_lambda_im[l], ssm_log_dt[l], ssm_b_re[l],
                                ssm_b_im[l], ssm_c_re[l], ssm_c_im[l])
    sy_p, sy_t, hp_re, hp_im, hs_re, hs_im = _s5(
        u_p, u_t, state_ssm_re[l].reshape(DEC_BATCH, -1), state_ssm_im[l].reshape(DEC_BATCH, -1),
        b_t, ct_t, pwl, row(ssm_d[l]))

    mix_w = (bf(glu_w[l]), row(glu_b[l]), bf(w_branch_a[l]), bf(w_branch_b[l]), bf(w_out[l]))
    y = []
    for x, a, sy, (tm_ffn, tm) in zip(x1, a_out, (sy_p, sy_t), tiles):
        x2 = _out_proj(x, a, sy, row(norm_mix[l]), w_in_b, row(b_gate[l]), *mix_w, tm)
        y.append(_ffn(x2, row(norm_ffn2[l]), ffn2_w_gate[l], ffn2_w_up[l], ffn2_w_down[l], fg, *tm_ffn, True))
    y_p, y_t = y

    y_prompt = y_p.reshape(BATCH, SEQ, D_MODEL)
    y_sample = jnp.transpose(y_t[:N_SAMPLE_ROWS].reshape(DEC_SEQ, DEC_BATCH, D_MODEL), (1, 0, 2))
    pool_p = jnp.stack([u_p[(b + 1) * SEQ - POOL_HIST:(b + 1) * SEQ, :W_POOL] for b in range(BATCH)])[None]
    u_pool_s = jnp.transpose(u_t[:N_SAMPLE_ROWS, :W_POOL].reshape(DEC_SEQ, DEC_BATCH, W_POOL), (1, 0, 2))
    pool_s = jnp.concatenate([state_pool[l][:, DEC_SEQ:], u_pool_s], axis=1)[None]
    shp_p = (1, BATCH, N_GROUPS, SSM_STATE)
    shp_s = (1, DEC_BATCH, N_GROUPS, SSM_STATE)
    return (y_prompt, y_sample, pool_p, pool_s,
            hp_re.reshape(shp_p), hp_im.reshape(shp_p), hs_re.reshape(shp_s), hs_im.reshape(shp_s))
```

```python
import functools

import jax
import jax.numpy as jnp
from jax import lax
from jax.experimental import pallas as pl
from jax.experimental.pallas import tpu as pltpu

F32 = jnp.float32
BF16 = jnp.bfloat16

D_MODEL = 2048
BATCH = 4
SEQ = 2048
DEC_BATCH = 128
DEC_SEQ = 4
N_META = 16
D_FF = 5632
W_POOL = 1024
W_SSM = 1024
POOL_WINDOWS = (2, 4, 8, 16)
POOL_GW = 256
POOL_HIST = 15
SSM_GS = 16
N_GROUPS = 64
SSM_STATE = 64
RMS_EPS = 1e-6

N_PROMPT_ROWS = BATCH * SEQ
N_SAMPLE_ROWS = DEC_BATCH * DEC_SEQ
N_TAIL_ROWS = N_SAMPLE_ROWS + N_META
TM_FFN = 1024
TM = 512
TF = 256
TF_TAIL = 512
TN_OUT = 512
DOWN_CHUNK = 512

LANES = 128
N_SLABS = W_SSM // LANES
SLAB_STATE = (LANES // SSM_GS) * SSM_STATE
CHUNK = 8
N_META_CHUNKS = N_META // CHUNK
N_CHUNKS = (SEQ + N_META) // CHUNK
FOLD_ROW0 = 8
SEQ_STRIDE = FOLD_ROW0 + SEQ // CHUNK
FOLD_ROWS = BATCH * SEQ_STRIDE

VMEM_LIMIT = 60 * 1024 * 1024


def _rms(x, g):
    r = lax.rsqrt(jnp.mean(x * x, axis=-1, keepdims=True) + RMS_EPS)
    return x * r * g


def _params(n_axes):
    return pltpu.CompilerParams(dimension_semantics=("arbitrary",) * n_axes, vmem_limit_bytes=VMEM_LIMIT)


def _ffn_kernel(x_ref, g_ref, wg_ref, wu_ref, wd_ref, fg_ref, o_ref, xn_ref, *, final_norm):
    j = pl.program_id(1)

    @pl.when(j == 0)
    def _():
        x = x_ref[...]
        xn_ref[...] = _rms(x, g_ref[...]).astype(BF16)
        o_ref[...] = x

    xn = xn_ref[...]
    gate = jnp.dot(xn, wg_ref[...].astype(BF16), preferred_element_type=F32)
    up = jnp.dot(xn, wu_ref[...].astype(BF16), preferred_element_type=F32)
    h = (gate * jax.nn.sigmoid(gate) * up * 0.5).astype(BF16)
    for n in range(D_MODEL // DOWN_CHUNK):
        cols = slice(n * DOWN_CHUNK, (n + 1) * DOWN_CHUNK)
        o_ref[:, cols] += jnp.dot(h, wd_ref[:, cols].astype(BF16), preferred_element_type=F32)

    if final_norm:
        @pl.when(j == pl.num_programs(1) - 1)
        def _():
            o_ref[...] = _rms(o_ref[...], fg_ref[...])


def _ffn(x, g, wg, wu, wd, fg, tm, tf, final_norm):
    rows = x.shape[0]
    return pl.pallas_call(
        functools.partial(_ffn_kernel, final_norm=final_norm),
        grid=(rows // tm, D_FF // tf),
        in_specs=[
            pl.BlockSpec((tm, D_MODEL), lambda i, j: (i, 0)),
            pl.BlockSpec((1, D_MODEL), lambda i, j: (0, 0)),
            pl.BlockSpec((D_MODEL, tf), lambda i, j: (0, j)),
            pl.BlockSpec((D_MODEL, tf), lambda i, j: (0, j)),
            pl.BlockSpec((tf, D_MODEL), lambda i, j: (j, 0)),
            pl.BlockSpec((1, D_MODEL), lambda i, j: (0, 0)),
        ],
        out_specs=pl.BlockSpec((tm, D_MODEL), lambda i, j: (i, 0)),
        out_shape=jax.ShapeDtypeStruct((rows, D_MODEL), F32),
        scratch_shapes=[pltpu.VMEM((tm, D_MODEL), BF16)],
        compiler_params=_params(2),
        name="ffn_final" if final_norm else "ffn",
    )(x, g, wg, wu, wd, fg)


def _inproj_kernel(x_ref, g_ref, w_ref, o_ref):
    xn = _rms(x_ref[...], g_ref[...]).astype(BF16)
    o_ref[...] = jnp.dot(xn, w_ref[...], preferred_element_type=F32)


def _inproj(x, g, w_in, tm):
    rows = x.shape[0]
    n = W_POOL + W_SSM
    return pl.pallas_call(
        _inproj_kernel,
        grid=(rows // tm,),
        in_specs=[
            pl.BlockSpec((tm, D_MODEL), lambda i: (i, 0)),
            pl.BlockSpec((1, D_MODEL), lambda i: (0, 0)),
            pl.BlockSpec((D_MODEL, n), lambda i: (0, 0)),
        ],
        out_specs=pl.BlockSpec((tm, n), lambda i: (i, 0)),
        out_shape=jax.ShapeDtypeStruct((rows, n), F32),
        compiler_params=_params(1),
        name="inproj",
    )(x, g, w_in)


TILES_PER_SEQ = SEQ // TM
HALO = 16


def _pool_project(mean_ref, u_ref, pw_ref, scale_ref, o_ref):
    for g in range(len(POOL_WINDOWS)):
        cols = slice(g * POOL_GW, (g + 1) * POOL_GW)
        d = (mean_ref[:, cols] - u_ref[:, cols]).astype(BF16)
        y = jnp.dot(d, pw_ref[g], preferred_element_type=F32) * scale_ref[:, cols]
        o_ref[:, cols] = y.astype(BF16)


def _inproj_pool_kernel(x_ref, g_ref, w_ref, meta_ref, pw_ref, scale_ref, u_ref, o_ref, full_ref, mean_ref):
    @pl.when(pl.program_id(0) % TILES_PER_SEQ == 0)
    def _():
        full_ref[0:HALO, :] = meta_ref[...]

    xn = _rms(x_ref[...], g_ref[...]).astype(BF16)
    u_pool = jnp.dot(xn, w_ref[:, :W_POOL], preferred_element_type=F32)
    u_ref[:, :W_POOL] = u_pool
    u_ref[:, W_POOL:] = jnp.dot(xn, w_ref[:, W_POOL:], preferred_element_type=F32)
    full_ref[HALO:HALO + TM, :] = u_pool
    for g, w in enumerate(POOL_WINDOWS):
        cols = slice(g * POOL_GW, (g + 1) * POOL_GW)
        acc = full_ref[HALO:HALO + TM, cols]
        for k in range(1, w):
            acc = acc + full_ref[HALO - k:HALO - k + TM, cols]
        mean_ref[:, cols] = acc * (1.0 / w)
    _pool_project(mean_ref, full_ref.at[HALO:HALO + TM], pw_ref, scale_ref, o_ref)
    full_ref[0:HALO, :] = full_ref[TM:TM + HALO, :]


def _pool_tail_kernel(u_ref, hist_ref, pw_ref, scale_ref, o_ref, mean_ref):
    for g, w in enumerate(POOL_WINDOWS):
        cols = slice(g * POOL_GW, (g + 1) * POOL_GW)
        for t in range(DEC_SEQ):
            acc = None
            for k in range(w):
                p = POOL_HIST + t - k
                if p >= POOL_HIST:
                    q = p - POOL_HIST
                    term = u_ref[q * DEC_BATCH:(q + 1) * DEC_BATCH, cols]
                else:
                    term = hist_ref[p, :, cols]
                acc = term if acc is None else acc + term
            mean_ref[t * DEC_BATCH:(t + 1) * DEC_BATCH, cols] = acc * (1.0 / w)
    mean_ref[N_SAMPLE_ROWS:, :] = u_ref[N_SAMPLE_ROWS:, :]
    _pool_project(mean_ref, u_ref, pw_ref, scale_ref, o_ref)


_PW_SPEC = pl.BlockSpec((len(POOL_WINDOWS), POOL_GW, POOL_GW), lambda i: (0, 0, 0))
_SCALE_SPEC = pl.BlockSpec((1, W_POOL), lambda i: (0, 0))


def _inproj_pool_prompt(x, g, w_in, u_t, pool_w, pool_scale):
    n = W_POOL + W_SSM
    return pl.pallas_call(
        _inproj_pool_kernel,
        grid=(N_PROMPT_ROWS // TM,),
        in_specs=[
            pl.BlockSpec((TM, D_MODEL), lambda i: (i, 0)),
            pl.BlockSpec((1, D_MODEL), lambda i: (0, 0)),
            pl.BlockSpec((D_MODEL, n), lambda i: (0, 0)),
            pl.BlockSpec((N_META, W_POOL), lambda i: (N_SAMPLE_ROWS // N_META, 0)),
            _PW_SPEC, _SCALE_SPEC,
        ],
        out_specs=[pl.BlockSpec((TM, n), lambda i: (i, 0)), pl.BlockSpec((TM, W_POOL), lambda i: (i, 0))],
        out_shape=[jax.ShapeDtypeStruct((N_PROMPT_ROWS, n), F32),
                   jax.ShapeDtypeStruct((N_PROMPT_ROWS, W_POOL), BF16)],
        scratch_shapes=[pltpu.VMEM((HALO + TM, W_POOL), F32), pltpu.VMEM((TM, W_POOL), F32)],
        compiler_params=_params(1),
        name="inproj_pool",
    )(x, g, w_in, u_t, pool_w, pool_scale)


def _pool_tail(u_t, hist_t, pool_w, pool_scale):
    return pl.pallas_call(
        _pool_tail_kernel,
        grid=(1,),
        in_specs=[
            pl.BlockSpec((N_TAIL_ROWS, W_POOL), lambda i: (0, 0)),
            pl.BlockSpec((POOL_HIST, DEC_BATCH, W_POOL), lambda i: (0, 0, 0)),
            _PW_SPEC, _SCALE_SPEC,
        ],
        out_specs=pl.BlockSpec((N_TAIL_ROWS, W_POOL), lambda i: (0, 0)),
        out_shape=jax.ShapeDtypeStruct((N_TAIL_ROWS, W_POOL), BF16),
        scratch_shapes=[pltpu.VMEM((N_TAIL_ROWS, W_POOL), F32)],
        compiler_params=_params(1),
        name="pool_tail",
    )(u_t, hist_t, pool_w, pool_scale)


def _s5_tables(lam_re, lam_im, log_dt, b_re, b_im, c_re, c_im):
    dt = jnp.exp(log_dt)[:, None]
    k = jnp.arange(CHUNK + 1, dtype=F32)[:, None, None]
    mag = jnp.exp(k * (lam_re * dt)[None])
    ang = k * (lam_im * dt)[None]
    pw_re, pw_im = mag * jnp.cos(ang), mag * jnp.sin(ang)
    lb_re, lb_im = pw_re[1], pw_im[1]
    den = lam_re * lam_re + lam_im * lam_im
    q_re = ((lb_re - 1.0) * lam_re + lb_im * lam_im) / den
    q_im = (lb_im * lam_re - (lb_re - 1.0) * lam_im) / den
    bb_re = q_re[..., None] * b_re - q_im[..., None] * b_im
    bb_im = q_re[..., None] * b_im + q_im[..., None] * b_re
    eye = jnp.eye(LANES // SSM_GS, dtype=F32)
    gl = LANES // SSM_GS

    def bd_b(bb):
        return jnp.einsum("qgpc,gh->qgchp", bb.reshape(N_SLABS, gl, SSM_STATE, SSM_GS), eye).reshape(
            N_SLABS, LANES, SLAB_STATE)

    def bd_ct(cc):
        return jnp.einsum("qgcp,gh->qgchp", cc.reshape(N_SLABS, gl, SSM_GS, SSM_STATE), eye).reshape(
            N_SLABS, LANES, SLAB_STATE)

    b_t = jnp.concatenate([bd_b(bb_re), bd_b(bb_im)], axis=2)
    ct_t = jnp.concatenate([bd_ct(c_re), bd_ct(c_im)], axis=2)
    pr = pw_re.reshape(CHUNK + 1, N_SLABS, SLAB_STATE)
    pi = pw_im.reshape(CHUNK + 1, N_SLABS, SLAB_STATE)
    pwl = jnp.concatenate([jnp.transpose(pr, (1, 0, 2)), jnp.transpose(pi, (1, 0, 2))], axis=2)
    return b_t, ct_t, pwl


_NT = (((1,), (1,)), ((), ()))


def _s5_kernel(up_ref, um_ref, us_ref, h0re_ref, h0im_ref, b_ref, ct_ref, pwl_ref, d_ref,
               yp_ref, yt_ref, hpre_ref, hpim_ref, hsre_ref, hsim_ref,
               l_ref, f_ref, m_ref, et_ref, vre_ref, vim_ref, yf_ref, hs_ref):
    ns = SLAB_STATE
    b_re = b_ref[0][:, :ns]
    b_im = b_ref[0][:, ns:]
    ct_re = ct_ref[0][:, :ns]
    ct_im = ct_ref[0][:, ns:]
    ct_neg = jnp.concatenate([ct_re, -ct_im], axis=1).astype(BF16)

    l_ref[...] = jnp.zeros_like(l_ref)
    for b in range(BATCH):
        r0 = b * SEQ_STRIDE + FOLD_ROW0
        for s in range(CHUNK):
            cols = slice(s * LANES, (s + 1) * LANES)
            l_ref[r0:r0 + SEQ // CHUNK, cols] = up_ref[pl.ds(b * SEQ + s, SEQ // CHUNK, stride=CHUNK), :]
            for m in range(N_META_CHUNKS):
                r = r0 - N_META_CHUNKS + m
                l_ref[r:r + 1, cols] = um_ref[m * CHUNK + s:m * CHUNK + s + 1, :]
    lb = l_ref[...].astype(BF16)

    for s in range(CHUNK):
        p = pwl_ref[0, CHUNK - 1 - s:CHUNK - s, :]
        p_re, p_im = p[:, :ns], p[:, ns:]
        rows = slice(s * LANES, (s + 1) * LANES)
        f_ref[rows, :ns] = (b_re * p_re - b_im * p_im).astype(BF16)
        f_ref[rows, ns:] = (b_re * p_im + b_im * p_re).astype(BF16)
    v = jnp.dot(lb, f_ref[...], preferred_element_type=F32)
    nk = ns // LANES
    for k in range(nk):
        vre_ref[k] = v[:, k * LANES:(k + 1) * LANES]
        vim_ref[k] = v[:, ns + k * LANES:ns + (k + 1) * LANES]

    a = pwl_ref[0, CHUNK:CHUNK + 1, :]
    a_re = [jnp.broadcast_to(a[:, k * LANES:(k + 1) * LANES], (BATCH, LANES)) for k in range(nk)]
    a_im = [jnp.broadcast_to(a[:, ns + k * LANES:ns + (k + 1) * LANES], (BATCH, LANES)) for k in range(nk)]
    h_re = [jnp.zeros((BATCH, LANES), F32) for _ in range(nk)]
    h_im = [jnp.zeros((BATCH, LANES), F32) for _ in range(nk)]
    for j in range(N_CHUNKS):
        rows = pl.ds(FOLD_ROW0 - N_META_CHUNKS + j, BATCH, stride=SEQ_STRIDE)
        for k in range(nk):
            v_re = vre_ref[k, rows, :]
            v_im = vim_ref[k, rows, :]
            vre_ref[k, rows, :] = h_re[k]
            vim_ref[k, rows, :] = h_im[k]
            h_re[k], h_im[k] = (a_re[k] * h_re[k] - a_im[k] * h_im[k] + v_re,
                                a_re[k] * h_im[k] + a_im[k] * h_re[k] + v_im)
    for k in range(nk):
        hpre_ref[:, k * LANES:(k + 1) * LANES] = h_re[k]
        hpim_ref[:, k * LANES:(k + 1) * LANES] = h_im[k]

    kb_all = lax.dot_general(f_ref[...], ct_neg, _NT, preferred_element_type=F32)
    ri = lax.broadcasted_iota(jnp.int32, (LANES, LANES), 0)
    ci = lax.broadcasted_iota(jnp.int32, (LANES, LANES), 1)
    d_diag = jnp.where(ri == ci, jnp.broadcast_to(d_ref[...], (LANES, LANES)), 0.0)
    kb = [None] * CHUNK
    for s in range(CHUNK):
        blk = kb_all[s * LANES:(s + 1) * LANES, :]
        if s == CHUNK - 1:
            blk = blk + d_diag
        kb[CHUNK - 1 - s] = blk.astype(BF16)

    half = CHUNK // 2
    for s in range(CHUNK):
        for t in range(half, CHUNK):
            blk = kb[t - s] if t >= s else jnp.zeros((LANES, LANES), BF16)
            m_ref[s * LANES:(s + 1) * LANES, (t - half) * LANES:(t - half + 1) * LANES] = blk
    y_left = jnp.dot(lb[:, :half * LANES], m_ref[half * LANES:, :], preferred_element_type=F32)
    y_right = jnp.dot(lb, m_ref[...], preferred_element_type=F32)

    for t in range(CHUNK):
        p = pwl_ref[0, t + 1:t + 2, :]
        p_re, p_im = p[:, :ns], p[:, ns:]
        rows = slice(t * LANES, (t + 1) * LANES)
        et_ref[rows, :ns] = (ct_re * p_re - ct_im * p_im).astype(BF16)
        et_ref[rows, ns:] = (-(ct_re * p_im + ct_im * p_re)).astype(BF16)
    h_in = jnp.concatenate([vre_ref[k] for k in range(nk)] + [vim_ref[k] for k in range(nk)],
                           axis=1).astype(BF16)
    ys = lax.dot_general(h_in, et_ref[...], _NT, preferred_element_type=F32)
    yf_ref[:, :half * LANES] = y_left + ys[:, :half * LANES]
    yf_ref[:, half * LANES:] = y_right + ys[:, half * LANES:]
    for b in range(BATCH):
        r0 = b * SEQ_STRIDE + FOLD_ROW0
        for t in range(CHUNK):
            yp_ref[pl.ds(b * SEQ + t, SEQ // CHUNK, stride=CHUNK), :] = (
                yf_ref[r0:r0 + SEQ // CHUNK, t * LANES:(t + 1) * LANES])

    u = us_ref[...]
    bu = jnp.dot(u.astype(BF16), b_ref[0].astype(BF16), preferred_element_type=F32)
    l1 = pwl_ref[0, 1:2, :]
    l_re, l_im = l1[:, :ns], l1[:, ns:]
    g_re = h0re_ref[...]
    g_im = h0im_ref[...]
    for t in range(DEC_SEQ):
        rows = slice(t * DEC_BATCH, (t + 1) * DEC_BATCH)
        g_re, g_im = (l_re * g_re - l_im * g_im + bu[rows, :ns],
                      l_re * g_im + l_im * g_re + bu[rows, ns:])
        hs_ref[rows, :ns] = g_re.astype(BF16)
        hs_ref[rows, ns:] = g_im.astype(BF16)
    hsre_ref[...] = g_re
    hsim_ref[...] = g_im
    yt_ref[0:N_SAMPLE_ROWS, :] = (lax.dot_general(hs_ref[...], ct_neg, _NT, preferred_element_type=F32)
                                  + d_ref[...] * u)
    yt_ref[N_SAMPLE_ROWS:, :] = jnp.zeros((N_META, LANES), F32)


def _s5(u_p, u_t, h0_re, h0_im, b_t, ct_t, pwl, d_skip):
    c0 = W_POOL // LANES
    ns = SLAB_STATE
    state = lambda rows: pl.BlockSpec((rows, ns), lambda i: (0, i))
    return pl.pallas_call(
        _s5_kernel,
        grid=(N_SLABS,),
        in_specs=[
            pl.BlockSpec((N_PROMPT_ROWS, LANES), lambda i: (0, c0 + i)),
            pl.BlockSpec((N_META, LANES), lambda i: (N_SAMPLE_ROWS // N_META, c0 + i)),
            pl.BlockSpec((N_SAMPLE_ROWS, LANES), lambda i: (0, c0 + i)),
            state(DEC_BATCH), state(DEC_BATCH),
            pl.BlockSpec((1, LANES, 2 * ns), lambda i: (i, 0, 0)),
            pl.BlockSpec((1, LANES, 2 * ns), lambda i: (i, 0, 0)),
            pl.BlockSpec((1, CHUNK + 1, 2 * ns), lambda i: (i, 0, 0)),
            pl.BlockSpec((1, LANES), lambda i: (0, i)),
        ],
        out_specs=[
            pl.BlockSpec((N_PROMPT_ROWS, LANES), lambda i: (0, i)),
            pl.BlockSpec((N_TAIL_ROWS, LANES), lambda i: (0, i)),
            state(BATCH), state(BATCH), state(DEC_BATCH), state(DEC_BATCH),
        ],
        out_shape=[
            jax.ShapeDtypeStruct((N_PROMPT_ROWS, W_SSM), F32),
            jax.ShapeDtypeStruct((N_TAIL_ROWS, W_SSM), F32),
            jax.ShapeDtypeStruct((BATCH, N_GROUPS * SSM_STATE), F32),
            jax.ShapeDtypeStruct((BATCH, N_GROUPS * SSM_STATE), F32),
            jax.ShapeDtypeStruct((DEC_BATCH, N_GROUPS * SSM_STATE), F32),
            jax.ShapeDtypeStruct((DEC_BATCH, N_GROUPS * SSM_STATE), F32),
        ],
        scratch_shapes=[
            pltpu.VMEM((FOLD_ROWS, CHUNK * LANES), F32),
            pltpu.VMEM((CHUNK * LANES, 2 * ns), BF16),
            pltpu.VMEM((CHUNK * LANES, CHUNK * LANES // 2), BF16),
            pltpu.VMEM((CHUNK * LANES, 2 * ns), BF16),
            pltpu.VMEM((ns // LANES, FOLD_ROWS, LANES), F32),
            pltpu.VMEM((ns // LANES, FOLD_ROWS, LANES), F32),
            pltpu.VMEM((FOLD_ROWS, CHUNK * LANES), F32),
            pltpu.VMEM((N_SAMPLE_ROWS, 2 * ns), BF16),
        ],
        compiler_params=_params(1),
        name="s5",
    )(u_p, u_t, u_t, h0_re, h0_im, b_t, ct_t, pwl, d_skip)


def _out_kernel(x_ref, a_ref, sy_ref, g_ref, wga_ref, wgb_ref, bga_ref, bgb_ref, gluw_ref, glub_ref,
                wa_ref, wb_ref, wo_ref, o_ref, xn_ref, s_ref):
    j = pl.program_id(1)

    @pl.when(j == 0)
    def _():
        x = x_ref[...]
        xn_ref[...] = _rms(x, g_ref[...]).astype(BF16)
        o_ref[...] = x
        s = jax.nn.gelu(sy_ref[...])
        z = jnp.dot(s.astype(BF16), gluw_ref[...], preferred_element_type=F32) + glub_ref[...]
        s_ref[...] = (s * jax.nn.sigmoid(z)).astype(BF16)

    xn = xn_ref[...]
    ga = jax.nn.sigmoid(jnp.dot(xn, wga_ref[...], preferred_element_type=F32) + bga_ref[...])
    gb = jax.nn.sigmoid(jnp.dot(xn, wgb_ref[...], preferred_element_type=F32) + bgb_ref[...])
    merged = (ga * jnp.dot(a_ref[...], wa_ref[...], preferred_element_type=F32)
              + gb * jnp.dot(s_ref[...], wb_ref[...], preferred_element_type=F32))
    o_ref[...] += jnp.dot(merged.astype(BF16), wo_ref[...], preferred_element_type=F32)


def _out_proj(x, a, sy, g, w_in, b_gate, glu_w, glu_b, w_a, w_b, w_out, tm):
    rows = x.shape[0]
    nb = D_MODEL // TN_OUT
    g0 = (W_POOL + W_SSM) // TN_OUT
    return pl.pallas_call(
        _out_kernel,
        grid=(rows // tm, nb),
        in_specs=[
            pl.BlockSpec((tm, D_MODEL), lambda i, j: (i, 0)),
            pl.BlockSpec((tm, W_POOL), lambda i, j: (i, 0)),
            pl.BlockSpec((tm, W_SSM), lambda i, j: (i, 0)),
            pl.BlockSpec((1, D_MODEL), lambda i, j: (0, 0)),
            pl.BlockSpec((D_MODEL, TN_OUT), lambda i, j: (0, g0 + j)),
            pl.BlockSpec((D_MODEL, TN_OUT), lambda i, j: (0, g0 + nb + j)),
            pl.BlockSpec((1, TN_OUT), lambda i, j: (0, j)),
            pl.BlockSpec((1, TN_OUT), lambda i, j: (0, nb + j)),
            pl.BlockSpec((W_SSM, W_SSM), lambda i, j: (0, 0)),
            pl.BlockSpec((1, W_SSM), lambda i, j: (0, 0)),
            pl.BlockSpec((W_POOL, TN_OUT), lambda i, j: (0, j)),
            pl.BlockSpec((W_SSM, TN_OUT), lambda i, j: (0, j)),
            pl.BlockSpec((TN_OUT, D_MODEL), lambda i, j: (j, 0)),
        ],
        out_specs=pl.BlockSpec((tm, D_MODEL), lambda i, j: (i, 0)),
        out_shape=jax.ShapeDtypeStruct((rows, D_MODEL), F32),
        scratch_shapes=[pltpu.VMEM((tm, D_MODEL), BF16), pltpu.VMEM((tm, W_SSM), BF16)],
        compiler_params=_params(2),
        name="out_proj",
    )(x, a, sy, g, w_in, w_in, b_gate, b_gate, glu_w, glu_b, w_a, w_b, w_out)


def kernel(x_prompt, x_sample, state_pool, state_ssm_re, state_ssm_im, meta_tokens, norm_ffn1, ffn1_w_gate, ffn1_w_up, ffn1_w_down, norm_mix, w_in, b_gate, pool_w, pool_scale, ssm_lambda_re, ssm_lambda_im, ssm_log_dt, ssm_b_re, ssm_b_im, ssm_c_re, ssm_c_im, ssm_d, glu_w, glu_b, w_branch_a, w_branch_b, w_out, norm_ffn2, ffn2_w_gate, ffn2_w_up, ffn2_w_down, final_norm):
    l = 0
    bf = lambda w: w.astype(BF16)
    row = lambda v: v.reshape(1, -1).astype(F32)
    tiles = (((TM_FFN, TF), TM), ((N_TAIL_ROWS, TF_TAIL), N_TAIL_ROWS))

    xs = (x_prompt.reshape(N_PROMPT_ROWS, D_MODEL),
          jnp.concatenate([jnp.transpose(x_sample, (1, 0, 2)).reshape(N_SAMPLE_ROWS, D_MODEL),
                           meta_tokens.astype(F32)], axis=0))

    fg = row(final_norm)
    w_in_b = bf(w_in[l])
    x1 = [_ffn(x, row(norm_ffn1[l]), ffn1_w_gate[l], ffn1_w_up[l], ffn1_w_down[l], fg, *tm_ffn, False)
          for x, (tm_ffn, _) in zip(xs, tiles)]

    pool_wb, pool_sc = bf(pool_w[l]), row(pool_scale[l])
    hist_t = jnp.transpose(state_pool[l], (1, 0, 2))
    u_t = _inproj(x1[1], row(norm_mix[l]), w_in_b, N_TAIL_ROWS)
    a_t = _pool_tail(u_t, hist_t, pool_wb, pool_sc)
    u_p, a_p = _inproj_pool_prompt(x1[0], row(norm_mix[l]), w_in_b, u_t, pool_wb, pool_sc)
    a_out = (a_p, a_t)

    b_t, ct_t, pwl = _s5_tables(ssm_lambda_re[l], ssm_lambda_im[l], ssm_log_dt[l], ssm_b_re[l],
                                ssm_b_im[l], ssm_c_re[l], ssm_c_im[l])
    sy_p, sy_t, hp_re, hp_im, hs_re, hs_im = _s5(
        u_p, u_t, state_ssm_re[l].reshape(DEC_BATCH, -1), state_ssm_im[l].reshape(DEC_BATCH, -1),
        b_t, ct_t, pwl, row(ssm_d[l]))

    mix_w = (bf(glu_w[l]), row(glu_b[l]), bf(w_branch_a[l]), bf(w_branch_b[l]), bf(w_out[l]))
    y = []
    for x, a, sy, (tm_ffn, tm) in zip(x1, a_out, (sy_p, sy_t), tiles):
        x2 = _out_proj(x, a, sy, row(norm_mix[l]), w_in_b, row(b_gate[l]), *mix_w, tm)
        y.append(_ffn(x2, row(norm_ffn2[l]), ffn2_w_gate[l], ffn2_w_up[l], ffn2_w_down[l], fg, *tm_ffn, True))
    y_p, y_t = y

    y_prompt = y_p.reshape(BATCH, SEQ, D_MODEL)
    y_sample = jnp.transpose(y_t[:N_SAMPLE_ROWS].reshape(DEC_SEQ, DEC_BATCH, D_MODEL), (1, 0, 2))
    pool_p = jnp.stack([u_p[(b + 1) * SEQ - POOL_HIST:(b + 1) * SEQ, :W_POOL] for b in range(BATCH)])[None]
    u_pool_s = jnp.transpose(u_t[:N_SAMPLE_ROWS, :W_POOL].reshape(DEC_SEQ, DEC_BATCH, W_POOL), (1, 0, 2))
    pool_s = jnp.concatenate([state_pool[l][:, DEC_SEQ:], u_pool_s], axis=1)[None]
    shp_p = (1, BATCH, N_GROUPS, SSM_STATE)
    shp_s = (1, DEC_BATCH, N_GROUPS, SSM_STATE)
    return (y_prompt, y_sample, pool_p, pool_s,
            hp_re.reshape(shp_p), hp_im.reshape(shp_p), hs_re.reshape(shp_s), hs_im.reshape(shp_s))
```

```python
import functools

import jax
import jax.numpy as jnp
from jax import lax
from jax.experimental import pallas as pl
from jax.experimental.pallas import tpu as pltpu

F32 = jnp.float32
BF16 = jnp.bfloat16

D_MODEL = 2048
BATCH = 4
SEQ = 2048
DEC_BATCH = 128
DEC_SEQ = 4
N_META = 16
D_FF = 5632
W_POOL = 1024
W_SSM = 1024
POOL_WINDOWS = (2, 4, 8, 16)
POOL_GW = 256
POOL_HIST = 15
SSM_GS = 16
N_GROUPS = 64
SSM_STATE = 64
RMS_EPS = 1e-6

N_PROMPT_ROWS = BATCH * SEQ
N_SAMPLE_ROWS = DEC_BATCH * DEC_SEQ
N_TAIL_ROWS = N_SAMPLE_ROWS + N_META
TM_FFN = 1024
TM = 512
TF = 256
TF_TAIL = 512
TN_OUT = 512
DOWN_CHUNK = 512

LANES = 128
N_SLABS = W_SSM // LANES
SLAB_STATE = (LANES // SSM_GS) * SSM_STATE
CHUNK = 8
N_META_CHUNKS = N_META // CHUNK
N_CHUNKS = (SEQ + N_META) // CHUNK
FOLD_ROW0 = 8
SEQ_STRIDE = FOLD_ROW0 + SEQ // CHUNK
FOLD_ROWS = BATCH * SEQ_STRIDE

VMEM_LIMIT = 60 * 1024 * 1024


def _rms(x, g):
    r = lax.rsqrt(jnp.mean(x * x, axis=-1, keepdims=True) + RMS_EPS)
    return x * r * g


def _params(n_axes):
    return pltpu.CompilerParams(dimension_semantics=("arbitrary",) * n_axes, vmem_limit_bytes=VMEM_LIMIT)


def _ffn_kernel(x_ref, g_ref, wg_ref, wu_ref, wd_ref, fg_ref, o_ref, xn_ref, *, final_norm):
    j = pl.program_id(1)

    @pl.when(j == 0)
    def _():
        x = x_ref[...]
        xn_ref[...] = _rms(x, g_ref[...]).astype(BF16)
        o_ref[...] = x

    xn = xn_ref[...]
    gate = jnp.dot(xn, wg_ref[...].astype(BF16), preferred_element_type=F32)
    up = jnp.dot(xn, wu_ref[...].astype(BF16), preferred_element_type=F32)
    h = (gate * jax.nn.sigmoid(gate) * up * 0.5).astype(BF16)
    for n in range(D_MODEL // DOWN_CHUNK):
        cols = slice(n * DOWN_CHUNK, (n + 1) * DOWN_CHUNK)
        o_ref[:, cols] += jnp.dot(h, wd_ref[:, cols].astype(BF16), preferred_element_type=F32)

    if final_norm:
        @pl.when(j == pl.num_programs(1) - 1)
        def _():
            o_ref[...] = _rms(o_ref[...], fg_ref[...])


def _ffn(x, g, wg, wu, wd, fg, tm, tf, final_norm):
    rows = x.shape[0]
    return pl.pallas_call(
        functools.partial(_ffn_kernel, final_norm=final_norm),
        grid=(rows // tm, D_FF // tf),
        in_specs=[
            pl.BlockSpec((tm, D_MODEL), lambda i, j: (i, 0)),
            pl.BlockSpec((1, D_MODEL), lambda i, j: (0, 0)),
            pl.BlockSpec((D_MODEL, tf), lambda i, j: (0, j)),
            pl.BlockSpec((D_MODEL, tf), lambda i, j: (0, j)),
            pl.BlockSpec((tf, D_MODEL), lambda i, j: (j, 0)),
            pl.BlockSpec((1, D_MODEL), lambda i, j: (0, 0)),
        ],
        out_specs=pl.BlockSpec((tm, D_MODEL), lambda i, j: (i, 0)),
        out_shape=jax.ShapeDtypeStruct((rows, D_MODEL), F32),
        scratch_shapes=[pltpu.VMEM((tm, D_MODEL), BF16)],
        compiler_params=_params(2),
        name="ffn_final" if final_norm else "ffn",
    )(x, g, wg, wu, wd, fg)


def _inproj_kernel(x_ref, g_ref, w_ref, o_ref):
    xn = _rms(x_ref[...], g_ref[...]).astype(BF16)
    o_ref[...] = jnp.dot(xn, w_ref[...], preferred_element_type=F32)


def _inproj(x, g, w_in, tm):
    rows = x.shape[0]
    n = W_POOL + W_SSM
    return pl.pallas_call(
        _inproj_kernel,
        grid=(rows // tm,),
        in_specs=[
            pl.BlockSpec((tm, D_MODEL), lambda i: (i, 0)),
            pl.BlockSpec((1, D_MODEL), lambda i: (0, 0)),
            pl.BlockSpec((D_MODEL, n), lambda i: (0, 0)),
        ],
        out_specs=pl.BlockSpec((tm, n), lambda i: (i, 0)),
        out_shape=jax.ShapeDtypeStruct((rows, n), F32),
        compiler_params=_params(1),
        name="inproj",
    )(x, g, w_in)


TILES_PER_SEQ = SEQ // TM
HALO = 16


def _pool_project(mean_ref, u_ref, pw_ref, scale_ref, o_ref):
    for g in range(len(POOL_WINDOWS)):
        cols = slice(g * POOL_GW, (g + 1) * POOL_GW)
        d = (mean_ref[:, cols] - u_ref[:, cols]).astype(BF16)
        y = jnp.dot(d, pw_ref[g], preferred_element_type=F32) * scale_ref[:, cols]
        o_ref[:, cols] = y.astype(BF16)


def _inproj_pool_kernel(x_ref, g_ref, w_ref, meta_ref, pw_ref, scale_ref, u_ref, o_ref, full_ref):
    @pl.when(pl.program_id(0) % TILES_PER_SEQ == 0)
    def _():
        full_ref[0:HALO, :] = meta_ref[...]

    xn = _rms(x_ref[...], g_ref[...]).astype(BF16)
    u_pool = jnp.dot(xn, w_ref[:, :W_POOL], preferred_element_type=F32)
    u_ref[:, :W_POOL] = u_pool
    full_ref[HALO:HALO + TM, :] = u_pool
    for g, w in enumerate(POOL_WINDOWS):
        cols = slice(g * POOL_GW, (g + 1) * POOL_GW)
        ucols = slice(W_POOL + g * POOL_GW, W_POOL + (g + 1) * POOL_GW)
        u_ref[:, ucols] = jnp.dot(xn, w_ref[:, ucols], preferred_element_type=F32)
        acc = full_ref[HALO:HALO + TM, cols]
        for k in range(1, w):
            acc = acc + full_ref[HALO - k:HALO - k + TM, cols]
        d = (acc * (1.0 / w) - full_ref[HALO:HALO + TM, cols]).astype(BF16)
        y = jnp.dot(d, pw_ref[g], preferred_element_type=F32) * scale_ref[:, cols]
        o_ref[:, cols] = y.astype(BF16)
    full_ref[0:HALO, :] = full_ref[TM:TM + HALO, :]


def _pool_tail_kernel(u_ref, hist_ref, pw_ref, scale_ref, o_ref, mean_ref):
    for g, w in enumerate(POOL_WINDOWS):
        cols = slice(g * POOL_GW, (g + 1) * POOL_GW)
        for t in range(DEC_SEQ):
            acc = None
            for k in range(w):
                p = POOL_HIST + t - k
                if p >= POOL_HIST:
                    q = p - POOL_HIST
                    term = u_ref[q * DEC_BATCH:(q + 1) * DEC_BATCH, cols]
                else:
                    term = hist_ref[p, :, cols]
                acc = term if acc is None else acc + term
            mean_ref[t * DEC_BATCH:(t + 1) * DEC_BATCH, cols] = acc * (1.0 / w)
    mean_ref[N_SAMPLE_ROWS:, :] = u_ref[N_SAMPLE_ROWS:, :]
    _pool_project(mean_ref, u_ref, pw_ref, scale_ref, o_ref)


_PW_SPEC = pl.BlockSpec((len(POOL_WINDOWS), POOL_GW, POOL_GW), lambda i: (0, 0, 0))
_SCALE_SPEC = pl.BlockSpec((1, W_POOL), lambda i: (0, 0))


def _inproj_pool_prompt(x, g, w_in, u_t, pool_w, pool_scale):
    n = W_POOL + W_SSM
    return pl.pallas_call(
        _inproj_pool_kernel,
        grid=(N_PROMPT_ROWS // TM,),
        in_specs=[
            pl.BlockSpec((TM, D_MODEL), lambda i: (i, 0)),
            pl.BlockSpec((1, D_MODEL), lambda i: (0, 0)),
            pl.BlockSpec((D_MODEL, n), lambda i: (0, 0)),
            pl.BlockSpec((N_META, W_POOL), lambda i: (N_SAMPLE_ROWS // N_META, 0)),
            _PW_SPEC, _SCALE_SPEC,
        ],
        out_specs=[pl.BlockSpec((TM, n), lambda i: (i, 0)), pl.BlockSpec((TM, W_POOL), lambda i: (i, 0))],
        out_shape=[jax.ShapeDtypeStruct((N_PROMPT_ROWS, n), F32),
                   jax.ShapeDtypeStruct((N_PROMPT_ROWS, W_POOL), BF16)],
        scratch_shapes=[pltpu.VMEM((HALO + TM, W_POOL), F32)],
        compiler_params=_params(1),
        name="inproj_pool",
    )(x, g, w_in, u_t, pool_w, pool_scale)


def _pool_tail(u_t, hist_t, pool_w, pool_scale):
    return pl.pallas_call(
        _pool_tail_kernel,
        grid=(1,),
        in_specs=[
            pl.BlockSpec((N_TAIL_ROWS, W_POOL), lambda i: (0, 0)),
            pl.BlockSpec((POOL_HIST, DEC_BATCH, W_POOL), lambda i: (0, 0, 0)),
            _PW_SPEC, _SCALE_SPEC,
        ],
        out_specs=pl.BlockSpec((N_TAIL_ROWS, W_POOL), lambda i: (0, 0)),
        out_shape=jax.ShapeDtypeStruct((N_TAIL_ROWS, W_POOL), BF16),
        scratch_shapes=[pltpu.VMEM((N_TAIL_ROWS, W_POOL), F32)],
        compiler_params=_params(1),
        name="pool_tail",
    )(u_t, hist_t, pool_w, pool_scale)


def _s5_tables(lam_re, lam_im, log_dt, b_re, b_im, c_re, c_im):
    dt = jnp.exp(log_dt)[:, None]
    k = jnp.arange(CHUNK + 1, dtype=F32)[:, None, None]
    mag = jnp.exp(k * (lam_re * dt)[None])
    ang = k * (lam_im * dt)[None]
    pw_re, pw_im = mag * jnp.cos(ang), mag * jnp.sin(ang)
    lb_re, lb_im = pw_re[1], pw_im[1]
    den = lam_re * lam_re + lam_im * lam_im
    q_re = ((lb_re - 1.0) * lam_re + lb_im * lam_im) / den
    q_im = (lb_im * lam_re - (lb_re - 1.0) * lam_im) / den
    bt_re, bt_im = jnp.transpose(b_re, (0, 2, 1)), jnp.transpose(b_im, (0, 2, 1))
    bb_re = q_re[:, None, :] * bt_re - q_im[:, None, :] * bt_im
    bb_im = q_re[:, None, :] * bt_im + q_im[:, None, :] * bt_re
    slab = lambda t: t.reshape(N_SLABS, LANES, SSM_STATE)
    b_c = jnp.concatenate([slab(bb_re), slab(bb_im)], axis=2)
    c_c = jnp.concatenate([slab(c_re), slab(c_im)], axis=2)
    pr = pw_re.reshape(CHUNK + 1, N_SLABS, SLAB_STATE)
    pi = pw_im.reshape(CHUNK + 1, N_SLABS, SLAB_STATE)
    pwl = jnp.concatenate([jnp.transpose(pr, (1, 0, 2)), jnp.transpose(pi, (1, 0, 2))], axis=2)
    return b_c, c_c, pwl


_NT = (((1,), (1,)), ((), ()))


def _block_diag(x):
    tiled = jnp.concatenate([x] * (LANES // SSM_GS), axis=1)
    shift = lambda n: n.bit_length() - 1
    row_group = lax.shift_right_logical(lax.broadcasted_iota(jnp.int32, tiled.shape, 0), shift(SSM_GS))
    lane_group = lax.shift_right_logical(lax.broadcasted_iota(jnp.int32, tiled.shape, 1), shift(SSM_STATE))
    return jnp.where(row_group == lane_group, tiled, 0.0)


def _s5_kernel(up_ref, um_ref, us_ref, h0re_ref, h0im_ref, b_ref, ct_ref, pwl_ref, d_ref,
               yp_ref, yt_ref, hpre_ref, hpim_ref, hsre_ref, hsim_ref,
               l_ref, f_ref, m_ref, et_ref, vre_ref, vim_ref, yf_ref, hs_ref):
    ns = SLAB_STATE
    b_re = _block_diag(b_ref[0][:, :SSM_STATE])
    b_im = _block_diag(b_ref[0][:, SSM_STATE:])
    ct_re = _block_diag(ct_ref[0][:, :SSM_STATE])
    ct_im = _block_diag(ct_ref[0][:, SSM_STATE:])
    ct_neg = jnp.concatenate([ct_re, -ct_im], axis=1).astype(BF16)

    l_ref[...] = jnp.zeros_like(l_ref)
    for b in range(BATCH):
        r0 = b * SEQ_STRIDE + FOLD_ROW0
        for s in range(CHUNK):
            cols = slice(s * LANES, (s + 1) * LANES)
            l_ref[r0:r0 + SEQ // CHUNK, cols] = up_ref[pl.ds(b * SEQ + s, SEQ // CHUNK, stride=CHUNK), :]
            for m in range(N_META_CHUNKS):
                r = r0 - N_META_CHUNKS + m
                l_ref[r:r + 1, cols] = um_ref[m * CHUNK + s:m * CHUNK + s + 1, :]
    lb = l_ref[...].astype(BF16)

    for s in range(CHUNK):
        p = pwl_ref[0, CHUNK - 1 - s:CHUNK - s, :]
        p_re, p_im = p[:, :ns], p[:, ns:]
        rows = slice(s * LANES, (s + 1) * LANES)
        f_ref[rows, :ns] = (b_re * p_re - b_im * p_im).astype(BF16)
        f_ref[rows, ns:] = (b_re * p_im + b_im * p_re).astype(BF16)
    v = jnp.dot(lb, f_ref[...], preferred_element_type=F32)
    nk = ns // LANES
    for k in range(nk):
        vre_ref[k] = v[:, k * LANES:(k + 1) * LANES]
        vim_ref[k] = v[:, ns + k * LANES:ns + (k + 1) * LANES]

    a = pwl_ref[0, CHUNK:CHUNK + 1, :]
    a_re = [jnp.broadcast_to(a[:, k * LANES:(k + 1) * LANES], (BATCH, LANES)) for k in range(nk)]
    a_im = [jnp.broadcast_to(a[:, ns + k * LANES:ns + (k + 1) * LANES], (BATCH, LANES)) for k in range(nk)]
    h_re = [jnp.zeros((BATCH, LANES), F32) for _ in range(nk)]
    h_im = [jnp.zeros((BATCH, LANES), F32) for _ in range(nk)]
    for j in range(N_CHUNKS):
        rows = pl.ds(FOLD_ROW0 - N_META_CHUNKS + j, BATCH, stride=SEQ_STRIDE)
        for k in range(nk):
            v_re = vre_ref[k, rows, :]
            v_im = vim_ref[k, rows, :]
            vre_ref[k, rows, :] = h_re[k]
            vim_ref[k, rows, :] = h_im[k]
            h_re[k], h_im[k] = (a_re[k] * h_re[k] - a_im[k] * h_im[k] + v_re,
                                a_re[k] * h_im[k] + a_im[k] * h_re[k] + v_im)
    for k in range(nk):
        hpre_ref[:, k * LANES:(k + 1) * LANES] = h_re[k]
        hpim_ref[:, k * LANES:(k + 1) * LANES] = h_im[k]

    kb_all = lax.dot_general(f_ref[...], ct_neg, _NT, preferred_element_type=F32)
    ri = lax.broadcasted_iota(jnp.int32, (LANES, LANES), 0)
    ci = lax.broadcasted_iota(jnp.int32, (LANES, LANES), 1)
    d_diag = jnp.where(ri == ci, jnp.broadcast_to(d_ref[...], (LANES, LANES)), 0.0)
    kb = [None] * CHUNK
    for s in range(CHUNK):
        blk = kb_all[s * LANES:(s + 1) * LANES, :]
        if s == CHUNK - 1:
            blk = blk + d_diag
        kb[CHUNK - 1 - s] = blk.astype(BF16)

    half = CHUNK // 2
    for s in range(CHUNK):
        for t in range(half, CHUNK):
            blk = kb[t - s] if t >= s else jnp.zeros((LANES, LANES), BF16)
            m_ref[s * LANES:(s + 1) * LANES, (t - half) * LANES:(t - half + 1) * LANES] = blk
    y_left = jnp.dot(lb[:, :half * LANES], m_ref[half * LANES:, :], preferred_element_type=F32)
    y_right = jnp.dot(lb, m_ref[...], preferred_element_type=F32)

    for t in range(CHUNK):
        p = pwl_ref[0, t + 1:t + 2, :]
        p_re, p_im = p[:, :ns], p[:, ns:]
        rows = slice(t * LANES, (t + 1) * LANES)
        et_ref[rows, :ns] = (ct_re * p_re - ct_im * p_im).astype(BF16)
        et_ref[rows, ns:] = (-(ct_re * p_im + ct_im * p_re)).astype(BF16)
    h_in = jnp.concatenate([vre_ref[k] for k in range(nk)] + [vim_ref[k] for k in range(nk)],
                           axis=1).astype(BF16)
    ys = lax.dot_general(h_in, et_ref[...], _NT, preferred_element_type=F32)
    yf_ref[:, :half * LANES] = y_left + ys[:, :half * LANES]
    yf_ref[:, half * LANES:] = y_right + ys[:, half * LANES:]
    for b in range(BATCH):
        r0 = b * SEQ_STRIDE + FOLD_ROW0
        for t in range(CHUNK):
            yp_ref[pl.ds(b * SEQ + t, SEQ // CHUNK, stride=CHUNK), :] = (
                yf_ref[r0:r0 + SEQ // CHUNK, t * LANES:(t + 1) * LANES])

    u = us_ref[...]
    b_all = jnp.concatenate([b_re, b_im], axis=1).astype(BF16)
    bu = jnp.dot(u.astype(BF16), b_all, preferred_element_type=F32)
    l1 = pwl_ref[0, 1:2, :]
    l_re, l_im = l1[:, :ns], l1[:, ns:]
    g_re = h0re_ref[...]
    g_im = h0im_ref[...]
    for t in range(DEC_SEQ):
        rows = slice(t * DEC_BATCH, (t + 1) * DEC_BATCH)
        g_re, g_im = (l_re * g_re - l_im * g_im + bu[rows, :ns],
                      l_re * g_im + l_im * g_re + bu[rows, ns:])
        hs_ref[rows, :ns] = g_re.astype(BF16)
        hs_ref[rows, ns:] = g_im.astype(BF16)
    hsre_ref[...] = g_re
    hsim_ref[...] = g_im
    yt_ref[0:N_SAMPLE_ROWS, :] = (lax.dot_general(hs_ref[...], ct_neg, _NT, preferred_element_type=F32)
                                  + d_ref[...] * u)
    yt_ref[N_SAMPLE_ROWS:, :] = jnp.zeros((N_META, LANES), F32)


def _s5(u_p, u_t, h0_re, h0_im, b_t, ct_t, pwl, d_skip):
    c0 = W_POOL // LANES
    ns = SLAB_STATE
    state = lambda rows: pl.BlockSpec((rows, ns), lambda i: (0, i))
    return pl.pallas_call(
        _s5_kernel,
        grid=(N_SLABS,),
        in_specs=[
            pl.BlockSpec((N_PROMPT_ROWS, LANES), lambda i: (0, c0 + i)),
            pl.BlockSpec((N_META, LANES), lambda i: (N_SAMPLE_ROWS // N_META, c0 + i)),
            pl.BlockSpec((N_SAMPLE_ROWS, LANES), lambda i: (0, c0 + i)),
            state(DEC_BATCH), state(DEC_BATCH),
            pl.BlockSpec((1, LANES, 2 * SSM_STATE), lambda i: (i, 0, 0)),
            pl.BlockSpec((1, LANES, 2 * SSM_STATE), lambda i: (i, 0, 0)),
            pl.BlockSpec((1, CHUNK + 1, 2 * ns), lambda i: (i, 0, 0)),
            pl.BlockSpec((1, LANES), lambda i: (0, i)),
        ],
        out_specs=[
            pl.BlockSpec((N_PROMPT_ROWS, LANES), lambda i: (0, i)),
            pl.BlockSpec((N_TAIL_ROWS, LANES), lambda i: (0, i)),
            state(BATCH), state(BATCH), state(DEC_BATCH), state(DEC_BATCH),
        ],
        out_shape=[
            jax.ShapeDtypeStruct((N_PROMPT_ROWS, W_SSM), F32),
            jax.ShapeDtypeStruct((N_TAIL_ROWS, W_SSM), F32),
            jax.ShapeDtypeStruct((BATCH, N_GROUPS * SSM_STATE), F32),
            jax.ShapeDtypeStruct((BATCH, N_GROUPS * SSM_STATE), F32),
            jax.ShapeDtypeStruct((DEC_BATCH, N_GROUPS * SSM_STATE), F32),
            jax.ShapeDtypeStruct((DEC_BATCH, N_GROUPS * SSM_STATE), F32),
        ],
        scratch_shapes=[
            pltpu.VMEM((FOLD_ROWS, CHUNK * LANES), F32),
            pltpu.VMEM((CHUNK * LANES, 2 * ns), BF16),
            pltpu.VMEM((CHUNK * LANES, CHUNK * LANES // 2), BF16),
            pltpu.VMEM((CHUNK * LANES, 2 * ns), BF16),
            pltpu.VMEM((ns // LANES, FOLD_ROWS, LANES), F32),
            pltpu.VMEM((ns // LANES, FOLD_ROWS, LANES), F32),
            pltpu.VMEM((FOLD_ROWS, CHUNK * LANES), F32),
            pltpu.VMEM((N_SAMPLE_ROWS, 2 * ns), BF16),
        ],
        compiler_params=_params(1),
        name="s5",
    )(u_p, u_t, u_t, h0_re, h0_im, b_t, ct_t, pwl, d_skip)


def _out_kernel(x_ref, a_ref, sy_ref, g_ref, wga_ref, wgb_ref, bga_ref, bgb_ref, gluw_ref, glub_ref,
                wa_ref, wb_ref, wo_ref, o_ref, xn_ref, s_ref):
    j = pl.program_id(1)

    @pl.when(j == 0)
    def _():
        x = x_ref[...]
        xn_ref[...] = _rms(x, g_ref[...]).astype(BF16)
        o_ref[...] = x
        s = jax.nn.gelu(sy_ref[...])
        z = jnp.dot(s.astype(BF16), gluw_ref[...], preferred_element_type=F32) + glub_ref[...]
        s_ref[...] = (s * jax.nn.sigmoid(z)).astype(BF16)

    xn = xn_ref[...]
    ga = jax.nn.sigmoid(jnp.dot(xn, wga_ref[...], preferred_element_type=F32) + bga_ref[...])
    gb = jax.nn.sigmoid(jnp.dot(xn, wgb_ref[...], preferred_element_type=F32) + bgb_ref[...])
    merged = (ga * jnp.dot(a_ref[...], wa_ref[...], preferred_element_type=F32)
              + gb * jnp.dot(s_ref[...], wb_ref[...], preferred_element_type=F32))
    o_ref[...] += jnp.dot(merged.astype(BF16), wo_ref[...], preferred_element_type=F32)


def _out_proj(x, a, sy, g, w_in, b_gate, glu_w, glu_b, w_a, w_b, w_out, tm):
    rows = x.shape[0]
    nb = D_MODEL // TN_OUT
    g0 = (W_POOL + W_SSM) // TN_OUT
    return pl.pallas_call(
        _out_kernel,
        grid=(rows // tm, nb),
        in_specs=[
            pl.BlockSpec((tm, D_MODEL), lambda i, j: (i, 0)),
            pl.BlockSpec((tm, W_POOL), lambda i, j: (i, 0)),
            pl.BlockSpec((tm, W_SSM), lambda i, j: (i, 0)),
            pl.BlockSpec((1, D_MODEL), lambda i, j: (0, 0)),
            pl.BlockSpec((D_MODEL, TN_OUT), lambda i, j: (0, g0 + j)),
            pl.BlockSpec((D_MODEL, TN_OUT), lambda i, j: (0, g0 + nb + j)),
            pl.BlockSpec((1, TN_OUT), lambda i, j: (0, j)),
            pl.BlockSpec((1, TN_OUT), lambda i, j: (0, nb + j)),
            pl.BlockSpec((W_SSM, W_SSM), lambda i, j: (0, 0)),
            pl.BlockSpec((1, W_SSM), lambda i, j: (0, 0)),
            pl.BlockSpec((W_POOL, TN_OUT), lambda i, j: (0, j)),
            pl.BlockSpec((W_SSM, TN_OUT), lambda i, j: (0, j)),
            pl.BlockSpec((TN_OUT, D_MODEL), lambda i, j: (j, 0)),
        ],
        out_specs=pl.BlockSpec((tm, D_MODEL), lambda i, j: (i, 0)),
        out_shape=jax.ShapeDtypeStruct((rows, D_MODEL), F32),
        scratch_shapes=[pltpu.VMEM((tm, D_MODEL), BF16), pltpu.VMEM((tm, W_SSM), BF16)],
        compiler_params=_params(2),
        name="out_proj",
    )(x, a, sy, g, w_in, w_in, b_gate, b_gate, glu_w, glu_b, w_a, w_b, w_out)


def kernel(x_prompt, x_sample, state_pool, state_ssm_re, state_ssm_im, meta_tokens, norm_ffn1, ffn1_w_gate, ffn1_w_up, ffn1_w_down, norm_mix, w_in, b_gate, pool_w, pool_scale, ssm_lambda_re, ssm_lambda_im, ssm_log_dt, ssm_b_re, ssm_b_im, ssm_c_re, ssm_c_im, ssm_d, glu_w, glu_b, w_branch_a, w_branch_b, w_out, norm_ffn2, ffn2_w_gate, ffn2_w_up, ffn2_w_down, final_norm):
    l = 0
    bf = lambda w: w.astype(BF16)
    row = lambda v: v.reshape(1, -1).astype(F32)
    tiles = (((TM_FFN, TF), TM), ((N_TAIL_ROWS, TF_TAIL), N_TAIL_ROWS))

    xs = (x_prompt.reshape(N_PROMPT_ROWS, D_MODEL),
          jnp.concatenate([jnp.transpose(x_sample, (1, 0, 2)).reshape(N_SAMPLE_ROWS, D_MODEL),
                           meta_tokens.astype(F32)], axis=0))

    fg = row(final_norm)
    w_in_b = bf(w_in[l])
    x1 = [_ffn(x, row(norm_ffn1[l]), ffn1_w_gate[l], ffn1_w_up[l], ffn1_w_down[l], fg, *tm_ffn, False)
          for x, (tm_ffn, _) in zip(xs, tiles)]

    pool_wb, pool_sc = bf(pool_w[l]), row(pool_scale[l])
    hist_t = jnp.transpose(state_pool[l], (1, 0, 2))
    u_t = _inproj(x1[1], row(norm_mix[l]), w_in_b, N_TAIL_ROWS)
    a_t = _pool_tail(u_t, hist_t, pool_wb, pool_sc)
    u_p, a_p = _inproj_pool_prompt(x1[0], row(norm_mix[l]), w_in_b, u_t, pool_wb, pool_sc)
    a_out = (a_p, a_t)

    b_t, ct_t, pwl = _s5_tables(ssm_lambda_re[l], ssm_lambda_im[l], ssm_log_dt[l], ssm_b_re[l],
                                ssm_b_im[l], ssm_c_re[l], ssm_c_im[l])
    sy_p, sy_t, hp_re, hp_im, hs_re, hs_im = _s5(
        u_p, u_t, state_ssm_re[l].reshape(DEC_BATCH, -1), state_ssm_im[l].reshape(DEC_BATCH, -1),
        b_t, ct_t, pwl, row(ssm_d[l]))

    mix_w = (bf(glu_w[l]), row(glu_b[l]), bf(w_branch_a[l]), bf(w_branch_b[l]), bf(w_out[l]))
    y = []
    for x, a, sy, (tm_ffn, tm) in zip(x1, a_out, (sy_p, sy_t), tiles):
        x2 = _out_proj(x, a, sy, row(norm_mix[l]), w_in_b, row(b_gate[l]), *mix_w, tm)
        y.append(_ffn(x2, row(norm_ffn2[l]), ffn2_w_gate[l], ffn2_w_up[l], ffn2_w_down[l], fg, *tm_ffn, True))
    y_p, y_t = y

    y_prompt = y_p.reshape(BATCH, SEQ, D_MODEL)
    y_sample = jnp.transpose(y_t[:N_SAMPLE_ROWS].reshape(DEC_SEQ, DEC_BATCH, D_MODEL), (1, 0, 2))
    pool_p = jnp.stack([u_p[(b + 1) * SEQ - POOL_HIST:(b + 1) * SEQ, :W_POOL] for b in range(BATCH)])[None]
    u_pool_s = jnp.transpose(u_t[:N_SAMPLE_ROWS, :W_POOL].reshape(DEC_SEQ, DEC_BATCH, W_POOL), (1, 0, 2))
    pool_s = jnp.concatenate([state_pool[l][:, DEC_SEQ:], u_pool_s], axis=1)[None]
    shp_p = (1, BATCH, N_GROUPS, SSM_STATE)
    shp_s = (1, DEC_BATCH, N_GROUPS, SSM_STATE)
    return (y_prompt, y_sample, pool_p, pool_s,
            hp_re.reshape(shp_p), hp_im.reshape(shp_p), hs_re.reshape(shp_s), hs_im.reshape(shp_s))
```

```python
import functools

import jax
import jax.numpy as jnp
from jax import lax
from jax.experimental import pallas as pl
from jax.experimental.pallas import tpu as pltpu

F32 = jnp.float32
BF16 = jnp.bfloat16

D_MODEL = 2048
BATCH = 4
SEQ = 2048
DEC_BATCH = 128
DEC_SEQ = 4
N_META = 16
D_FF = 5632
W_POOL = 1024
W_SSM = 1024
POOL_WINDOWS = (2, 4, 8, 16)
POOL_GW = 256
POOL_HIST = 15
SSM_GS = 16
N_GROUPS = 64
SSM_STATE = 64
RMS_EPS = 1e-6

N_PROMPT_ROWS = BATCH * SEQ
N_SAMPLE_ROWS = DEC_BATCH * DEC_SEQ
N_TAIL_ROWS = N_SAMPLE_ROWS + N_META
TM_FFN = 1024
TM = 512
TF = 256
TF_TAIL = 512
TN_OUT = 512
DOWN_CHUNK = 512

LANES = 128
N_SLABS = W_SSM // LANES
SLAB_STATE = (LANES // SSM_GS) * SSM_STATE
CHUNK = 8
N_META_CHUNKS = N_META // CHUNK
N_CHUNKS = (SEQ + N_META) // CHUNK
FOLD_ROW0 = 8
SEQ_STRIDE = FOLD_ROW0 + SEQ // CHUNK
FOLD_ROWS = BATCH * SEQ_STRIDE

V7X_VMEM_BYTES = 64 * 1024 * 1024
VMEM_LIMIT = V7X_VMEM_BYTES - 4 * 1024 * 1024


def _rms(x, g):
    r = lax.rsqrt(jnp.mean(x * x, axis=-1, keepdims=True) + RMS_EPS)
    return x * r * g


def _params(n_axes):
    return pltpu.CompilerParams(dimension_semantics=("arbitrary",) * n_axes, vmem_limit_bytes=VMEM_LIMIT)


def _ffn_kernel(x_ref, g_ref, wg_ref, wu_ref, wd_ref, fg_ref, o_ref, xn_ref, *, final_norm):
    j = pl.program_id(1)

    @pl.when(j == 0)
    def _():
        x = x_ref[...]
        xn_ref[...] = _rms(x, g_ref[...]).astype(BF16)
        o_ref[...] = x

    xn = xn_ref[...]
    gate = jnp.dot(xn, wg_ref[...].astype(BF16), preferred_element_type=F32)
    up = jnp.dot(xn, wu_ref[...].astype(BF16), preferred_element_type=F32)
    h = (gate * jax.nn.sigmoid(gate) * up * 0.5).astype(BF16)
    for n in range(D_MODEL // DOWN_CHUNK):
        cols = slice(n * DOWN_CHUNK, (n + 1) * DOWN_CHUNK)
        o_ref[:, cols] += jnp.dot(h, wd_ref[:, cols].astype(BF16), preferred_element_type=F32)

    if final_norm:
        @pl.when(j == pl.num_programs(1) - 1)
        def _():
            o_ref[...] = _rms(o_ref[...], fg_ref[...])


def _ffn(x, g, wg, wu, wd, fg, tm, tf, final_norm):
    rows = x.shape[0]
    return pl.pallas_call(
        functools.partial(_ffn_kernel, final_norm=final_norm),
        grid=(rows // tm, D_FF // tf),
        in_specs=[
            pl.BlockSpec((tm, D_MODEL), lambda i, j: (i, 0)),
            pl.BlockSpec((1, D_MODEL), lambda i, j: (0, 0)),
            pl.BlockSpec((D_MODEL, tf), lambda i, j: (0, j)),
            pl.BlockSpec((D_MODEL, tf), lambda i, j: (0, j)),
            pl.BlockSpec((tf, D_MODEL), lambda i, j: (j, 0)),
            pl.BlockSpec((1, D_MODEL), lambda i, j: (0, 0)),
        ],
        out_specs=pl.BlockSpec((tm, D_MODEL), lambda i, j: (i, 0)),
        out_shape=jax.ShapeDtypeStruct((rows, D_MODEL), F32),
        scratch_shapes=[pltpu.VMEM((tm, D_MODEL), BF16)],
        compiler_params=_params(2),
        name="ffn_final" if final_norm else "ffn",
    )(x, g, wg, wu, wd, fg)


def _inproj_kernel(x_ref, g_ref, w_ref, o_ref):
    xn = _rms(x_ref[...], g_ref[...]).astype(BF16)
    o_ref[...] = jnp.dot(xn, w_ref[...], preferred_element_type=F32)


def _inproj(x, g, w_in, tm):
    rows = x.shape[0]
    n = W_POOL + W_SSM
    return pl.pallas_call(
        _inproj_kernel,
        grid=(rows // tm,),
        in_specs=[
            pl.BlockSpec((tm, D_MODEL), lambda i: (i, 0)),
            pl.BlockSpec((1, D_MODEL), lambda i: (0, 0)),
            pl.BlockSpec((D_MODEL, n), lambda i: (0, 0)),
        ],
        out_specs=pl.BlockSpec((tm, n), lambda i: (i, 0)),
        out_shape=jax.ShapeDtypeStruct((rows, n), F32),
        compiler_params=_params(1),
        name="inproj",
    )(x, g, w_in)


TILES_PER_SEQ = SEQ // TM
HALO = 16


def _pool_project(mean_ref, u_ref, pw_ref, scale_ref, o_ref):
    for g in range(len(POOL_WINDOWS)):
        cols = slice(g * POOL_GW, (g + 1) * POOL_GW)
        d = (mean_ref[:, cols] - u_ref[:, cols]).astype(BF16)
        y = jnp.dot(d, pw_ref[g], preferred_element_type=F32) * scale_ref[:, cols]
        o_ref[:, cols] = y.astype(BF16)


def _inproj_pool_kernel(x_ref, g_ref, w_ref, meta_ref, pw_ref, scale_ref, u_ref, o_ref, full_ref):
    @pl.when(pl.program_id(0) % TILES_PER_SEQ == 0)
    def _():
        full_ref[0:HALO, :] = meta_ref[...]

    xn = _rms(x_ref[...], g_ref[...]).astype(BF16)
    u_pool = jnp.dot(xn, w_ref[:, :W_POOL], preferred_element_type=F32)
    u_ref[:, :W_POOL] = u_pool
    full_ref[HALO:HALO + TM, :] = u_pool
    for g, w in enumerate(POOL_WINDOWS):
        cols = slice(g * POOL_GW, (g + 1) * POOL_GW)
        ucols = slice(W_POOL + g * POOL_GW, W_POOL + (g + 1) * POOL_GW)
        u_ref[:, ucols] = jnp.dot(xn, w_ref[:, ucols], preferred_element_type=F32)
        acc = full_ref[HALO:HALO + TM, cols]
        for k in range(1, w):
            acc = acc + full_ref[HALO - k:HALO - k + TM, cols]
        d = (acc * (1.0 / w) - full_ref[HALO:HALO + TM, cols]).astype(BF16)
        y = jnp.dot(d, pw_ref[g], preferred_element_type=F32) * scale_ref[:, cols]
        o_ref[:, cols] = y.astype(BF16)
    full_ref[0:HALO, :] = full_ref[TM:TM + HALO, :]


def _pool_tail_kernel(u_ref, hist_ref, pw_ref, scale_ref, o_ref, mean_ref):
    for g, w in enumerate(POOL_WINDOWS):
        cols = slice(g * POOL_GW, (g + 1) * POOL_GW)
        for t in range(DEC_SEQ):
            acc = None
            for k in range(w):
                p = POOL_HIST + t - k
                if p >= POOL_HIST:
                    q = p - POOL_HIST
                    term = u_ref[q * DEC_BATCH:(q + 1) * DEC_BATCH, cols]
                else:
                    term = hist_ref[p, :, cols]
                acc = term if acc is None else acc + term
            mean_ref[t * DEC_BATCH:(t + 1) * DEC_BATCH, cols] = acc * (1.0 / w)
    mean_ref[N_SAMPLE_ROWS:, :] = u_ref[N_SAMPLE_ROWS:, :]
    _pool_project(mean_ref, u_ref, pw_ref, scale_ref, o_ref)


_PW_SPEC = pl.BlockSpec((len(POOL_WINDOWS), POOL_GW, POOL_GW), lambda i: (0, 0, 0))
_SCALE_SPEC = pl.BlockSpec((1, W_POOL), lambda i: (0, 0))


def _inproj_pool_prompt(x, g, w_in, u_t, pool_w, pool_scale):
    n = W_POOL + W_SSM
    return pl.pallas_call(
        _inproj_pool_kernel,
        grid=(N_PROMPT_ROWS // TM,),
        in_specs=[
            pl.BlockSpec((TM, D_MODEL), lambda i: (i, 0)),
            pl.BlockSpec((1, D_MODEL), lambda i: (0, 0)),
            pl.BlockSpec((D_MODEL, n), lambda i: (0, 0)),
            pl.BlockSpec((N_META, W_POOL), lambda i: (N_SAMPLE_ROWS // N_META, 0)),
            _PW_SPEC, _SCALE_SPEC,
        ],
        out_specs=[pl.BlockSpec((TM, n), lambda i: (i, 0)), pl.BlockSpec((TM, W_POOL), lambda i: (i, 0))],
        out_shape=[jax.ShapeDtypeStruct((N_PROMPT_ROWS, n), F32),
                   jax.ShapeDtypeStruct((N_PROMPT_ROWS, W_POOL), BF16)],
        scratch_shapes=[pltpu.VMEM((HALO + TM, W_POOL), F32)],
        compiler_params=_params(1),
        name="inproj_pool",
    )(x, g, w_in, u_t, pool_w, pool_scale)


def _pool_tail(u_t, hist_t, pool_w, pool_scale):
    return pl.pallas_call(
        _pool_tail_kernel,
        grid=(1,),
        in_specs=[
            pl.BlockSpec((N_TAIL_ROWS, W_POOL), lambda i: (0, 0)),
            pl.BlockSpec((POOL_HIST, DEC_BATCH, W_POOL), lambda i: (0, 0, 0)),
            _PW_SPEC, _SCALE_SPEC,
        ],
        out_specs=pl.BlockSpec((N_TAIL_ROWS, W_POOL), lambda i: (0, 0)),
        out_shape=jax.ShapeDtypeStruct((N_TAIL_ROWS, W_POOL), BF16),
        scratch_shapes=[pltpu.VMEM((N_TAIL_ROWS, W_POOL), F32)],
        compiler_params=_params(1),
        name="pool_tail",
    )(u_t, hist_t, pool_w, pool_scale)


def _s5_tables(lam_re, lam_im, log_dt, b_re, b_im, c_re, c_im):
    dt = jnp.exp(log_dt)[:, None]
    k = jnp.arange(CHUNK + 1, dtype=F32)[:, None, None]
    mag = jnp.exp(k * (lam_re * dt)[None])
    ang = k * (lam_im * dt)[None]
    pw_re, pw_im = mag * jnp.cos(ang), mag * jnp.sin(ang)
    lb_re, lb_im = pw_re[1], pw_im[1]
    den = lam_re * lam_re + lam_im * lam_im
    q_re = ((lb_re - 1.0) * lam_re + lb_im * lam_im) / den
    q_im = (lb_im * lam_re - (lb_re - 1.0) * lam_im) / den
    bt_re, bt_im = jnp.transpose(b_re, (0, 2, 1)), jnp.transpose(b_im, (0, 2, 1))
    bb_re = q_re[:, None, :] * bt_re - q_im[:, None, :] * bt_im
    bb_im = q_re[:, None, :] * bt_im + q_im[:, None, :] * bt_re
    slab = lambda t: t.reshape(N_SLABS, LANES, SSM_STATE)
    b_c = jnp.concatenate([slab(bb_re), slab(bb_im)], axis=2)
    c_c = jnp.concatenate([slab(c_re), slab(c_im)], axis=2)
    pr = pw_re.reshape(CHUNK + 1, N_SLABS, SLAB_STATE)
    pi = pw_im.reshape(CHUNK + 1, N_SLABS, SLAB_STATE)
    pwl = jnp.concatenate([jnp.transpose(pr, (1, 0, 2)), jnp.transpose(pi, (1, 0, 2))], axis=2)
    return b_c, c_c, pwl


_NT = (((1,), (1,)), ((), ()))


def _block_diag(x):
    tiled = jnp.concatenate([x] * (LANES // SSM_GS), axis=1)
    shift = lambda n: n.bit_length() - 1
    row_group = lax.shift_right_logical(lax.broadcasted_iota(jnp.int32, tiled.shape, 0), shift(SSM_GS))
    lane_group = lax.shift_right_logical(lax.broadcasted_iota(jnp.int32, tiled.shape, 1), shift(SSM_STATE))
    return jnp.where(row_group == lane_group, tiled, 0.0)


def _s5_kernel(up_ref, um_ref, us_ref, h0re_ref, h0im_ref, b_ref, ct_ref, pwl_ref, d_ref,
               yp_ref, yt_ref, hpre_ref, hpim_ref, hsre_ref, hsim_ref,
               l_ref, f_ref, m_ref, et_ref, vre_ref, vim_ref, yf_ref, hs_ref):
    ns = SLAB_STATE
    b_re = _block_diag(b_ref[0][:, :SSM_STATE])
    b_im = _block_diag(b_ref[0][:, SSM_STATE:])
    ct_re = _block_diag(ct_ref[0][:, :SSM_STATE])
    ct_im = _block_diag(ct_ref[0][:, SSM_STATE:])
    ct_neg = jnp.concatenate([ct_re, -ct_im], axis=1).astype(BF16)

    for b in range(BATCH):
        r0 = b * SEQ_STRIDE + FOLD_ROW0
        l_ref[r0 - FOLD_ROW0:r0, :] = jnp.zeros((FOLD_ROW0, CHUNK * LANES), F32)
        for s in range(CHUNK):
            cols = slice(s * LANES, (s + 1) * LANES)
            l_ref[r0:r0 + SEQ // CHUNK, cols] = up_ref[pl.ds(b * SEQ + s, SEQ // CHUNK, stride=CHUNK), :]
            for m in range(N_META_CHUNKS):
                r = r0 - N_META_CHUNKS + m
                l_ref[r:r + 1, cols] = um_ref[m * CHUNK + s:m * CHUNK + s + 1, :]
    lb = l_ref[...].astype(BF16)

    for s in range(CHUNK):
        p = pwl_ref[0, CHUNK - 1 - s:CHUNK - s, :]
        p_re, p_im = p[:, :ns], p[:, ns:]
        rows = slice(s * LANES, (s + 1) * LANES)
        f_ref[rows, :ns] = (b_re * p_re - b_im * p_im).astype(BF16)
        f_ref[rows, ns:] = (b_re * p_im + b_im * p_re).astype(BF16)
    v = jnp.dot(lb, f_ref[...], preferred_element_type=F32)
    nk = ns // LANES
    for k in range(nk):
        vre_ref[k] = v[:, k * LANES:(k + 1) * LANES]
        vim_ref[k] = v[:, ns + k * LANES:ns + (k + 1) * LANES]

    a = pwl_ref[0, CHUNK:CHUNK + 1, :]
    a_re = [jnp.broadcast_to(a[:, k * LANES:(k + 1) * LANES], (BATCH, LANES)) for k in range(nk)]
    a_im = [jnp.broadcast_to(a[:, ns + k * LANES:ns + (k + 1) * LANES], (BATCH, LANES)) for k in range(nk)]
    h_re = [jnp.zeros((BATCH, LANES), F32) for _ in range(nk)]
    h_im = [jnp.zeros((BATCH, LANES), F32) for _ in range(nk)]
    for j in range(N_CHUNKS):
        rows = pl.ds(FOLD_ROW0 - N_META_CHUNKS + j, BATCH, stride=SEQ_STRIDE)
        for k in range(nk):
            v_re = vre_ref[k, rows, :]
            v_im = vim_ref[k, rows, :]
            vre_ref[k, rows, :] = h_re[k]
            vim_ref[k, rows, :] = h_im[k]
            h_re[k], h_im[k] = (a_re[k] * h_re[k] - a_im[k] * h_im[k] + v_re,
                                a_re[k] * h_im[k] + a_im[k] * h_re[k] + v_im)
    for k in range(nk):
        hpre_ref[:, k * LANES:(k + 1) * LANES] = h_re[k]
        hpim_ref[:, k * LANES:(k + 1) * LANES] = h_im[k]

    kb_all = lax.dot_general(f_ref[...], ct_neg, _NT, preferred_element_type=F32)
    ri = lax.broadcasted_iota(jnp.int32, (LANES, LANES), 0)
    ci = lax.broadcasted_iota(jnp.int32, (LANES, LANES), 1)
    d_diag = jnp.where(ri == ci, jnp.broadcast_to(d_ref[...], (LANES, LANES)), 0.0)
    kb = [None] * CHUNK
    for s in range(CHUNK):
        blk = kb_all[s * LANES:(s + 1) * LANES, :]
        if s == CHUNK - 1:
            blk = blk + d_diag
        kb[CHUNK - 1 - s] = blk.astype(BF16)

    half = CHUNK // 2
    for s in range(CHUNK):
        for t in range(half, CHUNK):
            blk = kb[t - s] if t >= s else jnp.zeros((LANES, LANES), BF16)
            m_ref[s * LANES:(s + 1) * LANES, (t - half) * LANES:(t - half + 1) * LANES] = blk
    y_left = jnp.dot(lb[:, :half * LANES], m_ref[half * LANES:, :], preferred_element_type=F32)
    y_right = jnp.dot(lb, m_ref[...], preferred_element_type=F32)

    for t in range(CHUNK):
        p = pwl_ref[0, t + 1:t + 2, :]
        p_re, p_im = p[:, :ns], p[:, ns:]
        rows = slice(t * LANES, (t + 1) * LANES)
        et_ref[rows, :ns] = (ct_re * p_re - ct_im * p_im).astype(BF16)
        et_ref[rows, ns:] = (-(ct_re * p_im + ct_im * p_re)).astype(BF16)
    h_in = jnp.concatenate([vre_ref[k] for k in range(nk)] + [vim_ref[k] for k in range(nk)],
                           axis=1).astype(BF16)
    ys = lax.dot_general(h_in, et_ref[...], _NT, preferred_element_type=F32)
    yf_ref[:, :half * LANES] = y_left + ys[:, :half * LANES]
    yf_ref[:, half * LANES:] = y_right + ys[:, half * LANES:]
    for b in range(BATCH):
        r0 = b * SEQ_STRIDE + FOLD_ROW0
        for t in range(CHUNK):
            yp_ref[pl.ds(b * SEQ + t, SEQ // CHUNK, stride=CHUNK), :] = (
                yf_ref[r0:r0 + SEQ // CHUNK, t * LANES:(t + 1) * LANES])

    u = us_ref[...]
    b_all = jnp.concatenate([b_re, b_im], axis=1).astype(BF16)
    bu = jnp.dot(u.astype(BF16), b_all, preferred_element_type=F32)
    l1 = pwl_ref[0, 1:2, :]
    l_re, l_im = l1[:, :ns], l1[:, ns:]
    g_re = h0re_ref[...]
    g_im = h0im_ref[...]
    for t in range(DEC_SEQ):
        rows = slice(t * DEC_BATCH, (t + 1) * DEC_BATCH)
        g_re, g_im = (l_re * g_re - l_im * g_im + bu[rows, :ns],
                      l_re * g_im + l_im * g_re + bu[rows, ns:])
        hs_ref[rows, :ns] = g_re.astype(BF16)
        hs_ref[rows, ns:] = g_im.astype(BF16)
    hsre_ref[...] = g_re
    hsim_ref[...] = g_im
    yt_ref[0:N_SAMPLE_ROWS, :] = (lax.dot_general(hs_ref[...], ct_neg, _NT, preferred_element_type=F32)
                                  + d_ref[...] * u)
    yt_ref[N_SAMPLE_ROWS:, :] = jnp.zeros((N_META, LANES), F32)


def _s5(u_p, u_t, h0_re, h0_im, b_t, ct_t, pwl, d_skip):
    c0 = W_POOL // LANES
    ns = SLAB_STATE
    state = lambda rows: pl.BlockSpec((rows, ns), lambda i: (0, i))
    return pl.pallas_call(
        _s5_kernel,
        grid=(N_SLABS,),
        in_specs=[
            pl.BlockSpec((N_PROMPT_ROWS, LANES), lambda i: (0, c0 + i)),
            pl.BlockSpec((N_META, LANES), lambda i: (N_SAMPLE_ROWS // N_META, c0 + i)),
            pl.BlockSpec((N_SAMPLE_ROWS, LANES), lambda i: (0, c0 + i)),
            state(DEC_BATCH), state(DEC_BATCH),
            pl.BlockSpec((1, LANES, 2 * SSM_STATE), lambda i: (i, 0, 0)),
            pl.BlockSpec((1, LANES, 2 * SSM_STATE), lambda i: (i, 0, 0)),
            pl.BlockSpec((1, CHUNK + 1, 2 * ns), lambda i: (i, 0, 0)),
            pl.BlockSpec((1, LANES), lambda i: (0, i)),
        ],
        out_specs=[
            pl.BlockSpec((N_PROMPT_ROWS, LANES), lambda i: (0, i)),
            pl.BlockSpec((N_TAIL_ROWS, LANES), lambda i: (0, i)),
            state(BATCH), state(BATCH), state(DEC_BATCH), state(DEC_BATCH),
        ],
        out_shape=[
            jax.ShapeDtypeStruct((N_PROMPT_ROWS, W_SSM), F32),
            jax.ShapeDtypeStruct((N_TAIL_ROWS, W_SSM), F32),
            jax.ShapeDtypeStruct((BATCH, N_GROUPS * SSM_STATE), F32),
            jax.ShapeDtypeStruct((BATCH, N_GROUPS * SSM_STATE), F32),
            jax.ShapeDtypeStruct((DEC_BATCH, N_GROUPS * SSM_STATE), F32),
            jax.ShapeDtypeStruct((DEC_BATCH, N_GROUPS * SSM_STATE), F32),
        ],
        scratch_shapes=[
            pltpu.VMEM((FOLD_ROWS, CHUNK * LANES), F32),
            pltpu.VMEM((CHUNK * LANES, 2 * ns), BF16),
            pltpu.VMEM((CHUNK * LANES, CHUNK * LANES // 2), BF16),
            pltpu.VMEM((CHUNK * LANES, 2 * ns), BF16),
            pltpu.VMEM((ns // LANES, FOLD_ROWS, LANES), F32),
            pltpu.VMEM((ns // LANES, FOLD_ROWS, LANES), F32),
            pltpu.VMEM((FOLD_ROWS, CHUNK * LANES), F32),
            pltpu.VMEM((N_SAMPLE_ROWS, 2 * ns), BF16),
        ],
        compiler_params=_params(1),
        name="s5",
    )(u_p, u_t, u_t, h0_re, h0_im, b_t, ct_t, pwl, d_skip)


def _out_kernel(x_ref, a_ref, sy_ref, g_ref, wga_ref, wgb_ref, bga_ref, bgb_ref, gluw_ref, glub_ref,
                wa_ref, wb_ref, wo_ref, o_ref, xn_ref, s_ref):
    j = pl.program_id(1)

    @pl.when(j == 0)
    def _():
        x = x_ref[...]
        xn_ref[...] = _rms(x, g_ref[...]).astype(BF16)
        o_ref[...] = x
        s = jax.nn.gelu(sy_ref[...])
        z = jnp.dot(s.astype(BF16), gluw_ref[...], preferred_element_type=F32) + glub_ref[...]
        s_ref[...] = (s * jax.nn.sigmoid(z)).astype(BF16)

    xn = xn_ref[...]
    ga = jax.nn.sigmoid(jnp.dot(xn, wga_ref[...], preferred_element_type=F32) + bga_ref[...])
    gb = jax.nn.sigmoid(jnp.dot(xn, wgb_ref[...], preferred_element_type=F32) + bgb_ref[...])
    merged = (ga * jnp.dot(a_ref[...], wa_ref[...], preferred_element_type=F32)
              + gb * jnp.dot(s_ref[...], wb_ref[...], preferred_element_type=F32))
    o_ref[...] += jnp.dot(merged.astype(BF16), wo_ref[...], preferred_element_type=F32)


def _out_proj(x, a, sy, g, w_in, b_gate, glu_w, glu_b, w_a, w_b, w_out, tm):
    rows = x.shape[0]
    nb = D_MODEL // TN_OUT
    g0 = (W_POOL + W_SSM) // TN_OUT
    return pl.pallas_call(
        _out_kernel,
        grid=(rows // tm, nb),
        in_specs=[
            pl.BlockSpec((tm, D_MODEL), lambda i, j: (i, 0)),
            pl.BlockSpec((tm, W_POOL), lambda i, j: (i, 0)),
            pl.BlockSpec((tm, W_SSM), lambda i, j: (i, 0)),
            pl.BlockSpec((1, D_MODEL), lambda i, j: (0, 0)),
            pl.BlockSpec((D_MODEL, TN_OUT), lambda i, j: (0, g0 + j)),
            pl.BlockSpec((D_MODEL, TN_OUT), lambda i, j: (0, g0 + nb + j)),
            pl.BlockSpec((1, TN_OUT), lambda i, j: (0, j)),
            pl.BlockSpec((1, TN_OUT), lambda i, j: (0, nb + j)),
            pl.BlockSpec((W_SSM, W_SSM), lambda i, j: (0, 0)),
            pl.BlockSpec((1, W_SSM), lambda i, j: (0, 0)),
            pl.BlockSpec((W_POOL, TN_OUT), lambda i, j: (0, j)),
            pl.BlockSpec((W_SSM, TN_OUT), lambda i, j: (0, j)),
            pl.BlockSpec((TN_OUT, D_MODEL), lambda i, j: (j, 0)),
        ],
        out_specs=pl.BlockSpec((tm, D_MODEL), lambda i, j: (i, 0)),
        out_shape=jax.ShapeDtypeStruct((rows, D_MODEL), F32),
        scratch_shapes=[pltpu.VMEM((tm, D_MODEL), BF16), pltpu.VMEM((tm, W_SSM), BF16)],
        compiler_params=_params(2),
        name="out_proj",
    )(x, a, sy, g, w_in, w_in, b_gate, b_gate, glu_w, glu_b, w_a, w_b, w_out)


def kernel(x_prompt, x_sample, state_pool, state_ssm_re, state_ssm_im, meta_tokens, norm_ffn1, ffn1_w_gate, ffn1_w_up, ffn1_w_down, norm_mix, w_in, b_gate, pool_w, pool_scale, ssm_lambda_re, ssm_lambda_im, ssm_log_dt, ssm_b_re, ssm_b_im, ssm_c_re, ssm_c_im, ssm_d, glu_w, glu_b, w_branch_a, w_branch_b, w_out, norm_ffn2, ffn2_w_gate, ffn2_w_up, ffn2_w_down, final_norm):
    l = 0
    bf = lambda w: w.astype(BF16)
    row = lambda v: v.reshape(1, -1).astype(F32)
    tiles = (((TM_FFN, TF), TM), ((N_TAIL_ROWS, TF_TAIL), N_TAIL_ROWS))

    xs = (x_prompt.reshape(N_PROMPT_ROWS, D_MODEL),
          jnp.concatenate([jnp.transpose(x_sample, (1, 0, 2)).reshape(N_SAMPLE_ROWS, D_MODEL),
                           meta_tokens.astype(F32)], axis=0))

    fg = row(final_norm)
    w_in_b = bf(w_in[l])
    x1 = [_ffn(x, row(norm_ffn1[l]), ffn1_w_gate[l], ffn1_w_up[l], ffn1_w_down[l], fg, *tm_ffn, False)
          for x, (tm_ffn, _) in zip(xs, tiles)]

    pool_wb, pool_sc = bf(pool_w[l]), row(pool_scale[l])
    hist_t = jnp.transpose(state_pool[l], (1, 0, 2))
    u_t = _inproj(x1[1], row(norm_mix[l]), w_in_b, N_TAIL_ROWS)
    a_t = _pool_tail(u_t, hist_t, pool_wb, pool_sc)
    u_p, a_p = _inproj_pool_prompt(x1[0], row(norm_mix[l]), w_in_b, u_t, pool_wb, pool_sc)
    a_out = (a_p, a_t)

    b_t, ct_t, pwl = _s5_tables(ssm_lambda_re[l], ssm_lambda_im[l], ssm_log_dt[l], ssm_b_re[l],
                                ssm_b_im[l], ssm_c_re[l], ssm_c_im[l])
    sy_p, sy_t, hp_re, hp_im, hs_re, hs_im = _s5(
        u_p, u_t, state_ssm_re[l].reshape(DEC_BATCH, -1), state_ssm_im[l].reshape(DEC_BATCH, -1),
        b_t, ct_t, pwl, row(ssm_d[l]))

    mix_w = (bf(glu_w[l]), row(glu_b[l]), bf(w_branch_a[l]), bf(w_branch_b[l]), bf(w_out[l]))
    y = []
    for x, a, sy, (tm_ffn, tm) in zip(x1, a_out, (sy_p, sy_t), tiles):
        x2 = _out_proj(x, a, sy, row(norm_mix[l]), w_in_b, row(b_gate[l]), *mix_w, tm)
        y.append(_ffn(x2, row(norm_ffn2[l]), ffn2_w_gate[l], ffn2_w_up[l], ffn2_w_down[l], fg, *tm_ffn, True))
    y_p, y_t = y

    y_prompt = y_p.reshape(BATCH, SEQ, D_MODEL)
    y_sample = jnp.transpose(y_t[:N_SAMPLE_ROWS].reshape(DEC_SEQ, DEC_BATCH, D_MODEL), (1, 0, 2))
    pool_p = jnp.stack([u_p[(b + 1) * SEQ - POOL_HIST:(b + 1) * SEQ, :W_POOL] for b in range(BATCH)])[None]
    u_pool_s = jnp.transpose(u_t[:N_SAMPLE_ROWS, :W_POOL].reshape(DEC_SEQ, DEC_BATCH, W_POOL), (1, 0, 2))
    pool_s = jnp.concatenate([state_pool[l][:, DEC_SEQ:], u_pool_s], axis=1)[None]
    shp_p = (1, BATCH, N_GROUPS, SSM_STATE)
    shp_s = (1, DEC_BATCH, N_GROUPS, SSM_STATE)
    return (y_prompt, y_sample, pool_p, pool_s,
            hp_re.reshape(shp_p), hp_im.reshape(shp_p), hs_re.reshape(shp_s), hs_im.reshape(shp_s))
```

```python
import functools

import jax
import jax.numpy as jnp
from jax import lax
from jax.experimental import pallas as pl
from jax.experimental.pallas import tpu as pltpu

F32 = jnp.float32
BF16 = jnp.bfloat16

D_MODEL = 2048
BATCH = 4
SEQ = 2048
DEC_BATCH = 128
DEC_SEQ = 4
N_META = 16
D_FF = 5632
W_POOL = 1024
W_SSM = 1024
POOL_WINDOWS = (2, 4, 8, 16)
POOL_GW = 256
POOL_HIST = 15
SSM_GS = 16
N_GROUPS = 64
SSM_STATE = 64
RMS_EPS = 1e-6

N_PROMPT_ROWS = BATCH * SEQ
N_SAMPLE_ROWS = DEC_BATCH * DEC_SEQ
N_TAIL_ROWS = N_SAMPLE_ROWS + N_META
TM_FFN = 1024
TM = 512
TF = 256
TF_TAIL = 512
TN_OUT = 512

LANES = 128
N_SLABS = W_SSM // LANES
SLAB_STATE = (LANES // SSM_GS) * SSM_STATE
CHUNK = 8
N_META_CHUNKS = N_META // CHUNK
N_CHUNKS = (SEQ + N_META) // CHUNK
FOLD_ROW0 = 8
SEQ_STRIDE = FOLD_ROW0 + SEQ // CHUNK
FOLD_ROWS = BATCH * SEQ_STRIDE

V7X_VMEM_BYTES = 64 * 1024 * 1024
VMEM_LIMIT = V7X_VMEM_BYTES - 4 * 1024 * 1024


def _rms(x, g):
    r = lax.rsqrt(jnp.mean(x * x, axis=-1, keepdims=True) + RMS_EPS)
    return x * r * g


def _params(n_axes):
    return pltpu.CompilerParams(dimension_semantics=("arbitrary",) * n_axes, vmem_limit_bytes=VMEM_LIMIT)


def _ffn_kernel(x_ref, g_ref, wg_ref, wu_ref, wd_ref, fg_ref, o_ref, *rest, final_norm, emit_bf16):
    xn_ref = rest[-1]
    j = pl.program_id(1)

    @pl.when(j == 0)
    def _():
        x = x_ref[...]
        xn_ref[...] = _rms(x, g_ref[...]).astype(BF16)
        o_ref[...] = x

    xn = xn_ref[...]
    wg = wg_ref[...].astype(BF16)
    wu = wu_ref[...].astype(BF16)
    wd = wd_ref[...].astype(BF16)
    if emit_bf16:
        wgb_ref, wub_ref, wdb_ref = rest[:3]
        wgb_ref[...] = wg
        wub_ref[...] = wu
        wdb_ref[...] = wd
    gate = jnp.dot(xn, wg, preferred_element_type=F32)
    up = jnp.dot(xn, wu, preferred_element_type=F32)
    h = (gate * jax.nn.sigmoid(gate) * up * 0.5).astype(BF16)
    o_ref[...] += jnp.dot(h, wd, preferred_element_type=F32)

    if final_norm:
        @pl.when(j == pl.num_programs(1) - 1)
        def _():
            o_ref[...] = _rms(o_ref[...], fg_ref[...])


def _ffn(x, g, wg, wu, wd, fg, tm, tf, final_norm, emit_bf16=False):
    rows = x.shape[0]
    n_steps = D_FF // tf
    row_spec = pl.BlockSpec((tm, D_MODEL), lambda i, j: (i, 0))
    out_specs, out_shape = row_spec, jax.ShapeDtypeStruct((rows, D_MODEL), F32)
    if emit_bf16:
        once = lambda i, j: jnp.where(i == 0, j, n_steps - 1)
        out_specs = [row_spec,
                     pl.BlockSpec((D_MODEL, tf), lambda i, j: (0, once(i, j))),
                     pl.BlockSpec((D_MODEL, tf), lambda i, j: (0, once(i, j))),
                     pl.BlockSpec((tf, D_MODEL), lambda i, j: (once(i, j), 0))]
        out_shape = [out_shape] + [jax.ShapeDtypeStruct(w.shape, BF16) for w in (wg, wu, wd)]
    return pl.pallas_call(
        functools.partial(_ffn_kernel, final_norm=final_norm, emit_bf16=emit_bf16),
        grid=(rows // tm, n_steps),
        in_specs=[
            row_spec,
            pl.BlockSpec((1, D_MODEL), lambda i, j: (0, 0)),
            pl.BlockSpec((D_MODEL, tf), lambda i, j: (0, j)),
            pl.BlockSpec((D_MODEL, tf), lambda i, j: (0, j)),
            pl.BlockSpec((tf, D_MODEL), lambda i, j: (j, 0)),
            pl.BlockSpec((1, D_MODEL), lambda i, j: (0, 0)),
        ],
        out_specs=out_specs,
        out_shape=out_shape,
        scratch_shapes=[pltpu.VMEM((tm, D_MODEL), BF16)],
        compiler_params=_params(2),
        name="ffn_final" if final_norm else "ffn",
    )(x, g, wg, wu, wd, fg)


def _inproj_kernel(x_ref, g_ref, w_ref, o_ref):
    xn = _rms(x_ref[...], g_ref[...]).astype(BF16)
    o_ref[...] = jnp.dot(xn, w_ref[...], preferred_element_type=F32)


def _inproj(x, g, w_in, tm):
    rows = x.shape[0]
    n = W_POOL + W_SSM
    return pl.pallas_call(
        _inproj_kernel,
        grid=(rows // tm,),
        in_specs=[
            pl.BlockSpec((tm, D_MODEL), lambda i: (i, 0)),
            pl.BlockSpec((1, D_MODEL), lambda i: (0, 0)),
            pl.BlockSpec((D_MODEL, n), lambda i: (0, 0)),
        ],
        out_specs=pl.BlockSpec((tm, n), lambda i: (i, 0)),
        out_shape=jax.ShapeDtypeStruct((rows, n), F32),
        compiler_params=_params(1),
        name="inproj",
    )(x, g, w_in)


TILES_PER_SEQ = SEQ // TM
HALO = 16


def _pool_project(mean_ref, u_ref, pw_ref, scale_ref, o_ref):
    for g in range(len(POOL_WINDOWS)):
        cols = slice(g * POOL_GW, (g + 1) * POOL_GW)
        d = (mean_ref[:, cols] - u_ref[:, cols]).astype(BF16)
        y = jnp.dot(d, pw_ref[g], preferred_element_type=F32) * scale_ref[:, cols]
        o_ref[:, cols] = y.astype(BF16)


def _inproj_pool_kernel(x_ref, g_ref, w_ref, meta_ref, pw_ref, scale_ref, u_ref, o_ref, full_ref):
    @pl.when(pl.program_id(0) % TILES_PER_SEQ == 0)
    def _():
        full_ref[0:HALO, :] = meta_ref[...]

    xn = _rms(x_ref[...], g_ref[...]).astype(BF16)
    u_pool = jnp.dot(xn, w_ref[:, :W_POOL], preferred_element_type=F32)
    u_ref[:, :W_POOL] = u_pool
    full_ref[HALO:HALO + TM, :] = u_pool
    for g, w in enumerate(POOL_WINDOWS):
        cols = slice(g * POOL_GW, (g + 1) * POOL_GW)
        ucols = slice(W_POOL + g * POOL_GW, W_POOL + (g + 1) * POOL_GW)
        u_ref[:, ucols] = jnp.dot(xn, w_ref[:, ucols], preferred_element_type=F32)
        acc = full_ref[HALO:HALO + TM, cols]
        for k in range(1, w):
            acc = acc + full_ref[HALO - k:HALO - k + TM, cols]
        d = (acc * (1.0 / w) - full_ref[HALO:HALO + TM, cols]).astype(BF16)
        y = jnp.dot(d, pw_ref[g], preferred_element_type=F32) * scale_ref[:, cols]
        o_ref[:, cols] = y.astype(BF16)
    full_ref[0:HALO, :] = full_ref[TM:TM + HALO, :]


def _pool_tail_kernel(u_ref, hist_ref, pw_ref, scale_ref, o_ref, mean_ref):
    for g, w in enumerate(POOL_WINDOWS):
        cols = slice(g * POOL_GW, (g + 1) * POOL_GW)
        for t in range(DEC_SEQ):
            acc = None
            for k in range(w):
                p = POOL_HIST + t - k
                if p >= POOL_HIST:
                    q = p - POOL_HIST
                    term = u_ref[q * DEC_BATCH:(q + 1) * DEC_BATCH, cols]
                else:
                    term = hist_ref[p, :, cols]
                acc = term if acc is None else acc + term
            mean_ref[t * DEC_BATCH:(t + 1) * DEC_BATCH, cols] = acc * (1.0 / w)
    mean_ref[N_SAMPLE_ROWS:, :] = u_ref[N_SAMPLE_ROWS:, :]
    _pool_project(mean_ref, u_ref, pw_ref, scale_ref, o_ref)


_PW_SPEC = pl.BlockSpec((len(POOL_WINDOWS), POOL_GW, POOL_GW), lambda i: (0, 0, 0))
_SCALE_SPEC = pl.BlockSpec((1, W_POOL), lambda i: (0, 0))


def _inproj_pool_prompt(x, g, w_in, u_t, pool_w, pool_scale):
    n = W_POOL + W_SSM
    return pl.pallas_call(
        _inproj_pool_kernel,
        grid=(N_PROMPT_ROWS // TM,),
        in_specs=[
            pl.BlockSpec((TM, D_MODEL), lambda i: (i, 0)),
            pl.BlockSpec((1, D_MODEL), lambda i: (0, 0)),
            pl.BlockSpec((D_MODEL, n), lambda i: (0, 0)),
            pl.BlockSpec((N_META, W_POOL), lambda i: (N_SAMPLE_ROWS // N_META, 0)),
            _PW_SPEC, _SCALE_SPEC,
        ],
        out_specs=[pl.BlockSpec((TM, n), lambda i: (i, 0)), pl.BlockSpec((TM, W_POOL), lambda i: (i, 0))],
        out_shape=[jax.ShapeDtypeStruct((N_PROMPT_ROWS, n), F32),
                   jax.ShapeDtypeStruct((N_PROMPT_ROWS, W_POOL), BF16)],
        scratch_shapes=[pltpu.VMEM((HALO + TM, W_POOL), F32)],
        compiler_params=_params(1),
        name="inproj_pool",
    )(x, g, w_in, u_t, pool_w, pool_scale)


def _pool_tail(u_t, hist_t, pool_w, pool_scale):
    return pl.pallas_call(
        _pool_tail_kernel,
        grid=(1,),
        in_specs=[
            pl.BlockSpec((N_TAIL_ROWS, W_POOL), lambda i: (0, 0)),
            pl.BlockSpec((POOL_HIST, DEC_BATCH, W_POOL), lambda i: (0, 0, 0)),
            _PW_SPEC, _SCALE_SPEC,
        ],
        out_specs=pl.BlockSpec((N_TAIL_ROWS, W_POOL), lambda i: (0, 0)),
        out_shape=jax.ShapeDtypeStruct((N_TAIL_ROWS, W_POOL), BF16),
        scratch_shapes=[pltpu.VMEM((N_TAIL_ROWS, W_POOL), F32)],
        compiler_params=_params(1),
        name="pool_tail",
    )(u_t, hist_t, pool_w, pool_scale)


def _s5_tables(lam_re, lam_im, log_dt, b_re, b_im, c_re, c_im):
    dt = jnp.exp(log_dt)[:, None]
    k = jnp.arange(CHUNK + 1, dtype=F32)[:, None, None]
    mag = jnp.exp(k * (lam_re * dt)[None])
    ang = k * (lam_im * dt)[None]
    pw_re, pw_im = mag * jnp.cos(ang), mag * jnp.sin(ang)
    lb_re, lb_im = pw_re[1], pw_im[1]
    den = lam_re * lam_re + lam_im * lam_im
    q_re = ((lb_re - 1.0) * lam_re + lb_im * lam_im) / den
    q_im = (lb_im * lam_re - (lb_re - 1.0) * lam_im) / den
    bt_re, bt_im = jnp.transpose(b_re, (0, 2, 1)), jnp.transpose(b_im, (0, 2, 1))
    bb_re = q_re[:, None, :] * bt_re - q_im[:, None, :] * bt_im
    bb_im = q_re[:, None, :] * bt_im + q_im[:, None, :] * bt_re
    slab = lambda t: t.reshape(N_SLABS, LANES, SSM_STATE)
    b_c = jnp.concatenate([slab(bb_re), slab(bb_im)], axis=2)
    c_c = jnp.concatenate([slab(c_re), slab(c_im)], axis=2)
    pr = pw_re.reshape(CHUNK + 1, N_SLABS, SLAB_STATE)
    pi = pw_im.reshape(CHUNK + 1, N_SLABS, SLAB_STATE)
    pwl = jnp.concatenate([jnp.transpose(pr, (1, 0, 2)), jnp.transpose(pi, (1, 0, 2))], axis=2)
    return b_c, c_c, pwl


_NT = (((1,), (1,)), ((), ()))


def _block_diag(x):
    tiled = jnp.concatenate([x] * (LANES // SSM_GS), axis=1)
    shift = lambda n: n.bit_length() - 1
    row_group = lax.shift_right_logical(lax.broadcasted_iota(jnp.int32, tiled.shape, 0), shift(SSM_GS))
    lane_group = lax.shift_right_logical(lax.broadcasted_iota(jnp.int32, tiled.shape, 1), shift(SSM_STATE))
    return jnp.where(row_group == lane_group, tiled, 0.0)


def _s5_kernel(up_ref, um_ref, us_ref, h0re_ref, h0im_ref, b_ref, ct_ref, pwl_ref, d_ref,
               yp_ref, yt_ref, hpre_ref, hpim_ref, hsre_ref, hsim_ref,
               l_ref, f_ref, m_ref, et_ref, vre_ref, vim_ref, yf_ref, hs_ref):
    ns = SLAB_STATE
    b_re = _block_diag(b_ref[0][:, :SSM_STATE])
    b_im = _block_diag(b_ref[0][:, SSM_STATE:])
    ct_re = _block_diag(ct_ref[0][:, :SSM_STATE])
    ct_im = _block_diag(ct_ref[0][:, SSM_STATE:])
    ct_neg = jnp.concatenate([ct_re, -ct_im], axis=1).astype(BF16)

    for b in range(BATCH):
        r0 = b * SEQ_STRIDE + FOLD_ROW0
        l_ref[r0 - FOLD_ROW0:r0, :] = jnp.zeros((FOLD_ROW0, CHUNK * LANES), F32)
        for s in range(CHUNK):
            cols = slice(s * LANES, (s + 1) * LANES)
            l_ref[r0:r0 + SEQ // CHUNK, cols] = up_ref[pl.ds(b * SEQ + s, SEQ // CHUNK, stride=CHUNK), :]
            for m in range(N_META_CHUNKS):
                r = r0 - N_META_CHUNKS + m
                l_ref[r:r + 1, cols] = um_ref[m * CHUNK + s:m * CHUNK + s + 1, :]
    lb = l_ref[...].astype(BF16)

    for s in range(CHUNK):
        p = pwl_ref[0, CHUNK - 1 - s:CHUNK - s, :]
        p_re, p_im = p[:, :ns], p[:, ns:]
        rows = slice(s * LANES, (s + 1) * LANES)
        f_ref[rows, :ns] = (b_re * p_re - b_im * p_im).astype(BF16)
        f_ref[rows, ns:] = (b_re * p_im + b_im * p_re).astype(BF16)
    v = jnp.dot(lb, f_ref[...], preferred_element_type=F32)
    nk = ns // LANES
    for k in range(nk):
        vre_ref[k] = v[:, k * LANES:(k + 1) * LANES]
        vim_ref[k] = v[:, ns + k * LANES:ns + (k + 1) * LANES]

    a = pwl_ref[0, CHUNK:CHUNK + 1, :]
    a_re = [jnp.broadcast_to(a[:, k * LANES:(k + 1) * LANES], (BATCH, LANES)) for k in range(nk)]
    a_im = [jnp.broadcast_to(a[:, ns + k * LANES:ns + (k + 1) * LANES], (BATCH, LANES)) for k in range(nk)]
    h_re = [jnp.zeros((BATCH, LANES), F32) for _ in range(nk)]
    h_im = [jnp.zeros((BATCH, LANES), F32) for _ in range(nk)]
    for j in range(N_CHUNKS):
        rows = pl.ds(FOLD_ROW0 - N_META_CHUNKS + j, BATCH, stride=SEQ_STRIDE)
        for k in range(nk):
            v_re = vre_ref[k, rows, :]
            v_im = vim_ref[k, rows, :]
            vre_ref[k, rows, :] = h_re[k]
            vim_ref[k, rows, :] = h_im[k]
            h_re[k], h_im[k] = (a_re[k] * h_re[k] - a_im[k] * h_im[k] + v_re,
                                a_re[k] * h_im[k] + a_im[k] * h_re[k] + v_im)
    for k in range(nk):
        hpre_ref[:, k * LANES:(k + 1) * LANES] = h_re[k]
        hpim_ref[:, k * LANES:(k + 1) * LANES] = h_im[k]

    kb_all = lax.dot_general(f_ref[...], ct_neg, _NT, preferred_element_type=F32)
    ri = lax.broadcasted_iota(jnp.int32, (LANES, LANES), 0)
    ci = lax.broadcasted_iota(jnp.int32, (LANES, LANES), 1)
    d_diag = jnp.where(ri == ci, jnp.broadcast_to(d_ref[...], (LANES, LANES)), 0.0)
    kb = [None] * CHUNK
    for s in range(CHUNK):
        blk = kb_all[s * LANES:(s + 1) * LANES, :]
        if s == CHUNK - 1:
            blk = blk + d_diag
        kb[CHUNK - 1 - s] = blk.astype(BF16)

    half = CHUNK // 2
    for s in range(CHUNK):
        for t in range(half, CHUNK):
            blk = kb[t - s] if t >= s else jnp.zeros((LANES, LANES), BF16)
            m_ref[s * LANES:(s + 1) * LANES, (t - half) * LANES:(t - half + 1) * LANES] = blk
    y_left = jnp.dot(lb[:, :half * LANES], m_ref[half * LANES:, :], preferred_element_type=F32)
    y_right = jnp.dot(lb, m_ref[...], preferred_element_type=F32)

    for t in range(CHUNK):
        p = pwl_ref[0, t + 1:t + 2, :]
        p_re, p_im = p[:, :ns], p[:, ns:]
        rows = slice(t * LANES, (t + 1) * LANES)
        et_ref[rows, :ns] = (ct_re * p_re - ct_im * p_im).astype(BF16)
        et_ref[rows, ns:] = (-(ct_re * p_im + ct_im * p_re)).astype(BF16)
    h_in = jnp.concatenate([vre_ref[k] for k in range(nk)] + [vim_ref[k] for k in range(nk)],
                           axis=1).astype(BF16)
    ys = lax.dot_general(h_in, et_ref[...], _NT, preferred_element_type=F32)
    yf_ref[:, :half * LANES] = y_left + ys[:, :half * LANES]
    yf_ref[:, half * LANES:] = y_right + ys[:, half * LANES:]
    for b in range(BATCH):
        r0 = b * SEQ_STRIDE + FOLD_ROW0
        for t in range(CHUNK):
            yp_ref[pl.ds(b * SEQ + t, SEQ // CHUNK, stride=CHUNK), :] = (
                yf_ref[r0:r0 + SEQ // CHUNK, t * LANES:(t + 1) * LANES])

    u = us_ref[...]
    b_all = jnp.concatenate([b_re, b_im], axis=1).astype(BF16)
    bu = jnp.dot(u.astype(BF16), b_all, preferred_element_type=F32)
    l1 = pwl_ref[0, 1:2, :]
    l_re, l_im = l1[:, :ns], l1[:, ns:]
    g_re = h0re_ref[...]
    g_im = h0im_ref[...]
    for t in range(DEC_SEQ):
        rows = slice(t * DEC_BATCH, (t + 1) * DEC_BATCH)
        g_re, g_im = (l_re * g_re - l_im * g_im + bu[rows, :ns],
                      l_re * g_im + l_im * g_re + bu[rows, ns:])
        hs_ref[rows, :ns] = g_re.astype(BF16)
        hs_ref[rows, ns:] = g_im.astype(BF16)
    hsre_ref[...] = g_re
    hsim_ref[...] = g_im
    yt_ref[0:N_SAMPLE_ROWS, :] = (lax.dot_general(hs_ref[...], ct_neg, _NT, preferred_element_type=F32)
                                  + d_ref[...] * u)
    yt_ref[N_SAMPLE_ROWS:, :] = jnp.zeros((N_META, LANES), F32)


def _s5(u_p, u_t, h0_re, h0_im, b_t, ct_t, pwl, d_skip):
    c0 = W_POOL // LANES
    ns = SLAB_STATE
    state = lambda rows: pl.BlockSpec((rows, ns), lambda i: (0, i))
    return pl.pallas_call(
        _s5_kernel,
        grid=(N_SLABS,),
        in_specs=[
            pl.BlockSpec((N_PROMPT_ROWS, LANES), lambda i: (0, c0 + i)),
            pl.BlockSpec((N_META, LANES), lambda i: (N_SAMPLE_ROWS // N_META, c0 + i)),
            pl.BlockSpec((N_SAMPLE_ROWS, LANES), lambda i: (0, c0 + i)),
            state(DEC_BATCH), state(DEC_BATCH),
            pl.BlockSpec((1, LANES, 2 * SSM_STATE), lambda i: (i, 0, 0)),
            pl.BlockSpec((1, LANES, 2 * SSM_STATE), lambda i: (i, 0, 0)),
            pl.BlockSpec((1, CHUNK + 1, 2 * ns), lambda i: (i, 0, 0)),
            pl.BlockSpec((1, LANES), lambda i: (0, i)),
        ],
        out_specs=[
            pl.BlockSpec((N_PROMPT_ROWS, LANES), lambda i: (0, i)),
            pl.BlockSpec((N_TAIL_ROWS, LANES), lambda i: (0, i)),
            state(BATCH), state(BATCH), state(DEC_BATCH), state(DEC_BATCH),
        ],
        out_shape=[
            jax.ShapeDtypeStruct((N_PROMPT_ROWS, W_SSM), F32),
            jax.ShapeDtypeStruct((N_TAIL_ROWS, W_SSM), F32),
            jax.ShapeDtypeStruct((BATCH, N_GROUPS * SSM_STATE), F32),
            jax.ShapeDtypeStruct((BATCH, N_GROUPS * SSM_STATE), F32),
            jax.ShapeDtypeStruct((DEC_BATCH, N_GROUPS * SSM_STATE), F32),
            jax.ShapeDtypeStruct((DEC_BATCH, N_GROUPS * SSM_STATE), F32),
        ],
        scratch_shapes=[
            pltpu.VMEM((FOLD_ROWS, CHUNK * LANES), F32),
            pltpu.VMEM((CHUNK * LANES, 2 * ns), BF16),
            pltpu.VMEM((CHUNK * LANES, CHUNK * LANES // 2), BF16),
            pltpu.VMEM((CHUNK * LANES, 2 * ns), BF16),
            pltpu.VMEM((ns // LANES, FOLD_ROWS, LANES), F32),
            pltpu.VMEM((ns // LANES, FOLD_ROWS, LANES), F32),
            pltpu.VMEM((FOLD_ROWS, CHUNK * LANES), F32),
            pltpu.VMEM((N_SAMPLE_ROWS, 2 * ns), BF16),
        ],
        compiler_params=_params(1),
        name="s5",
    )(u_p, u_t, u_t, h0_re, h0_im, b_t, ct_t, pwl, d_skip)


def _out_kernel(x_ref, a_ref, sy_ref, g_ref, wga_ref, wgb_ref, bga_ref, bgb_ref, gluw_ref, glub_ref,
                wa_ref, wb_ref, wo_ref, o_ref, xn_ref, s_ref):
    j = pl.program_id(1)

    @pl.when(j == 0)
    def _():
        x = x_ref[...]
        xn_ref[...] = _rms(x, g_ref[...]).astype(BF16)
        o_ref[...] = x
        s = jax.nn.gelu(sy_ref[...])
        z = jnp.dot(s.astype(BF16), gluw_ref[...], preferred_element_type=F32) + glub_ref[...]
        s_ref[...] = (s * jax.nn.sigmoid(z)).astype(BF16)

    xn = xn_ref[...]
    ga = jax.nn.sigmoid(jnp.dot(xn, wga_ref[...], preferred_element_type=F32) + bga_ref[...])
    gb = jax.nn.sigmoid(jnp.dot(xn, wgb_ref[...], preferred_element_type=F32) + bgb_ref[...])
    merged = (ga * jnp.dot(a_ref[...], wa_ref[...], preferred_element_type=F32)
              + gb * jnp.dot(s_ref[...], wb_ref[...], preferred_element_type=F32))
    o_ref[...] += jnp.dot(merged.astype(BF16), wo_ref[...], preferred_element_type=F32)


def _out_proj(x, a, sy, g, w_in, b_gate, glu_w, glu_b, w_a, w_b, w_out, tm):
    rows = x.shape[0]
    nb = D_MODEL // TN_OUT
    g0 = (W_POOL + W_SSM) // TN_OUT
    return pl.pallas_call(
        _out_kernel,
        grid=(rows // tm, nb),
        in_specs=[
            pl.BlockSpec((tm, D_MODEL), lambda i, j: (i, 0)),
            pl.BlockSpec((tm, W_POOL), lambda i, j: (i, 0)),
            pl.BlockSpec((tm, W_SSM), lambda i, j: (i, 0)),
            pl.BlockSpec((1, D_MODEL), lambda i, j: (0, 0)),
            pl.BlockSpec((D_MODEL, TN_OUT), lambda i, j: (0, g0 + j)),
            pl.BlockSpec((D_MODEL, TN_OUT), lambda i, j: (0, g0 + nb + j)),
            pl.BlockSpec((1, TN_OUT), lambda i, j: (0, j)),
            pl.BlockSpec((1, TN_OUT), lambda i, j: (0, nb + j)),
            pl.BlockSpec((W_SSM, W_SSM), lambda i, j: (0, 0)),
            pl.BlockSpec((1, W_SSM), lambda i, j: (0, 0)),
            pl.BlockSpec((W_POOL, TN_OUT), lambda i, j: (0, j)),
            pl.BlockSpec((W_SSM, TN_OUT), lambda i, j: (0, j)),
            pl.BlockSpec((TN_OUT, D_MODEL), lambda i, j: (j, 0)),
        ],
        out_specs=pl.BlockSpec((tm, D_MODEL), lambda i, j: (i, 0)),
        out_shape=jax.ShapeDtypeStruct((rows, D_MODEL), F32),
        scratch_shapes=[pltpu.VMEM((tm, D_MODEL), BF16), pltpu.VMEM((tm, W_SSM), BF16)],
        compiler_params=_params(2),
        name="out_proj",
    )(x, a, sy, g, w_in, w_in, b_gate, b_gate, glu_w, glu_b, w_a, w_b, w_out)


def kernel(x_prompt, x_sample, state_pool, state_ssm_re, state_ssm_im, meta_tokens, norm_ffn1, ffn1_w_gate, ffn1_w_up, ffn1_w_down, norm_mix, w_in, b_gate, pool_w, pool_scale, ssm_lambda_re, ssm_lambda_im, ssm_log_dt, ssm_b_re, ssm_b_im, ssm_c_re, ssm_c_im, ssm_d, glu_w, glu_b, w_branch_a, w_branch_b, w_out, norm_ffn2, ffn2_w_gate, ffn2_w_up, ffn2_w_down, final_norm):
    l = 0
    bf = lambda w: w.astype(BF16)
    row = lambda v: v.reshape(1, -1).astype(F32)
    x_p = x_prompt.reshape(N_PROMPT_ROWS, D_MODEL)
    x_t = jnp.concatenate([jnp.transpose(x_sample, (1, 0, 2)).reshape(N_SAMPLE_ROWS, D_MODEL),
                           meta_tokens.astype(F32)], axis=0)

    def ffn_both(xp, xt, g, wg, wu, wd, final):
        op, *w_bf = _ffn(xp, g, wg, wu, wd, fg, TM_FFN, TF, final, emit_bf16=True)
        return op, _ffn(xt, g, *w_bf, fg, N_TAIL_ROWS, TF_TAIL, final)

    fg = row(final_norm)
    w_in_b = bf(w_in[l])
    x1 = ffn_both(x_p, x_t, row(norm_ffn1[l]), ffn1_w_gate[l], ffn1_w_up[l], ffn1_w_down[l], False)

    pool_wb, pool_sc = bf(pool_w[l]), row(pool_scale[l])
    hist_t = jnp.transpose(state_pool[l], (1, 0, 2))
    u_t = _inproj(x1[1], row(norm_mix[l]), w_in_b, N_TAIL_ROWS)
    a_t = _pool_tail(u_t, hist_t, pool_wb, pool_sc)
    u_p, a_p = _inproj_pool_prompt(x1[0], row(norm_mix[l]), w_in_b, u_t, pool_wb, pool_sc)
    a_out = (a_p, a_t)

    b_t, ct_t, pwl = _s5_tables(ssm_lambda_re[l], ssm_lambda_im[l], ssm_log_dt[l], ssm_b_re[l],
                                ssm_b_im[l], ssm_c_re[l], ssm_c_im[l])
    sy_p, sy_t, hp_re, hp_im, hs_re, hs_im = _s5(
        u_p, u_t, state_ssm_re[l].reshape(DEC_BATCH, -1), state_ssm_im[l].reshape(DEC_BATCH, -1),
        b_t, ct_t, pwl, row(ssm_d[l]))

    mix_w = (bf(glu_w[l]), row(glu_b[l]), bf(w_branch_a[l]), bf(w_branch_b[l]), bf(w_out[l]))
    x2 = [_out_proj(x, a, sy, row(norm_mix[l]), w_in_b, row(b_gate[l]), *mix_w, tm)
          for x, a, sy, tm in zip(x1, a_out, (sy_p, sy_t), (TM, N_TAIL_ROWS))]
    y_p, y_t = ffn_both(*x2, row(norm_ffn2[l]), ffn2_w_gate[l], ffn2_w_up[l], ffn2_w_down[l], True)

    y_prompt = y_p.reshape(BATCH, SEQ, D_MODEL)
    y_sample = jnp.transpose(y_t[:N_SAMPLE_ROWS].reshape(DEC_SEQ, DEC_BATCH, D_MODEL), (1, 0, 2))
    pool_p = jnp.stack([u_p[(b + 1) * SEQ - POOL_HIST:(b + 1) * SEQ, :W_POOL] for b in range(BATCH)])[None]
    u_pool_s = jnp.transpose(u_t[:N_SAMPLE_ROWS, :W_POOL].reshape(DEC_SEQ, DEC_BATCH, W_POOL), (1, 0, 2))
    pool_s = jnp.concatenate([state_pool[l][:, DEC_SEQ:], u_pool_s], axis=1)[None]
    shp_p = (1, BATCH, N_GROUPS, SSM_STATE)
    shp_s = (1, DEC_BATCH, N_GROUPS, SSM_STATE)
    return (y_prompt, y_sample, pool_p, pool_s,
            hp_re.reshape(shp_p), hp_im.reshape(shp_p), hs_re.reshape(shp_s), hs_im.reshape(shp_s))
```

```python
import functools

import jax
import jax.numpy as jnp
from jax import lax
from jax.experimental import pallas as pl
from jax.experimental.pallas import tpu as pltpu

F32 = jnp.float32
BF16 = jnp.bfloat16

D_MODEL = 2048
BATCH = 4
SEQ = 2048
DEC_BATCH = 128
DEC_SEQ = 4
N_META = 16
D_FF = 5632
W_POOL = 1024
W_SSM = 1024
POOL_WINDOWS = (2, 4, 8, 16)
POOL_GW = 256
POOL_HIST = 15
SSM_GS = 16
N_GROUPS = 64
SSM_STATE = 64
RMS_EPS = 1e-6

N_PROMPT_ROWS = BATCH * SEQ
N_SAMPLE_ROWS = DEC_BATCH * DEC_SEQ
N_TAIL_ROWS = N_SAMPLE_ROWS + N_META
TM_FFN = 1024
TM = 512
TF = 256
TF_TAIL = 512
TN_OUT = 512

LANES = 128
N_SLABS = W_SSM // LANES
SLAB_STATE = (LANES // SSM_GS) * SSM_STATE
CHUNK = 8
N_META_CHUNKS = N_META // CHUNK
N_CHUNKS = (SEQ + N_META) // CHUNK
FOLD_ROW0 = 8
SEQ_STRIDE = FOLD_ROW0 + SEQ // CHUNK
FOLD_ROWS = BATCH * SEQ_STRIDE

V7X_VMEM_BYTES = 64 * 1024 * 1024
VMEM_LIMIT = V7X_VMEM_BYTES - 4 * 1024 * 1024


def _rms(x, g):
    r = lax.rsqrt(jnp.mean(x * x, axis=-1, keepdims=True) + RMS_EPS)
    return x * r * g


def _params(n_axes):
    return pltpu.CompilerParams(dimension_semantics=("arbitrary",) * n_axes, vmem_limit_bytes=VMEM_LIMIT)


def _ffn_kernel(x_ref, g_ref, wg_ref, wu_ref, wd_ref, fg_ref, o_ref, *rest, final_norm, emit_bf16):
    xn_ref = rest[-1]
    j = pl.program_id(1)

    @pl.when(j == 0)
    def _():
        x = x_ref[...]
        xn_ref[...] = _rms(x, g_ref[...]).astype(BF16)
        o_ref[...] = x

    xn = xn_ref[...]
    wg = wg_ref[...].astype(BF16)
    wu = wu_ref[...].astype(BF16)
    wd = wd_ref[...].astype(BF16)
    if emit_bf16:
        wgb_ref, wub_ref, wdb_ref = rest[:3]
        wgb_ref[...] = wg
        wub_ref[...] = wu
        wdb_ref[...] = wd
    gate = jnp.dot(xn, wg, preferred_element_type=F32)
    up = jnp.dot(xn, wu, preferred_element_type=F32)
    h = (gate * jax.nn.sigmoid(gate) * up * 0.5).astype(BF16)
    o_ref[...] += jnp.dot(h, wd, preferred_element_type=F32)

    if final_norm:
        @pl.when(j == pl.num_programs(1) - 1)
        def _():
            o_ref[...] = _rms(o_ref[...], fg_ref[...])


def _ffn(x, g, wg, wu, wd, fg, tm, tf, final_norm, emit_bf16=False):
    rows = x.shape[0]
    n_steps = D_FF // tf
    row_spec = pl.BlockSpec((tm, D_MODEL), lambda i, j: (i, 0))
    out_specs, out_shape = row_spec, jax.ShapeDtypeStruct((rows, D_MODEL), F32)
    if emit_bf16:
        once = lambda i, j: jnp.where(i == 0, j, n_steps - 1)
        out_specs = [row_spec,
                     pl.BlockSpec((D_MODEL, tf), lambda i, j: (0, once(i, j))),
                     pl.BlockSpec((D_MODEL, tf), lambda i, j: (0, once(i, j))),
                     pl.BlockSpec((tf, D_MODEL), lambda i, j: (once(i, j), 0))]
        out_shape = [out_shape] + [jax.ShapeDtypeStruct(w.shape, BF16) for w in (wg, wu, wd)]
    return pl.pallas_call(
        functools.partial(_ffn_kernel, final_norm=final_norm, emit_bf16=emit_bf16),
        grid=(rows // tm, n_steps),
        in_specs=[
            row_spec,
            pl.BlockSpec((1, D_MODEL), lambda i, j: (0, 0)),
            pl.BlockSpec((D_MODEL, tf), lambda i, j: (0, j)),
            pl.BlockSpec((D_MODEL, tf), lambda i, j: (0, j)),
            pl.BlockSpec((tf, D_MODEL), lambda i, j: (j, 0)),
            pl.BlockSpec((1, D_MODEL), lambda i, j: (0, 0)),
        ],
        out_specs=out_specs,
        out_shape=out_shape,
        scratch_shapes=[pltpu.VMEM((tm, D_MODEL), BF16)],
        compiler_params=_params(2),
        name="ffn_final" if final_norm else "ffn",
    )(x, g, wg, wu, wd, fg)


def _inproj_kernel(x_ref, g_ref, w_ref, o_ref, xn_ref):
    xn = _rms(x_ref[...], g_ref[...]).astype(BF16)
    xn_ref[...] = xn
    o_ref[...] = jnp.dot(xn, w_ref[...], preferred_element_type=F32)


def _inproj(x, g, w_in, tm):
    rows = x.shape[0]
    n = W_POOL + W_SSM
    return pl.pallas_call(
        _inproj_kernel,
        grid=(rows // tm,),
        in_specs=[
            pl.BlockSpec((tm, D_MODEL), lambda i: (i, 0)),
            pl.BlockSpec((1, D_MODEL), lambda i: (0, 0)),
            pl.BlockSpec((D_MODEL, n), lambda i: (0, 0)),
        ],
        out_specs=[pl.BlockSpec((tm, n), lambda i: (i, 0)), pl.BlockSpec((tm, D_MODEL), lambda i: (i, 0))],
        out_shape=[jax.ShapeDtypeStruct((rows, n), F32), jax.ShapeDtypeStruct((rows, D_MODEL), BF16)],
        compiler_params=_params(1),
        name="inproj",
    )(x, g, w_in)


TILES_PER_SEQ = SEQ // TM
HALO = 16


def _pool_project(mean_ref, u_ref, pw_ref, scale_ref, o_ref):
    for g in range(len(POOL_WINDOWS)):
        cols = slice(g * POOL_GW, (g + 1) * POOL_GW)
        d = (mean_ref[:, cols] - u_ref[:, cols]).astype(BF16)
        y = jnp.dot(d, pw_ref[g], preferred_element_type=F32) * scale_ref[:, cols]
        o_ref[:, cols] = y.astype(BF16)


def _inproj_pool_kernel(x_ref, g_ref, w_ref, meta_ref, pw_ref, scale_ref, u_ref, o_ref, xn_ref, full_ref):
    @pl.when(pl.program_id(0) % TILES_PER_SEQ == 0)
    def _():
        full_ref[0:HALO, :] = meta_ref[...]

    xn = _rms(x_ref[...], g_ref[...]).astype(BF16)
    xn_ref[...] = xn
    u_pool = jnp.dot(xn, w_ref[:, :W_POOL], preferred_element_type=F32)
    u_ref[:, :W_POOL] = u_pool
    full_ref[HALO:HALO + TM, :] = u_pool
    for g, w in enumerate(POOL_WINDOWS):
        cols = slice(g * POOL_GW, (g + 1) * POOL_GW)
        ucols = slice(W_POOL + g * POOL_GW, W_POOL + (g + 1) * POOL_GW)
        u_ref[:, ucols] = jnp.dot(xn, w_ref[:, ucols], preferred_element_type=F32)
        acc = full_ref[HALO:HALO + TM, cols]
        for k in range(1, w):
            acc = acc + full_ref[HALO - k:HALO - k + TM, cols]
        d = (acc * (1.0 / w) - full_ref[HALO:HALO + TM, cols]).astype(BF16)
        y = jnp.dot(d, pw_ref[g], preferred_element_type=F32) * scale_ref[:, cols]
        o_ref[:, cols] = y.astype(BF16)
    full_ref[0:HALO, :] = full_ref[TM:TM + HALO, :]


def _pool_tail_kernel(u_ref, hist_ref, pw_ref, scale_ref, o_ref, mean_ref):
    for g, w in enumerate(POOL_WINDOWS):
        cols = slice(g * POOL_GW, (g + 1) * POOL_GW)
        for t in range(DEC_SEQ):
            acc = None
            for k in range(w):
                p = POOL_HIST + t - k
                if p >= POOL_HIST:
                    q = p - POOL_HIST
                    term = u_ref[q * DEC_BATCH:(q + 1) * DEC_BATCH, cols]
                else:
                    term = hist_ref[p, :, cols]
                acc = term if acc is None else acc + term
            mean_ref[t * DEC_BATCH:(t + 1) * DEC_BATCH, cols] = acc * (1.0 / w)
    mean_ref[N_SAMPLE_ROWS:, :] = u_ref[N_SAMPLE_ROWS:, :]
    _pool_project(mean_ref, u_ref, pw_ref, scale_ref, o_ref)


_PW_SPEC = pl.BlockSpec((len(POOL_WINDOWS), POOL_GW, POOL_GW), lambda i: (0, 0, 0))
_SCALE_SPEC = pl.BlockSpec((1, W_POOL), lambda i: (0, 0))


def _inproj_pool_prompt(x, g, w_in, u_t, pool_w, pool_scale):
    n = W_POOL + W_SSM
    return pl.pallas_call(
        _inproj_pool_kernel,
        grid=(N_PROMPT_ROWS // TM,),
        in_specs=[
            pl.BlockSpec((TM, D_MODEL), lambda i: (i, 0)),
            pl.BlockSpec((1, D_MODEL), lambda i: (0, 0)),
            pl.BlockSpec((D_MODEL, n), lambda i: (0, 0)),
            pl.BlockSpec((N_META, W_POOL), lambda i: (N_SAMPLE_ROWS // N_META, 0)),
            _PW_SPEC, _SCALE_SPEC,
        ],
        out_specs=[pl.BlockSpec((TM, n), lambda i: (i, 0)), pl.BlockSpec((TM, W_POOL), lambda i: (i, 0)),
                   pl.BlockSpec((TM, D_MODEL), lambda i: (i, 0))],
        out_shape=[jax.ShapeDtypeStruct((N_PROMPT_ROWS, n), F32),
                   jax.ShapeDtypeStruct((N_PROMPT_ROWS, W_POOL), BF16),
                   jax.ShapeDtypeStruct((N_PROMPT_ROWS, D_MODEL), BF16)],
        scratch_shapes=[pltpu.VMEM((HALO + TM, W_POOL), F32)],
        compiler_params=_params(1),
        name="inproj_pool",
    )(x, g, w_in, u_t, pool_w, pool_scale)


def _pool_tail(u_t, hist_t, pool_w, pool_scale):
    return pl.pallas_call(
        _pool_tail_kernel,
        grid=(1,),
        in_specs=[
            pl.BlockSpec((N_TAIL_ROWS, W_POOL), lambda i: (0, 0)),
            pl.BlockSpec((POOL_HIST, DEC_BATCH, W_POOL), lambda i: (0, 0, 0)),
            _PW_SPEC, _SCALE_SPEC,
        ],
        out_specs=pl.BlockSpec((N_TAIL_ROWS, W_POOL), lambda i: (0, 0)),
        out_shape=jax.ShapeDtypeStruct((N_TAIL_ROWS, W_POOL), BF16),
        scratch_shapes=[pltpu.VMEM((N_TAIL_ROWS, W_POOL), F32)],
        compiler_params=_params(1),
        name="pool_tail",
    )(u_t, hist_t, pool_w, pool_scale)


def _s5_tables(lam_re, lam_im, log_dt, b_re, b_im, c_re, c_im):
    dt = jnp.exp(log_dt)[:, None]
    k = jnp.arange(CHUNK + 1, dtype=F32)[:, None, None]
    mag = jnp.exp(k * (lam_re * dt)[None])
    ang = k * (lam_im * dt)[None]
    pw_re, pw_im = mag * jnp.cos(ang), mag * jnp.sin(ang)
    lb_re, lb_im = pw_re[1], pw_im[1]
    den = lam_re * lam_re + lam_im * lam_im
    q_re = ((lb_re - 1.0) * lam_re + lb_im * lam_im) / den
    q_im = (lb_im * lam_re - (lb_re - 1.0) * lam_im) / den
    bt_re, bt_im = jnp.transpose(b_re, (0, 2, 1)), jnp.transpose(b_im, (0, 2, 1))
    bb_re = q_re[:, None, :] * bt_re - q_im[:, None, :] * bt_im
    bb_im = q_re[:, None, :] * bt_im + q_im[:, None, :] * bt_re
    slab = lambda t: t.reshape(N_SLABS, LANES, SSM_STATE)
    b_c = jnp.concatenate([slab(bb_re), slab(bb_im)], axis=2)
    c_c = jnp.concatenate([slab(c_re), slab(c_im)], axis=2)
    pr = pw_re.reshape(CHUNK + 1, N_SLABS, SLAB_STATE)
    pi = pw_im.reshape(CHUNK + 1, N_SLABS, SLAB_STATE)
    pwl = jnp.concatenate([jnp.transpose(pr, (1, 0, 2)), jnp.transpose(pi, (1, 0, 2))], axis=2)
    return b_c, c_c, pwl


_NT = (((1,), (1,)), ((), ()))


def _block_diag(x):
    tiled = jnp.concatenate([x] * (LANES // SSM_GS), axis=1)
    shift = lambda n: n.bit_length() - 1
    row_group = lax.shift_right_logical(lax.broadcasted_iota(jnp.int32, tiled.shape, 0), shift(SSM_GS))
    lane_group = lax.shift_right_logical(lax.broadcasted_iota(jnp.int32, tiled.shape, 1), shift(SSM_STATE))
    return jnp.where(row_group == lane_group, tiled, 0.0)


def _s5_kernel(up_ref, um_ref, us_ref, h0re_ref, h0im_ref, b_ref, ct_ref, pwl_ref, d_ref,
               yp_ref, yt_ref, hpre_ref, hpim_ref, hsre_ref, hsim_ref,
               l_ref, f_ref, m_ref, et_ref, vre_ref, vim_ref, yf_ref, hs_ref):
    ns = SLAB_STATE
    b_re = _block_diag(b_ref[0][:, :SSM_STATE])
    b_im = _block_diag(b_ref[0][:, SSM_STATE:])
    ct_re = _block_diag(ct_ref[0][:, :SSM_STATE])
    ct_im = _block_diag(ct_ref[0][:, SSM_STATE:])
    ct_neg = jnp.concatenate([ct_re, -ct_im], axis=1).astype(BF16)

    for b in range(BATCH):
        r0 = b * SEQ_STRIDE + FOLD_ROW0
        l_ref[r0 - FOLD_ROW0:r0, :] = jnp.zeros((FOLD_ROW0, CHUNK * LANES), F32)
        for s in range(CHUNK):
            cols = slice(s * LANES, (s + 1) * LANES)
            l_ref[r0:r0 + SEQ // CHUNK, cols] = up_ref[pl.ds(b * SEQ + s, SEQ // CHUNK, stride=CHUNK), :]
            for m in range(N_META_CHUNKS):
                r = r0 - N_META_CHUNKS + m
                l_ref[r:r + 1, cols] = um_ref[m * CHUNK + s:m * CHUNK + s + 1, :]
    lb = l_ref[...].astype(BF16)

    for s in range(CHUNK):
        p = pwl_ref[0, CHUNK - 1 - s:CHUNK - s, :]
        p_re, p_im = p[:, :ns], p[:, ns:]
        rows = slice(s * LANES, (s + 1) * LANES)
        f_ref[rows, :ns] = (b_re * p_re - b_im * p_im).astype(BF16)
        f_ref[rows, ns:] = (b_re * p_im + b_im * p_re).astype(BF16)
    v = jnp.dot(lb, f_ref[...], preferred_element_type=F32)
    nk = ns // LANES
    for k in range(nk):
        vre_ref[k] = v[:, k * LANES:(k + 1) * LANES]
        vim_ref[k] = v[:, ns + k * LANES:ns + (k + 1) * LANES]

    a = pwl_ref[0, CHUNK:CHUNK + 1, :]
    a_re = [jnp.broadcast_to(a[:, k * LANES:(k + 1) * LANES], (BATCH, LANES)) for k in range(nk)]
    a_im = [jnp.broadcast_to(a[:, ns + k * LANES:ns + (k + 1) * LANES], (BATCH, LANES)) for k in range(nk)]
    h_re = [jnp.zeros((BATCH, LANES), F32) for _ in range(nk)]
    h_im = [jnp.zeros((BATCH, LANES), F32) for _ in range(nk)]
    for j in range(N_CHUNKS):
        rows = pl.ds(FOLD_ROW0 - N_META_CHUNKS + j, BATCH, stride=SEQ_STRIDE)
        for k in range(nk):
            v_re = vre_ref[k, rows, :]
            v_im = vim_ref[k, rows, :]
            vre_ref[k, rows, :] = h_re[k]
            vim_ref[k, rows, :] = h_im[k]
            h_re[k], h_im[k] = (a_re[k] * h_re[k] - a_im[k] * h_im[k] + v_re,
                                a_re[k] * h_im[k] + a_im[k] * h_re[k] + v_im)
    for k in range(nk):
        hpre_ref[:, k * LANES:(k + 1) * LANES] = h_re[k]
        hpim_ref[:, k * LANES:(k + 1) * LANES] = h_im[k]

    kb_all = lax.dot_general(f_ref[...], ct_neg, _NT, preferred_element_type=F32)
    ri = lax.broadcasted_iota(jnp.int32, (LANES, LANES), 0)
    ci = lax.broadcasted_iota(jnp.int32, (LANES, LANES), 1)
    d_diag = jnp.where(ri == ci, jnp.broadcast_to(d_ref[...], (LANES, LANES)), 0.0)
    kb = [None] * CHUNK
    for s in range(CHUNK):
        blk = kb_all[s * LANES:(s + 1) * LANES, :]
        if s == CHUNK - 1:
            blk = blk + d_diag
        kb[CHUNK - 1 - s] = blk.astype(BF16)

    half = CHUNK // 2
    for s in range(CHUNK):
        for t in range(half, CHUNK):
            blk = kb[t - s] if t >= s else jnp.zeros((LANES, LANES), BF16)
            m_ref[s * LANES:(s + 1) * LANES, (t - half) * LANES:(t - half + 1) * LANES] = blk
    y_left = jnp.dot(lb[:, :half * LANES], m_ref[half * LANES:, :], preferred_element_type=F32)
    y_right = jnp.dot(lb, m_ref[...], preferred_element_type=F32)

    for t in range(CHUNK):
        p = pwl_ref[0, t + 1:t + 2, :]
        p_re, p_im = p[:, :ns], p[:, ns:]
        rows = slice(t * LANES, (t + 1) * LANES)
        et_ref[rows, :ns] = (ct_re * p_re - ct_im * p_im).astype(BF16)
        et_ref[rows, ns:] = (-(ct_re * p_im + ct_im * p_re)).astype(BF16)
    h_in = jnp.concatenate([vre_ref[k] for k in range(nk)] + [vim_ref[k] for k in range(nk)],
                           axis=1).astype(BF16)
    ys = lax.dot_general(h_in, et_ref[...], _NT, preferred_element_type=F32)
    yf_ref[:, :half * LANES] = y_left + ys[:, :half * LANES]
    yf_ref[:, half * LANES:] = y_right + ys[:, half * LANES:]
    for b in range(BATCH):
        r0 = b * SEQ_STRIDE + FOLD_ROW0
        for t in range(CHUNK):
            yp_ref[pl.ds(b * SEQ + t, SEQ // CHUNK, stride=CHUNK), :] = (
                yf_ref[r0:r0 + SEQ // CHUNK, t * LANES:(t + 1) * LANES])

    u = us_ref[...]
    b_all = jnp.concatenate([b_re, b_im], axis=1).astype(BF16)
    bu = jnp.dot(u.astype(BF16), b_all, preferred_element_type=F32)
    l1 = pwl_ref[0, 1:2, :]
    l_re, l_im = l1[:, :ns], l1[:, ns:]
    g_re = h0re_ref[...]
    g_im = h0im_ref[...]
    for t in range(DEC_SEQ):
        rows = slice(t * DEC_BATCH, (t + 1) * DEC_BATCH)
        g_re, g_im = (l_re * g_re - l_im * g_im + bu[rows, :ns],
                      l_re * g_im + l_im * g_re + bu[rows, ns:])
        hs_ref[rows, :ns] = g_re.astype(BF16)
        hs_ref[rows, ns:] = g_im.astype(BF16)
    hsre_ref[...] = g_re
    hsim_ref[...] = g_im
    yt_ref[0:N_SAMPLE_ROWS, :] = (lax.dot_general(hs_ref[...], ct_neg, _NT, preferred_element_type=F32)
                                  + d_ref[...] * u)
    yt_ref[N_SAMPLE_ROWS:, :] = jnp.zeros((N_META, LANES), F32)


def _s5(u_p, u_t, h0_re, h0_im, b_t, ct_t, pwl, d_skip):
    c0 = W_POOL // LANES
    ns = SLAB_STATE
    state = lambda rows: pl.BlockSpec((rows, ns), lambda i: (0, i))
    return pl.pallas_call(
        _s5_kernel,
        grid=(N_SLABS,),
        in_specs=[
            pl.BlockSpec((N_PROMPT_ROWS, LANES), lambda i: (0, c0 + i)),
            pl.BlockSpec((N_META, LANES), lambda i: (N_SAMPLE_ROWS // N_META, c0 + i)),
            pl.BlockSpec((N_SAMPLE_ROWS, LANES), lambda i: (0, c0 + i)),
            state(DEC_BATCH), state(DEC_BATCH),
            pl.BlockSpec((1, LANES, 2 * SSM_STATE), lambda i: (i, 0, 0)),
            pl.BlockSpec((1, LANES, 2 * SSM_STATE), lambda i: (i, 0, 0)),
            pl.BlockSpec((1, CHUNK + 1, 2 * ns), lambda i: (i, 0, 0)),
            pl.BlockSpec((1, LANES), lambda i: (0, i)),
        ],
        out_specs=[
            pl.BlockSpec((N_PROMPT_ROWS, LANES), lambda i: (0, i)),
            pl.BlockSpec((N_TAIL_ROWS, LANES), lambda i: (0, i)),
            state(BATCH), state(BATCH), state(DEC_BATCH), state(DEC_BATCH),
        ],
        out_shape=[
            jax.ShapeDtypeStruct((N_PROMPT_ROWS, W_SSM), F32),
            jax.ShapeDtypeStruct((N_TAIL_ROWS, W_SSM), F32),
            jax.ShapeDtypeStruct((BATCH, N_GROUPS * SSM_STATE), F32),
            jax.ShapeDtypeStruct((BATCH, N_GROUPS * SSM_STATE), F32),
            jax.ShapeDtypeStruct((DEC_BATCH, N_GROUPS * SSM_STATE), F32),
            jax.ShapeDtypeStruct((DEC_BATCH, N_GROUPS * SSM_STATE), F32),
        ],
        scratch_shapes=[
            pltpu.VMEM((FOLD_ROWS, CHUNK * LANES), F32),
            pltpu.VMEM((CHUNK * LANES, 2 * ns), BF16),
            pltpu.VMEM((CHUNK * LANES, CHUNK * LANES // 2), BF16),
            pltpu.VMEM((CHUNK * LANES, 2 * ns), BF16),
            pltpu.VMEM((ns // LANES, FOLD_ROWS, LANES), F32),
            pltpu.VMEM((ns // LANES, FOLD_ROWS, LANES), F32),
            pltpu.VMEM((FOLD_ROWS, CHUNK * LANES), F32),
            pltpu.VMEM((N_SAMPLE_ROWS, 2 * ns), BF16),
        ],
        compiler_params=_params(1),
        name="s5",
    )(u_p, u_t, u_t, h0_re, h0_im, b_t, ct_t, pwl, d_skip)


def _out_kernel(x_ref, xn_ref, a_ref, sy_ref, wga_ref, wgb_ref, bga_ref, bgb_ref, gluw_ref, glub_ref,
                wa_ref, wb_ref, wo_ref, o_ref, s_ref):
    j = pl.program_id(1)

    @pl.when(j == 0)
    def _():
        o_ref[...] = x_ref[...]
        s = jax.nn.gelu(sy_ref[...])
        z = jnp.dot(s.astype(BF16), gluw_ref[...], preferred_element_type=F32) + glub_ref[...]
        s_ref[...] = (s * jax.nn.sigmoid(z)).astype(BF16)

    xn = xn_ref[...]
    ga = jax.nn.sigmoid(jnp.dot(xn, wga_ref[...], preferred_element_type=F32) + bga_ref[...])
    gb = jax.nn.sigmoid(jnp.dot(xn, wgb_ref[...], preferred_element_type=F32) + bgb_ref[...])
    merged = (ga * jnp.dot(a_ref[...], wa_ref[...], preferred_element_type=F32)
              + gb * jnp.dot(s_ref[...], wb_ref[...], preferred_element_type=F32))
    o_ref[...] += jnp.dot(merged.astype(BF16), wo_ref[...], preferred_element_type=F32)


def _out_proj(x, xn, a, sy, w_in, b_gate, glu_w, glu_b, w_a, w_b, w_out, tm):
    rows = x.shape[0]
    nb = D_MODEL // TN_OUT
    g0 = (W_POOL + W_SSM) // TN_OUT
    return pl.pallas_call(
        _out_kernel,
        grid=(rows // tm, nb),
        in_specs=[
            pl.BlockSpec((tm, D_MODEL), lambda i, j: (i, 0)),
            pl.BlockSpec((tm, D_MODEL), lambda i, j: (i, 0)),
            pl.BlockSpec((tm, W_POOL), lambda i, j: (i, 0)),
            pl.BlockSpec((tm, W_SSM), lambda i, j: (i, 0)),
            pl.BlockSpec((D_MODEL, TN_OUT), lambda i, j: (0, g0 + j)),
            pl.BlockSpec((D_MODEL, TN_OUT), lambda i, j: (0, g0 + nb + j)),
            pl.BlockSpec((1, TN_OUT), lambda i, j: (0, j)),
            pl.BlockSpec((1, TN_OUT), lambda i, j: (0, nb + j)),
            pl.BlockSpec((W_SSM, W_SSM), lambda i, j: (0, 0)),
            pl.BlockSpec((1, W_SSM), lambda i, j: (0, 0)),
            pl.BlockSpec((W_POOL, TN_OUT), lambda i, j: (0, j)),
            pl.BlockSpec((W_SSM, TN_OUT), lambda i, j: (0, j)),
            pl.BlockSpec((TN_OUT, D_MODEL), lambda i, j: (j, 0)),
        ],
        out_specs=pl.BlockSpec((tm, D_MODEL), lambda i, j: (i, 0)),
        out_shape=jax.ShapeDtypeStruct((rows, D_MODEL), F32),
        scratch_shapes=[pltpu.VMEM((tm, W_SSM), BF16)],
        compiler_params=_params(2),
        name="out_proj",
    )(x, xn, a, sy, w_in, w_in, b_gate, b_gate, glu_w, glu_b, w_a, w_b, w_out)


def kernel(x_prompt, x_sample, state_pool, state_ssm_re, state_ssm_im, meta_tokens, norm_ffn1, ffn1_w_gate, ffn1_w_up, ffn1_w_down, norm_mix, w_in, b_gate, pool_w, pool_scale, ssm_lambda_re, ssm_lambda_im, ssm_log_dt, ssm_b_re, ssm_b_im, ssm_c_re, ssm_c_im, ssm_d, glu_w, glu_b, w_branch_a, w_branch_b, w_out, norm_ffn2, ffn2_w_gate, ffn2_w_up, ffn2_w_down, final_norm):
    l = 0
    bf = lambda w: w.astype(BF16)
    row = lambda v: v.reshape(1, -1).astype(F32)
    x_p = x_prompt.reshape(N_PROMPT_ROWS, D_MODEL)
    x_t = jnp.concatenate([jnp.transpose(x_sample, (1, 0, 2)).reshape(N_SAMPLE_ROWS, D_MODEL),
                           meta_tokens.astype(F32)], axis=0)

    def ffn_both(xp, xt, g, wg, wu, wd, final):
        op, *w_bf = _ffn(xp, g, wg, wu, wd, fg, TM_FFN, TF, final, emit_bf16=True)
        return op, _ffn(xt, g, *w_bf, fg, N_TAIL_ROWS, TF_TAIL, final)

    fg = row(final_norm)
    w_in_b = bf(w_in[l])
    x1 = ffn_both(x_p, x_t, row(norm_ffn1[l]), ffn1_w_gate[l], ffn1_w_up[l], ffn1_w_down[l], False)

    pool_wb, pool_sc = bf(pool_w[l]), row(pool_scale[l])
    hist_t = jnp.transpose(state_pool[l], (1, 0, 2))
    u_t, xn_t = _inproj(x1[1], row(norm_mix[l]), w_in_b, N_TAIL_ROWS)
    a_t = _pool_tail(u_t, hist_t, pool_wb, pool_sc)
    u_p, a_p, xn_p = _inproj_pool_prompt(x1[0], row(norm_mix[l]), w_in_b, u_t, pool_wb, pool_sc)

    b_t, ct_t, pwl = _s5_tables(ssm_lambda_re[l], ssm_lambda_im[l], ssm_log_dt[l], ssm_b_re[l],
                                ssm_b_im[l], ssm_c_re[l], ssm_c_im[l])
    sy_p, sy_t, hp_re, hp_im, hs_re, hs_im = _s5(
        u_p, u_t, state_ssm_re[l].reshape(DEC_BATCH, -1), state_ssm_im[l].reshape(DEC_BATCH, -1),
        b_t, ct_t, pwl, row(ssm_d[l]))

    mix_w = (bf(glu_w[l]), row(glu_b[l]), bf(w_branch_a[l]), bf(w_branch_b[l]), bf(w_out[l]))
    x2 = [_out_proj(x, xn, a, sy, w_in_b, row(b_gate[l]), *mix_w, tm)
          for x, xn, a, sy, tm in zip(x1, (xn_p, xn_t), (a_p, a_t), (sy_p, sy_t), (TM, N_TAIL_ROWS))]
    y_p, y_t = ffn_both(*x2, row(norm_ffn2[l]), ffn2_w_gate[l], ffn2_w_up[l], ffn2_w_down[l], True)

    y_prompt = y_p.reshape(BATCH, SEQ, D_MODEL)
    y_sample = jnp.transpose(y_t[:N_SAMPLE_ROWS].reshape(DEC_SEQ, DEC_BATCH, D_MODEL), (1, 0, 2))
    pool_p = jnp.stack([u_p[(b + 1) * SEQ - POOL_HIST:(b + 1) * SEQ, :W_POOL] for b in range(BATCH)])[None]
    u_pool_s = jnp.transpose(u_t[:N_SAMPLE_ROWS, :W_POOL].reshape(DEC_SEQ, DEC_BATCH, W_POOL), (1, 0, 2))
    pool_s = jnp.concatenate([state_pool[l][:, DEC_SEQ:], u_pool_s], axis=1)[None]
    shp_p = (1, BATCH, N_GROUPS, SSM_STATE)
    shp_s = (1, DEC_BATCH, N_GROUPS, SSM_STATE)
    return (y_prompt, y_sample, pool_p, pool_s,
            hp_re.reshape(shp_p), hp_im.reshape(shp_p), hs_re.reshape(shp_s), hs_im.reshape(shp_s))
```

```python
import functools

import jax
import jax.numpy as jnp
from jax import lax
from jax.experimental import pallas as pl
from jax.experimental.pallas import tpu as pltpu

F32 = jnp.float32
BF16 = jnp.bfloat16

D_MODEL = 2048
BATCH = 4
SEQ = 2048
DEC_BATCH = 128
DEC_SEQ = 4
N_META = 16
D_FF = 5632
W_POOL = 1024
W_SSM = 1024
POOL_WINDOWS = (2, 4, 8, 16)
POOL_GW = 256
POOL_HIST = 15
SSM_GS = 16
N_GROUPS = 64
SSM_STATE = 64
RMS_EPS = 1e-6

N_PROMPT_ROWS = BATCH * SEQ
N_SAMPLE_ROWS = DEC_BATCH * DEC_SEQ
N_TAIL_ROWS = N_SAMPLE_ROWS + N_META
TM_FFN = 1024
TM = 512
TF = 256
TF_TAIL = 512
TN_OUT = 512

LANES = 128
N_SLABS = W_SSM // LANES
SLAB_STATE = (LANES // SSM_GS) * SSM_STATE
CHUNK = 8
N_META_CHUNKS = N_META // CHUNK
N_CHUNKS = (SEQ + N_META) // CHUNK
FOLD_ROW0 = 8
SEQ_STRIDE = FOLD_ROW0 + SEQ // CHUNK
FOLD_ROWS = BATCH * SEQ_STRIDE

V7X_VMEM_BYTES = 64 * 1024 * 1024
VMEM_LIMIT = V7X_VMEM_BYTES - 4 * 1024 * 1024


def _rms(x, g):
    r = lax.rsqrt(jnp.mean(x * x, axis=-1, keepdims=True) + RMS_EPS)
    return x * r * g


def _serpentine(i, j, n, forward_tile=-1):
    return jnp.where((i % 2 == 0) | (i == forward_tile), j, n - 1 - j)


def _params(n_axes):
    return pltpu.CompilerParams(dimension_semantics=("arbitrary",) * n_axes, vmem_limit_bytes=VMEM_LIMIT)


def _ffn_kernel(x_ref, g_ref, wg_ref, wu_ref, wd_ref, fg_ref, o_ref, *rest, final_norm, emit_bf16):
    xn_ref = rest[-1]
    j = pl.program_id(1)

    @pl.when(j == 0)
    def _():
        x = x_ref[...]
        xn_ref[...] = _rms(x, g_ref[...]).astype(BF16)
        o_ref[...] = x

    xn = xn_ref[...]
    wg = wg_ref[...].astype(BF16)
    wu = wu_ref[...].astype(BF16)
    wd = wd_ref[...].astype(BF16)
    if emit_bf16:
        wgb_ref, wub_ref, wdb_ref = rest[:3]
        wgb_ref[...] = wg
        wub_ref[...] = wu
        wdb_ref[...] = wd
    gate = jnp.dot(xn, wg, preferred_element_type=F32)
    up = jnp.dot(xn, wu, preferred_element_type=F32)
    h = (gate * jax.nn.sigmoid(gate) * up * 0.5).astype(BF16)
    o_ref[...] += jnp.dot(h, wd, preferred_element_type=F32)

    if final_norm:
        @pl.when(j == pl.num_programs(1) - 1)
        def _():
            o_ref[...] = _rms(o_ref[...], fg_ref[...])


def _ffn(x, g, wg, wu, wd, fg, tm, tf, final_norm, emit_bf16=False):
    rows = x.shape[0]
    n_steps = D_FF // tf
    row_spec = pl.BlockSpec((tm, D_MODEL), lambda i, j: (i, 0))
    out_specs, out_shape = row_spec, jax.ShapeDtypeStruct((rows, D_MODEL), F32)
    blk = lambda i, j: _serpentine(i, j, n_steps, forward_tile=rows // tm - 1)
    if emit_bf16:
        once = lambda i, j: jnp.where(i == 0, j, n_steps - 1)
        out_specs = [row_spec,
                     pl.BlockSpec((D_MODEL, tf), lambda i, j: (0, once(i, j))),
                     pl.BlockSpec((D_MODEL, tf), lambda i, j: (0, once(i, j))),
                     pl.BlockSpec((tf, D_MODEL), lambda i, j: (once(i, j), 0))]
        out_shape = [out_shape] + [jax.ShapeDtypeStruct(w.shape, BF16) for w in (wg, wu, wd)]
    return pl.pallas_call(
        functools.partial(_ffn_kernel, final_norm=final_norm, emit_bf16=emit_bf16),
        grid=(rows // tm, n_steps),
        in_specs=[
            row_spec,
            pl.BlockSpec((1, D_MODEL), lambda i, j: (0, 0)),
            pl.BlockSpec((D_MODEL, tf), lambda i, j: (0, blk(i, j))),
            pl.BlockSpec((D_MODEL, tf), lambda i, j: (0, blk(i, j))),
            pl.BlockSpec((tf, D_MODEL), lambda i, j: (blk(i, j), 0)),
            pl.BlockSpec((1, D_MODEL), lambda i, j: (0, 0)),
        ],
        out_specs=out_specs,
        out_shape=out_shape,
        scratch_shapes=[pltpu.VMEM((tm, D_MODEL), BF16)],
        compiler_params=_params(2),
        name="ffn_final" if final_norm else "ffn",
    )(x, g, wg, wu, wd, fg)


def _inproj_kernel(x_ref, g_ref, w_ref, o_ref):
    xn = _rms(x_ref[...], g_ref[...]).astype(BF16)
    o_ref[...] = jnp.dot(xn, w_ref[...], preferred_element_type=F32)


def _inproj(x, g, w_in, tm):
    rows = x.shape[0]
    n = W_POOL + W_SSM
    return pl.pallas_call(
        _inproj_kernel,
        grid=(rows // tm,),
        in_specs=[
            pl.BlockSpec((tm, D_MODEL), lambda i: (i, 0)),
            pl.BlockSpec((1, D_MODEL), lambda i: (0, 0)),
            pl.BlockSpec((D_MODEL, n), lambda i: (0, 0)),
        ],
        out_specs=pl.BlockSpec((tm, n), lambda i: (i, 0)),
        out_shape=jax.ShapeDtypeStruct((rows, n), F32),
        compiler_params=_params(1),
        name="inproj",
    )(x, g, w_in)


TILES_PER_SEQ = SEQ // TM
HALO = 16


def _pool_project(mean_ref, u_ref, pw_ref, scale_ref, o_ref):
    for g in range(len(POOL_WINDOWS)):
        cols = slice(g * POOL_GW, (g + 1) * POOL_GW)
        d = (mean_ref[:, cols] - u_ref[:, cols]).astype(BF16)
        y = jnp.dot(d, pw_ref[g], preferred_element_type=F32) * scale_ref[:, cols]
        o_ref[:, cols] = y.astype(BF16)


def _inproj_pool_kernel(x_ref, g_ref, w_ref, meta_ref, pw_ref, scale_ref, u_ref, o_ref, full_ref):
    @pl.when(pl.program_id(0) % TILES_PER_SEQ == 0)
    def _():
        full_ref[0:HALO, :] = meta_ref[...]

    xn = _rms(x_ref[...], g_ref[...]).astype(BF16)
    u_pool = jnp.dot(xn, w_ref[:, :W_POOL], preferred_element_type=F32)
    u_ref[:, :W_POOL] = u_pool
    full_ref[HALO:HALO + TM, :] = u_pool
    for g, w in enumerate(POOL_WINDOWS):
        cols = slice(g * POOL_GW, (g + 1) * POOL_GW)
        ucols = slice(W_POOL + g * POOL_GW, W_POOL + (g + 1) * POOL_GW)
        u_ref[:, ucols] = jnp.dot(xn, w_ref[:, ucols], preferred_element_type=F32)
        acc = full_ref[HALO:HALO + TM, cols]
        for k in range(1, w):
            acc = acc + full_ref[HALO - k:HALO - k + TM, cols]
        d = (acc * (1.0 / w) - full_ref[HALO:HALO + TM, cols]).astype(BF16)
        y = jnp.dot(d, pw_ref[g], preferred_element_type=F32) * scale_ref[:, cols]
        o_ref[:, cols] = y.astype(BF16)
    full_ref[0:HALO, :] = full_ref[TM:TM + HALO, :]


def _pool_tail_kernel(u_ref, hist_ref, pw_ref, scale_ref, o_ref, mean_ref):
    for g, w in enumerate(POOL_WINDOWS):
        cols = slice(g * POOL_GW, (g + 1) * POOL_GW)
        for t in range(DEC_SEQ):
            acc = None
            for k in range(w):
                p = POOL_HIST + t - k
                if p >= POOL_HIST:
                    q = p - POOL_HIST
                    term = u_ref[q * DEC_BATCH:(q + 1) * DEC_BATCH, cols]
                else:
                    term = hist_ref[p, :, cols]
                acc = term if acc is None else acc + term
            mean_ref[t * DEC_BATCH:(t + 1) * DEC_BATCH, cols] = acc * (1.0 / w)
    mean_ref[N_SAMPLE_ROWS:, :] = u_ref[N_SAMPLE_ROWS:, :]
    _pool_project(mean_ref, u_ref, pw_ref, scale_ref, o_ref)


_PW_SPEC = pl.BlockSpec((len(POOL_WINDOWS), POOL_GW, POOL_GW), lambda i: (0, 0, 0))
_SCALE_SPEC = pl.BlockSpec((1, W_POOL), lambda i: (0, 0))


def _inproj_pool_prompt(x, g, w_in, u_t, pool_w, pool_scale):
    n = W_POOL + W_SSM
    return pl.pallas_call(
        _inproj_pool_kernel,
        grid=(N_PROMPT_ROWS // TM,),
        in_specs=[
            pl.BlockSpec((TM, D_MODEL), lambda i: (i, 0)),
            pl.BlockSpec((1, D_MODEL), lambda i: (0, 0)),
            pl.BlockSpec((D_MODEL, n), lambda i: (0, 0)),
            pl.BlockSpec((N_META, W_POOL), lambda i: (N_SAMPLE_ROWS // N_META, 0)),
            _PW_SPEC, _SCALE_SPEC,
        ],
        out_specs=[pl.BlockSpec((TM, n), lambda i: (i, 0)), pl.BlockSpec((TM, W_POOL), lambda i: (i, 0))],
        out_shape=[jax.ShapeDtypeStruct((N_PROMPT_ROWS, n), F32),
                   jax.ShapeDtypeStruct((N_PROMPT_ROWS, W_POOL), BF16)],
        scratch_shapes=[pltpu.VMEM((HALO + TM, W_POOL), F32)],
        compiler_params=_params(1),
        name="inproj_pool",
    )(x, g, w_in, u_t, pool_w, pool_scale)


def _pool_tail(u_t, hist_t, pool_w, pool_scale):
    return pl.pallas_call(
        _pool_tail_kernel,
        grid=(1,),
        in_specs=[
            pl.BlockSpec((N_TAIL_ROWS, W_POOL), lambda i: (0, 0)),
            pl.BlockSpec((POOL_HIST, DEC_BATCH, W_POOL), lambda i: (0, 0, 0)),
            _PW_SPEC, _SCALE_SPEC,
        ],
        out_specs=pl.BlockSpec((N_TAIL_ROWS, W_POOL), lambda i: (0, 0)),
        out_shape=jax.ShapeDtypeStruct((N_TAIL_ROWS, W_POOL), BF16),
        scratch_shapes=[pltpu.VMEM((N_TAIL_ROWS, W_POOL), F32)],
        compiler_params=_params(1),
        name="pool_tail",
    )(u_t, hist_t, pool_w, pool_scale)


def _s5_tables(lam_re, lam_im, log_dt, b_re, b_im, c_re, c_im):
    dt = jnp.exp(log_dt)[:, None]
    k = jnp.arange(CHUNK + 1, dtype=F32)[:, None, None]
    mag = jnp.exp(k * (lam_re * dt)[None])
    ang = k * (lam_im * dt)[None]
    pw_re, pw_im = mag * jnp.cos(ang), mag * jnp.sin(ang)
    lb_re, lb_im = pw_re[1], pw_im[1]
    den = lam_re * lam_re + lam_im * lam_im
    q_re = ((lb_re - 1.0) * lam_re + lb_im * lam_im) / den
    q_im = (lb_im * lam_re - (lb_re - 1.0) * lam_im) / den
    bt_re, bt_im = jnp.transpose(b_re, (0, 2, 1)), jnp.transpose(b_im, (0, 2, 1))
    bb_re = q_re[:, None, :] * bt_re - q_im[:, None, :] * bt_im
    bb_im = q_re[:, None, :] * bt_im + q_im[:, None, :] * bt_re
    slab = lambda t: t.reshape(N_SLABS, LANES, SSM_STATE)
    b_c = jnp.concatenate([slab(bb_re), slab(bb_im)], axis=2)
    c_c = jnp.concatenate([slab(c_re), slab(c_im)], axis=2)
    pr = pw_re.reshape(CHUNK + 1, N_SLABS, SLAB_STATE)
    pi = pw_im.reshape(CHUNK + 1, N_SLABS, SLAB_STATE)
    pwl = jnp.concatenate([jnp.transpose(pr, (1, 0, 2)), jnp.transpose(pi, (1, 0, 2))], axis=2)
    return b_c, c_c, pwl


_NT = (((1,), (1,)), ((), ()))


def _block_diag(x):
    tiled = jnp.concatenate([x] * (LANES // SSM_GS), axis=1)
    shift = lambda n: n.bit_length() - 1
    row_group = lax.shift_right_logical(lax.broadcasted_iota(jnp.int32, tiled.shape, 0), shift(SSM_GS))
    lane_group = lax.shift_right_logical(lax.broadcasted_iota(jnp.int32, tiled.shape, 1), shift(SSM_STATE))
    return jnp.where(row_group == lane_group, tiled, 0.0)


def _s5_kernel(up_ref, um_ref, us_ref, h0re_ref, h0im_ref, b_ref, ct_ref, pwl_ref, d_ref,
               yp_ref, yt_ref, hpre_ref, hpim_ref, hsre_ref, hsim_ref,
               l_ref, f_ref, m_ref, et_ref, vre_ref, vim_ref, yf_ref, hs_ref):
    ns = SLAB_STATE
    b_re = _block_diag(b_ref[0][:, :SSM_STATE])
    b_im = _block_diag(b_ref[0][:, SSM_STATE:])
    ct_re = _block_diag(ct_ref[0][:, :SSM_STATE])
    ct_im = _block_diag(ct_ref[0][:, SSM_STATE:])
    ct_neg = jnp.concatenate([ct_re, -ct_im], axis=1).astype(BF16)

    for b in range(BATCH):
        r0 = b * SEQ_STRIDE + FOLD_ROW0
        l_ref[r0 - FOLD_ROW0:r0, :] = jnp.zeros((FOLD_ROW0, CHUNK * LANES), F32)
        for s in range(CHUNK):
            cols = slice(s * LANES, (s + 1) * LANES)
            l_ref[r0:r0 + SEQ // CHUNK, cols] = up_ref[pl.ds(b * SEQ + s, SEQ // CHUNK, stride=CHUNK), :]
            for m in range(N_META_CHUNKS):
                r = r0 - N_META_CHUNKS + m
                l_ref[r:r + 1, cols] = um_ref[m * CHUNK + s:m * CHUNK + s + 1, :]
    lb = l_ref[...].astype(BF16)

    for s in range(CHUNK):
        p = pwl_ref[0, CHUNK - 1 - s:CHUNK - s, :]
        p_re, p_im = p[:, :ns], p[:, ns:]
        rows = slice(s * LANES, (s + 1) * LANES)
        f_ref[rows, :ns] = (b_re * p_re - b_im * p_im).astype(BF16)
        f_ref[rows, ns:] = (b_re * p_im + b_im * p_re).astype(BF16)
    v = jnp.dot(lb, f_ref[...], preferred_element_type=F32)
    nk = ns // LANES
    for k in range(nk):
        vre_ref[k] = v[:, k * LANES:(k + 1) * LANES]
        vim_ref[k] = v[:, ns + k * LANES:ns + (k + 1) * LANES]

    a = pwl_ref[0, CHUNK:CHUNK + 1, :]
    a_re = [jnp.broadcast_to(a[:, k * LANES:(k + 1) * LANES], (BATCH, LANES)) for k in range(nk)]
    a_im = [jnp.broadcast_to(a[:, ns + k * LANES:ns + (k + 1) * LANES], (BATCH, LANES)) for k in range(nk)]
    h_re = [jnp.zeros((BATCH, LANES), F32) for _ in range(nk)]
    h_im = [jnp.zeros((BATCH, LANES), F32) for _ in range(nk)]
    for j in range(N_CHUNKS):
        rows = pl.ds(FOLD_ROW0 - N_META_CHUNKS + j, BATCH, stride=SEQ_STRIDE)
        for k in range(nk):
            v_re = vre_ref[k, rows, :]
            v_im = vim_ref[k, rows, :]
            vre_ref[k, rows, :] = h_re[k]
            vim_ref[k, rows, :] = h_im[k]
            h_re[k], h_im[k] = (a_re[k] * h_re[k] - a_im[k] * h_im[k] + v_re,
                                a_re[k] * h_im[k] + a_im[k] * h_re[k] + v_im)
    for k in range(nk):
        hpre_ref[:, k * LANES:(k + 1) * LANES] = h_re[k]
        hpim_ref[:, k * LANES:(k + 1) * LANES] = h_im[k]

    kb_all = lax.dot_general(f_ref[...], ct_neg, _NT, preferred_element_type=F32)
    ri = lax.broadcasted_iota(jnp.int32, (LANES, LANES), 0)
    ci = lax.broadcasted_iota(jnp.int32, (LANES, LANES), 1)
    d_diag = jnp.where(ri == ci, jnp.broadcast_to(d_ref[...], (LANES, LANES)), 0.0)
    kb = [None] * CHUNK
    for s in range(CHUNK):
        blk = kb_all[s * LANES:(s + 1) * LANES, :]
        if s == CHUNK - 1:
            blk = blk + d_diag
        kb[CHUNK - 1 - s] = blk.astype(BF16)

    half = CHUNK // 2
    for s in range(CHUNK):
        for t in range(half, CHUNK):
            blk = kb[t - s] if t >= s else jnp.zeros((LANES, LANES), BF16)
            m_ref[s * LANES:(s + 1) * LANES, (t - half) * LANES:(t - half + 1) * LANES] = blk
    y_left = jnp.dot(lb[:, :half * LANES], m_ref[half * LANES:, :], preferred_element_type=F32)
    y_right = jnp.dot(lb, m_ref[...], preferred_element_type=F32)

    for t in range(CHUNK):
        p = pwl_ref[0, t + 1:t + 2, :]
        p_re, p_im = p[:, :ns], p[:, ns:]
        rows = slice(t * LANES, (t + 1) * LANES)
        et_ref[rows, :ns] = (ct_re * p_re - ct_im * p_im).astype(BF16)
        et_ref[rows, ns:] = (-(ct_re * p_im + ct_im * p_re)).astype(BF16)
    h_in = jnp.concatenate([vre_ref[k] for k in range(nk)] + [vim_ref[k] for k in range(nk)],
                           axis=1).astype(BF16)
    ys = lax.dot_general(h_in, et_ref[...], _NT, preferred_element_type=F32)
    yf_ref[:, :half * LANES] = y_left + ys[:, :half * LANES]
    yf_ref[:, half * LANES:] = y_right + ys[:, half * LANES:]
    for b in range(BATCH):
        r0 = b * SEQ_STRIDE + FOLD_ROW0
        for t in range(CHUNK):
            yp_ref[pl.ds(b * SEQ + t, SEQ // CHUNK, stride=CHUNK), :] = (
                yf_ref[r0:r0 + SEQ // CHUNK, t * LANES:(t + 1) * LANES])

    u = us_ref[...]
    b_all = jnp.concatenate([b_re, b_im], axis=1).astype(BF16)
    bu = jnp.dot(u.astype(BF16), b_all, preferred_element_type=F32)
    l1 = pwl_ref[0, 1:2, :]
    l_re, l_im = l1[:, :ns], l1[:, ns:]
    g_re = h0re_ref[...]
    g_im = h0im_ref[...]
    for t in range(DEC_SEQ):
        rows = slice(t * DEC_BATCH, (t + 1) * DEC_BATCH)
        g_re, g_im = (l_re * g_re - l_im * g_im + bu[rows, :ns],
                      l_re * g_im + l_im * g_re + bu[rows, ns:])
        hs_ref[rows, :ns] = g_re.astype(BF16)
        hs_ref[rows, ns:] = g_im.astype(BF16)
    hsre_ref[...] = g_re
    hsim_ref[...] = g_im
    yt_ref[0:N_SAMPLE_ROWS, :] = (lax.dot_general(hs_ref[...], ct_neg, _NT, preferred_element_type=F32)
                                  + d_ref[...] * u)
    yt_ref[N_SAMPLE_ROWS:, :] = jnp.zeros((N_META, LANES), F32)


def _s5(u_p, u_t, h0_re, h0_im, b_t, ct_t, pwl, d_skip):
    c0 = W_POOL // LANES
    ns = SLAB_STATE
    state = lambda rows: pl.BlockSpec((rows, ns), lambda i: (0, i))
    return pl.pallas_call(
        _s5_kernel,
        grid=(N_SLABS,),
        in_specs=[
            pl.BlockSpec((N_PROMPT_ROWS, LANES), lambda i: (0, c0 + i)),
            pl.BlockSpec((N_META, LANES), lambda i: (N_SAMPLE_ROWS // N_META, c0 + i)),
            pl.BlockSpec((N_SAMPLE_ROWS, LANES), lambda i: (0, c0 + i)),
            state(DEC_BATCH), state(DEC_BATCH),
            pl.BlockSpec((1, LANES, 2 * SSM_STATE), lambda i: (i, 0, 0)),
            pl.BlockSpec((1, LANES, 2 * SSM_STATE), lambda i: (i, 0, 0)),
            pl.BlockSpec((1, CHUNK + 1, 2 * ns), lambda i: (i, 0, 0)),
            pl.BlockSpec((1, LANES), lambda i: (0, i)),
        ],
        out_specs=[
            pl.BlockSpec((N_PROMPT_ROWS, LANES), lambda i: (0, i)),
            pl.BlockSpec((N_TAIL_ROWS, LANES), lambda i: (0, i)),
            state(BATCH), state(BATCH), state(DEC_BATCH), state(DEC_BATCH),
        ],
        out_shape=[
            jax.ShapeDtypeStruct((N_PROMPT_ROWS, W_SSM), F32),
            jax.ShapeDtypeStruct((N_TAIL_ROWS, W_SSM), F32),
            jax.ShapeDtypeStruct((BATCH, N_GROUPS * SSM_STATE), F32),
            jax.ShapeDtypeStruct((BATCH, N_GROUPS * SSM_STATE), F32),
            jax.ShapeDtypeStruct((DEC_BATCH, N_GROUPS * SSM_STATE), F32),
            jax.ShapeDtypeStruct((DEC_BATCH, N_GROUPS * SSM_STATE), F32),
        ],
        scratch_shapes=[
            pltpu.VMEM((FOLD_ROWS, CHUNK * LANES), F32),
            pltpu.VMEM((CHUNK * LANES, 2 * ns), BF16),
            pltpu.VMEM((CHUNK * LANES, CHUNK * LANES // 2), BF16),
            pltpu.VMEM((CHUNK * LANES, 2 * ns), BF16),
            pltpu.VMEM((ns // LANES, FOLD_ROWS, LANES), F32),
            pltpu.VMEM((ns // LANES, FOLD_ROWS, LANES), F32),
            pltpu.VMEM((FOLD_ROWS, CHUNK * LANES), F32),
            pltpu.VMEM((N_SAMPLE_ROWS, 2 * ns), BF16),
        ],
        compiler_params=_params(1),
        name="s5",
    )(u_p, u_t, u_t, h0_re, h0_im, b_t, ct_t, pwl, d_skip)


def _out_kernel(x_ref, a_ref, sy_ref, g_ref, wga_ref, wgb_ref, bga_ref, bgb_ref, gluw_ref, glub_ref,
                wa_ref, wb_ref, wo_ref, o_ref, xn_ref, s_ref):
    j = pl.program_id(1)

    @pl.when(j == 0)
    def _():
        x = x_ref[...]
        xn_ref[...] = _rms(x, g_ref[...]).astype(BF16)
        o_ref[...] = x
        s = jax.nn.gelu(sy_ref[...])
        z = jnp.dot(s.astype(BF16), gluw_ref[...], preferred_element_type=F32) + glub_ref[...]
        s_ref[...] = (s * jax.nn.sigmoid(z)).astype(BF16)

    xn = xn_ref[...]
    ga = jax.nn.sigmoid(jnp.dot(xn, wga_ref[...], preferred_element_type=F32) + bga_ref[...])
    gb = jax.nn.sigmoid(jnp.dot(xn, wgb_ref[...], preferred_element_type=F32) + bgb_ref[...])
    merged = (ga * jnp.dot(a_ref[...], wa_ref[...], preferred_element_type=F32)
              + gb * jnp.dot(s_ref[...], wb_ref[...], preferred_element_type=F32))
    o_ref[...] += jnp.dot(merged.astype(BF16), wo_ref[...], preferred_element_type=F32)


def _out_proj(x, a, sy, g, w_in, b_gate, glu_w, glu_b, w_a, w_b, w_out, tm):
    rows = x.shape[0]
    nb = D_MODEL // TN_OUT
    g0 = (W_POOL + W_SSM) // TN_OUT
    blk = lambda i, j: _serpentine(i, j, nb)
    return pl.pallas_call(
        _out_kernel,
        grid=(rows // tm, nb),
        in_specs=[
            pl.BlockSpec((tm, D_MODEL), lambda i, j: (i, 0)),
            pl.BlockSpec((tm, W_POOL), lambda i, j: (i, 0)),
            pl.BlockSpec((tm, W_SSM), lambda i, j: (i, 0)),
            pl.BlockSpec((1, D_MODEL), lambda i, j: (0, 0)),
            pl.BlockSpec((D_MODEL, TN_OUT), lambda i, j: (0, g0 + blk(i, j))),
            pl.BlockSpec((D_MODEL, TN_OUT), lambda i, j: (0, g0 + nb + blk(i, j))),
            pl.BlockSpec((1, TN_OUT), lambda i, j: (0, blk(i, j))),
            pl.BlockSpec((1, TN_OUT), lambda i, j: (0, nb + blk(i, j))),
            pl.BlockSpec((W_SSM, W_SSM), lambda i, j: (0, 0)),
            pl.BlockSpec((1, W_SSM), lambda i, j: (0, 0)),
            pl.BlockSpec((W_POOL, TN_OUT), lambda i, j: (0, blk(i, j))),
            pl.BlockSpec((W_SSM, TN_OUT), lambda i, j: (0, blk(i, j))),
            pl.BlockSpec((TN_OUT, D_MODEL), lambda i, j: (blk(i, j), 0)),
        ],
        out_specs=pl.BlockSpec((tm, D_MODEL), lambda i, j: (i, 0)),
        out_shape=jax.ShapeDtypeStruct((rows, D_MODEL), F32),
        scratch_shapes=[pltpu.VMEM((tm, D_MODEL), BF16), pltpu.VMEM((tm, W_SSM), BF16)],
        compiler_params=_params(2),
        name="out_proj",
    )(x, a, sy, g, w_in, w_in, b_gate, b_gate, glu_w, glu_b, w_a, w_b, w_out)


def kernel(x_prompt, x_sample, state_pool, state_ssm_re, state_ssm_im, meta_tokens, norm_ffn1, ffn1_w_gate, ffn1_w_up, ffn1_w_down, norm_mix, w_in, b_gate, pool_w, pool_scale, ssm_lambda_re, ssm_lambda_im, ssm_log_dt, ssm_b_re, ssm_b_im, ssm_c_re, ssm_c_im, ssm_d, glu_w, glu_b, w_branch_a, w_branch_b, w_out, norm_ffn2, ffn2_w_gate, ffn2_w_up, ffn2_w_down, final_norm):
    l = 0
    bf = lambda w: w.astype(BF16)
    row = lambda v: v.reshape(1, -1).astype(F32)
    x_p = x_prompt.reshape(N_PROMPT_ROWS, D_MODEL)
    x_t = jnp.concatenate([jnp.transpose(x_sample, (1, 0, 2)).reshape(N_SAMPLE_ROWS, D_MODEL),
                           meta_tokens.astype(F32)], axis=0)

    def ffn_both(xp, xt, g, wg, wu, wd, final):
        op, *w_bf = _ffn(xp, g, wg, wu, wd, fg, TM_FFN, TF, final, emit_bf16=True)
        return op, _ffn(xt, g, *w_bf, fg, N_TAIL_ROWS, TF_TAIL, final)

    fg = row(final_norm)
    w_in_b = bf(w_in[l])
    x1 = ffn_both(x_p, x_t, row(norm_ffn1[l]), ffn1_w_gate[l], ffn1_w_up[l], ffn1_w_down[l], False)

    pool_wb, pool_sc = bf(pool_w[l]), row(pool_scale[l])
    hist_t = jnp.transpose(state_pool[l], (1, 0, 2))
    u_t = _inproj(x1[1], row(norm_mix[l]), w_in_b, N_TAIL_ROWS)
    a_t = _pool_tail(u_t, hist_t, pool_wb, pool_sc)
    u_p, a_p = _inproj_pool_prompt(x1[0], row(norm_mix[l]), w_in_b, u_t, pool_wb, pool_sc)

    b_t, ct_t, pwl = _s5_tables(ssm_lambda_re[l], ssm_lambda_im[l], ssm_log_dt[l], ssm_b_re[l],
                                ssm_b_im[l], ssm_c_re[l], ssm_c_im[l])
    sy_p, sy_t, hp_re, hp_im, hs_re, hs_im = _s5(
        u_p, u_t, state_ssm_re[l].reshape(DEC_BATCH, -1), state_ssm_im[l].reshape(DEC_BATCH, -1),
        b_t, ct_t, pwl, row(ssm_d[l]))

    mix_w = (bf(glu_w[l]), row(glu_b[l]), bf(w_branch_a[l]), bf(w_branch_b[l]), bf(w_out[l]))
    x2 = [_out_proj(x, a, sy, row(norm_mix[l]), w_in_b, row(b_gate[l]), *mix_w, tm)
          for x, a, sy, tm in zip(x1, (a_p, a_t), (sy_p, sy_t), (TM, N_TAIL_ROWS))]
    y_p, y_t = ffn_both(*x2, row(norm_ffn2[l]), ffn2_w_gate[l], ffn2_w_up[l], ffn2_w_down[l], True)

    y_prompt = y_p.reshape(BATCH, SEQ, D_MODEL)
    y_sample = jnp.transpose(y_t[:N_SAMPLE_ROWS].reshape(DEC_SEQ, DEC_BATCH, D_MODEL), (1, 0, 2))
    pool_p = jnp.stack([u_p[(b + 1) * SEQ - POOL_HIST:(b + 1) * SEQ, :W_POOL] for b in range(BATCH)])[None]
    u_pool_s = jnp.transpose(u_t[:N_SAMPLE_ROWS, :W_POOL].reshape(DEC_SEQ, DEC_BATCH, W_POOL), (1, 0, 2))
    pool_s = jnp.concatenate([state_pool[l][:, DEC_SEQ:], u_pool_s], axis=1)[None]
    shp_p = (1, BATCH, N_GROUPS, SSM_STATE)
    shp_s = (1, DEC_BATCH, N_GROUPS, SSM_STATE)
    return (y_prompt, y_sample, pool_p, pool_s,
            hp_re.reshape(shp_p), hp_im.reshape(shp_p), hs_re.reshape(shp_s), hs_im.reshape(shp_s))
```

```python
import functools

import jax
import jax.numpy as jnp
from jax import lax
from jax.experimental import pallas as pl
from jax.experimental.pallas import tpu as pltpu

F32 = jnp.float32
BF16 = jnp.bfloat16

D_MODEL = 2048
BATCH = 4
SEQ = 2048
DEC_BATCH = 128
DEC_SEQ = 4
N_META = 16
D_FF = 5632
W_POOL = 1024
W_SSM = 1024
POOL_WINDOWS = (2, 4, 8, 16)
POOL_GW = 256
POOL_HIST = 15
SSM_GS = 16
N_GROUPS = 64
SSM_STATE = 64
RMS_EPS = 1e-6

N_PROMPT_ROWS = BATCH * SEQ
N_SAMPLE_ROWS = DEC_BATCH * DEC_SEQ
N_TAIL_ROWS = N_SAMPLE_ROWS + N_META
TM_FFN = 1024
TM = 512
TF = 256
TF_TAIL = 512
TN_OUT = 512

LANES = 128
N_SLABS = W_SSM // LANES
SLAB_STATE = (LANES // SSM_GS) * SSM_STATE
CHUNK = 8
N_META_CHUNKS = N_META // CHUNK
N_CHUNKS = (SEQ + N_META) // CHUNK
FOLD_ROW0 = 8
SEQ_STRIDE = FOLD_ROW0 + SEQ // CHUNK
FOLD_ROWS = BATCH * SEQ_STRIDE

V7X_VMEM_BYTES = 64 * 1024 * 1024
VMEM_LIMIT = V7X_VMEM_BYTES - 4 * 1024 * 1024


def _rms(x, g):
    r = lax.rsqrt(jnp.mean(x * x, axis=-1, keepdims=True) + RMS_EPS)
    return x * r * g


def _serpentine(i, j, n, forward_tile=-1):
    return jnp.where((i % 2 == 0) | (i == forward_tile), j, n - 1 - j)


def _params(n_axes):
    return pltpu.CompilerParams(dimension_semantics=("arbitrary",) * n_axes, vmem_limit_bytes=VMEM_LIMIT)


def _ffn_kernel(x_ref, g_ref, wg_ref, wu_ref, wd_ref, fg_ref, o_ref, *rest, final_norm, emit_bf16):
    xn_ref = rest[-1]
    j = pl.program_id(1)

    @pl.when(j == 0)
    def _():
        x = x_ref[...]
        xn_ref[...] = _rms(x, g_ref[...]).astype(BF16)
        o_ref[...] = x

    xn = xn_ref[...]
    wg = wg_ref[...].astype(BF16)
    wu = wu_ref[...].astype(BF16)
    wd = wd_ref[...].astype(BF16)
    if emit_bf16:
        wgb_ref, wub_ref, wdb_ref = rest[:3]
        wgb_ref[...] = wg
        wub_ref[...] = wu
        wdb_ref[...] = wd
    gate = jnp.dot(xn, wg, preferred_element_type=F32)
    up = jnp.dot(xn, wu, preferred_element_type=F32)
    h = (gate * jax.nn.sigmoid(gate) * up * 0.5).astype(BF16)
    o_ref[...] += jnp.dot(h, wd, preferred_element_type=F32)

    if final_norm:
        @pl.when(j == pl.num_programs(1) - 1)
        def _():
            o_ref[...] = _rms(o_ref[...], fg_ref[...])


def _ffn(x, g, wg, wu, wd, fg, tm, tf, final_norm, emit_bf16=False):
    rows = x.shape[0]
    n_steps = D_FF // tf
    row_spec = pl.BlockSpec((tm, D_MODEL), lambda i, j: (i, 0))
    out_specs, out_shape = row_spec, jax.ShapeDtypeStruct((rows, D_MODEL), F32)
    blk = lambda i, j: _serpentine(i, j, n_steps, forward_tile=rows // tm - 1)
    if emit_bf16:
        once = lambda i, j: jnp.where(i == 0, j, n_steps - 1)
        out_specs = [row_spec,
                     pl.BlockSpec((D_MODEL, tf), lambda i, j: (0, once(i, j))),
                     pl.BlockSpec((D_MODEL, tf), lambda i, j: (0, once(i, j))),
                     pl.BlockSpec((tf, D_MODEL), lambda i, j: (once(i, j), 0))]
        out_shape = [out_shape] + [jax.ShapeDtypeStruct(w.shape, BF16) for w in (wg, wu, wd)]
    return pl.pallas_call(
        functools.partial(_ffn_kernel, final_norm=final_norm, emit_bf16=emit_bf16),
        grid=(rows // tm, n_steps),
        in_specs=[
            row_spec,
            pl.BlockSpec((1, D_MODEL), lambda i, j: (0, 0)),
            pl.BlockSpec((D_MODEL, tf), lambda i, j: (0, blk(i, j))),
            pl.BlockSpec((D_MODEL, tf), lambda i, j: (0, blk(i, j))),
            pl.BlockSpec((tf, D_MODEL), lambda i, j: (blk(i, j), 0)),
            pl.BlockSpec((1, D_MODEL), lambda i, j: (0, 0)),
        ],
        out_specs=out_specs,
        out_shape=out_shape,
        scratch_shapes=[pltpu.VMEM((tm, D_MODEL), BF16)],
        compiler_params=_params(2),
        name="ffn_final" if final_norm else "ffn",
    )(x, g, wg, wu, wd, fg)


def _inproj_kernel(x_ref, g_ref, w_ref, o_ref, xn_ref):
    xn = _rms(x_ref[...], g_ref[...]).astype(BF16)
    xn_ref[...] = xn
    o_ref[...] = jnp.dot(xn, w_ref[...], preferred_element_type=F32)


def _inproj(x, g, w_in, tm):
    rows = x.shape[0]
    n = W_POOL + W_SSM
    return pl.pallas_call(
        _inproj_kernel,
        grid=(rows // tm,),
        in_specs=[
            pl.BlockSpec((tm, D_MODEL), lambda i: (i, 0)),
            pl.BlockSpec((1, D_MODEL), lambda i: (0, 0)),
            pl.BlockSpec((D_MODEL, n), lambda i: (0, 0)),
        ],
        out_specs=[pl.BlockSpec((tm, n), lambda i: (i, 0)), pl.BlockSpec((tm, D_MODEL), lambda i: (i, 0))],
        out_shape=[jax.ShapeDtypeStruct((rows, n), F32), jax.ShapeDtypeStruct((rows, D_MODEL), BF16)],
        compiler_params=_params(1),
        name="inproj",
    )(x, g, w_in)


TILES_PER_SEQ = SEQ // TM
HALO = 16


def _pool_project(mean_ref, u_ref, pw_ref, scale_ref, o_ref):
    for g in range(len(POOL_WINDOWS)):
        cols = slice(g * POOL_GW, (g + 1) * POOL_GW)
        d = (mean_ref[:, cols] - u_ref[:, cols]).astype(BF16)
        y = jnp.dot(d, pw_ref[g], preferred_element_type=F32) * scale_ref[:, cols]
        o_ref[:, cols] = y.astype(BF16)


def _inproj_pool_kernel(x_ref, g_ref, w_ref, meta_ref, pw_ref, scale_ref, u_ref, o_ref, xn_ref, full_ref):
    @pl.when(pl.program_id(0) % TILES_PER_SEQ == 0)
    def _():
        full_ref[0:HALO, :] = meta_ref[...]

    xn = _rms(x_ref[...], g_ref[...]).astype(BF16)
    xn_ref[...] = xn
    u_pool = jnp.dot(xn, w_ref[:, :W_POOL], preferred_element_type=F32)
    u_ref[:, :W_POOL] = u_pool
    full_ref[HALO:HALO + TM, :] = u_pool
    for g, w in enumerate(POOL_WINDOWS):
        cols = slice(g * POOL_GW, (g + 1) * POOL_GW)
        ucols = slice(W_POOL + g * POOL_GW, W_POOL + (g + 1) * POOL_GW)
        u_ref[:, ucols] = jnp.dot(xn, w_ref[:, ucols], preferred_element_type=F32)
        acc = full_ref[HALO:HALO + TM, cols]
        for k in range(1, w):
            acc = acc + full_ref[HALO - k:HALO - k + TM, cols]
        d = (acc * (1.0 / w) - full_ref[HALO:HALO + TM, cols]).astype(BF16)
        y = jnp.dot(d, pw_ref[g], preferred_element_type=F32) * scale_ref[:, cols]
        o_ref[:, cols] = y.astype(BF16)
    full_ref[0:HALO, :] = full_ref[TM:TM + HALO, :]


def _pool_tail_kernel(u_ref, hist_ref, pw_ref, scale_ref, o_ref, mean_ref):
    for g, w in enumerate(POOL_WINDOWS):
        cols = slice(g * POOL_GW, (g + 1) * POOL_GW)
        for t in range(DEC_SEQ):
            acc = None
            for k in range(w):
                p = POOL_HIST + t - k
                if p >= POOL_HIST:
                    q = p - POOL_HIST
                    term = u_ref[q * DEC_BATCH:(q + 1) * DEC_BATCH, cols]
                else:
                    term = hist_ref[p, :, cols]
                acc = term if acc is None else acc + term
            mean_ref[t * DEC_BATCH:(t + 1) * DEC_BATCH, cols] = acc * (1.0 / w)
    mean_ref[N_SAMPLE_ROWS:, :] = u_ref[N_SAMPLE_ROWS:, :]
    _pool_project(mean_ref, u_ref, pw_ref, scale_ref, o_ref)


_PW_SPEC = pl.BlockSpec((len(POOL_WINDOWS), POOL_GW, POOL_GW), lambda i: (0, 0, 0))
_SCALE_SPEC = pl.BlockSpec((1, W_POOL), lambda i: (0, 0))


def _inproj_pool_prompt(x, g, w_in, u_t, pool_w, pool_scale):
    n = W_POOL + W_SSM
    return pl.pallas_call(
        _inproj_pool_kernel,
        grid=(N_PROMPT_ROWS // TM,),
        in_specs=[
            pl.BlockSpec((TM, D_MODEL), lambda i: (i, 0)),
            pl.BlockSpec((1, D_MODEL), lambda i: (0, 0)),
            pl.BlockSpec((D_MODEL, n), lambda i: (0, 0)),
            pl.BlockSpec((N_META, W_POOL), lambda i: (N_SAMPLE_ROWS // N_META, 0)),
            _PW_SPEC, _SCALE_SPEC,
        ],
        out_specs=[pl.BlockSpec((TM, n), lambda i: (i, 0)), pl.BlockSpec((TM, W_POOL), lambda i: (i, 0)),
                   pl.BlockSpec((TM, D_MODEL), lambda i: (i, 0))],
        out_shape=[jax.ShapeDtypeStruct((N_PROMPT_ROWS, n), F32),
                   jax.ShapeDtypeStruct((N_PROMPT_ROWS, W_POOL), BF16),
                   jax.ShapeDtypeStruct((N_PROMPT_ROWS, D_MODEL), BF16)],
        scratch_shapes=[pltpu.VMEM((HALO + TM, W_POOL), F32)],
        compiler_params=_params(1),
        name="inproj_pool",
    )(x, g, w_in, u_t, pool_w, pool_scale)


def _pool_tail(u_t, hist_t, pool_w, pool_scale):
    return pl.pallas_call(
        _pool_tail_kernel,
        grid=(1,),
        in_specs=[
            pl.BlockSpec((N_TAIL_ROWS, W_POOL), lambda i: (0, 0)),
            pl.BlockSpec((POOL_HIST, DEC_BATCH, W_POOL), lambda i: (0, 0, 0)),
            _PW_SPEC, _SCALE_SPEC,
        ],
        out_specs=pl.BlockSpec((N_TAIL_ROWS, W_POOL), lambda i: (0, 0)),
        out_shape=jax.ShapeDtypeStruct((N_TAIL_ROWS, W_POOL), BF16),
        scratch_shapes=[pltpu.VMEM((N_TAIL_ROWS, W_POOL), F32)],
        compiler_params=_params(1),
        name="pool_tail",
    )(u_t, hist_t, pool_w, pool_scale)


def _s5_tables(lam_re, lam_im, log_dt, b_re, b_im, c_re, c_im):
    dt = jnp.exp(log_dt)[:, None]
    k = jnp.arange(CHUNK + 1, dtype=F32)[:, None, None]
    mag = jnp.exp(k * (lam_re * dt)[None])
    ang = k * (lam_im * dt)[None]
    pw_re, pw_im = mag * jnp.cos(ang), mag * jnp.sin(ang)
    lb_re, lb_im = pw_re[1], pw_im[1]
    den = lam_re * lam_re + lam_im * lam_im
    q_re = ((lb_re - 1.0) * lam_re + lb_im * lam_im) / den
    q_im = (lb_im * lam_re - (lb_re - 1.0) * lam_im) / den
    bt_re, bt_im = jnp.transpose(b_re, (0, 2, 1)), jnp.transpose(b_im, (0, 2, 1))
    bb_re = q_re[:, None, :] * bt_re - q_im[:, None, :] * bt_im
    bb_im = q_re[:, None, :] * bt_im + q_im[:, None, :] * bt_re
    slab = lambda t: t.reshape(N_SLABS, LANES, SSM_STATE)
    b_c = jnp.concatenate([slab(bb_re), slab(bb_im)], axis=2)
    c_c = jnp.concatenate([slab(c_re), slab(c_im)], axis=2)
    pr = pw_re.reshape(CHUNK + 1, N_SLABS, SLAB_STATE)
    pi = pw_im.reshape(CHUNK + 1, N_SLABS, SLAB_STATE)
    pwl = jnp.concatenate([jnp.transpose(pr, (1, 0, 2)), jnp.transpose(pi, (1, 0, 2))], axis=2)
    return b_c, c_c, pwl


_NT = (((1,), (1,)), ((), ()))


def _block_diag(x):
    tiled = jnp.concatenate([x] * (LANES // SSM_GS), axis=1)
    shift = lambda n: n.bit_length() - 1
    row_group = lax.shift_right_logical(lax.broadcasted_iota(jnp.int32, tiled.shape, 0), shift(SSM_GS))
    lane_group = lax.shift_right_logical(lax.broadcasted_iota(jnp.int32, tiled.shape, 1), shift(SSM_STATE))
    return jnp.where(row_group == lane_group, tiled, 0.0)


def _s5_kernel(up_ref, um_ref, us_ref, h0re_ref, h0im_ref, b_ref, ct_ref, pwl_ref, d_ref,
               yp_ref, yt_ref, hpre_ref, hpim_ref, hsre_ref, hsim_ref,
               l_ref, f_ref, m_ref, et_ref, vre_ref, vim_ref, yf_ref, hs_ref):
    ns = SLAB_STATE
    b_re = _block_diag(b_ref[0][:, :SSM_STATE])
    b_im = _block_diag(b_ref[0][:, SSM_STATE:])
    ct_re = _block_diag(ct_ref[0][:, :SSM_STATE])
    ct_im = _block_diag(ct_ref[0][:, SSM_STATE:])
    ct_neg = jnp.concatenate([ct_re, -ct_im], axis=1).astype(BF16)

    for b in range(BATCH):
        r0 = b * SEQ_STRIDE + FOLD_ROW0
        l_ref[r0 - FOLD_ROW0:r0, :] = jnp.zeros((FOLD_ROW0, CHUNK * LANES), F32)
        for s in range(CHUNK):
            cols = slice(s * LANES, (s + 1) * LANES)
            l_ref[r0:r0 + SEQ // CHUNK, cols] = up_ref[pl.ds(b * SEQ + s, SEQ // CHUNK, stride=CHUNK), :]
            for m in range(N_META_CHUNKS):
                r = r0 - N_META_CHUNKS + m
                l_ref[r:r + 1, cols] = um_ref[m * CHUNK + s:m * CHUNK + s + 1, :]
    lb = l_ref[...].astype(BF16)

    for s in range(CHUNK):
        p = pwl_ref[0, CHUNK - 1 - s:CHUNK - s, :]
        p_re, p_im = p[:, :ns], p[:, ns:]
        rows = slice(s * LANES, (s + 1) * LANES)
        f_ref[rows, :ns] = (b_re * p_re - b_im * p_im).astype(BF16)
        f_ref[rows, ns:] = (b_re * p_im + b_im * p_re).astype(BF16)
    v = jnp.dot(lb, f_ref[...], preferred_element_type=F32)
    nk = ns // LANES
    for k in range(nk):
        vre_ref[k] = v[:, k * LANES:(k + 1) * LANES]
        vim_ref[k] = v[:, ns + k * LANES:ns + (k + 1) * LANES]

    a = pwl_ref[0, CHUNK:CHUNK + 1, :]
    a_re = [jnp.broadcast_to(a[:, k * LANES:(k + 1) * LANES], (BATCH, LANES)) for k in range(nk)]
    a_im = [jnp.broadcast_to(a[:, ns + k * LANES:ns + (k + 1) * LANES], (BATCH, LANES)) for k in range(nk)]
    h_re = [jnp.zeros((BATCH, LANES), F32) for _ in range(nk)]
    h_im = [jnp.zeros((BATCH, LANES), F32) for _ in range(nk)]
    for j in range(N_CHUNKS):
        rows = pl.ds(FOLD_ROW0 - N_META_CHUNKS + j, BATCH, stride=SEQ_STRIDE)
        for k in range(nk):
            v_re = vre_ref[k, rows, :]
            v_im = vim_ref[k, rows, :]
            vre_ref[k, rows, :] = h_re[k]
            vim_ref[k, rows, :] = h_im[k]
            h_re[k], h_im[k] = (a_re[k] * h_re[k] - a_im[k] * h_im[k] + v_re,
                                a_re[k] * h_im[k] + a_im[k] * h_re[k] + v_im)
    for k in range(nk):
        hpre_ref[:, k * LANES:(k + 1) * LANES] = h_re[k]
        hpim_ref[:, k * LANES:(k + 1) * LANES] = h_im[k]

    kb_all = lax.dot_general(f_ref[...], ct_neg, _NT, preferred_element_type=F32)
    ri = lax.broadcasted_iota(jnp.int32, (LANES, LANES), 0)
    ci = lax.broadcasted_iota(jnp.int32, (LANES, LANES), 1)
    d_diag = jnp.where(ri == ci, jnp.broadcast_to(d_ref[...], (LANES, LANES)), 0.0)
    kb = [None] * CHUNK
    for s in range(CHUNK):
        blk = kb_all[s * LANES:(s + 1) * LANES, :]
        if s == CHUNK - 1:
            blk = blk + d_diag
        kb[CHUNK - 1 - s] = blk.astype(BF16)

    half = CHUNK // 2
    for s in range(CHUNK):
        for t in range(half, CHUNK):
            blk = kb[t - s] if t >= s else jnp.zeros((LANES, LANES), BF16)
            m_ref[s * LANES:(s + 1) * LANES, (t - half) * LANES:(t - half + 1) * LANES] = blk
    y_left = jnp.dot(lb[:, :half * LANES], m_ref[half * LANES:, :], preferred_element_type=F32)
    y_right = jnp.dot(lb, m_ref[...], preferred_element_type=F32)

    for t in range(CHUNK):
        p = pwl_ref[0, t + 1:t + 2, :]
        p_re, p_im = p[:, :ns], p[:, ns:]
        rows = slice(t * LANES, (t + 1) * LANES)
        et_ref[rows, :ns] = (ct_re * p_re - ct_im * p_im).astype(BF16)
        et_ref[rows, ns:] = (-(ct_re * p_im + ct_im * p_re)).astype(BF16)
    h_in = jnp.concatenate([vre_ref[k] for k in range(nk)] + [vim_ref[k] for k in range(nk)],
                           axis=1).astype(BF16)
    ys = lax.dot_general(h_in, et_ref[...], _NT, preferred_element_type=F32)
    yf_ref[:, :half * LANES] = y_left + ys[:, :half * LANES]
    yf_ref[:, half * LANES:] = y_right + ys[:, half * LANES:]
    for b in range(BATCH):
        r0 = b * SEQ_STRIDE + FOLD_ROW0
        for t in range(CHUNK):
            yp_ref[pl.ds(b * SEQ + t, SEQ // CHUNK, stride=CHUNK), :] = (
                yf_ref[r0:r0 + SEQ // CHUNK, t * LANES:(t + 1) * LANES])

    u = us_ref[...]
    b_all = jnp.concatenate([b_re, b_im], axis=1).astype(BF16)
    bu = jnp.dot(u.astype(BF16), b_all, preferred_element_type=F32)
    l1 = pwl_ref[0, 1:2, :]
    l_re, l_im = l1[:, :ns], l1[:, ns:]
    g_re = h0re_ref[...]
    g_im = h0im_ref[...]
    for t in range(DEC_SEQ):
        rows = slice(t * DEC_BATCH, (t + 1) * DEC_BATCH)
        g_re, g_im = (l_re * g_re - l_im * g_im + bu[rows, :ns],
                      l_re * g_im + l_im * g_re + bu[rows, ns:])
        hs_ref[rows, :ns] = g_re.astype(BF16)
        hs_ref[rows, ns:] = g_im.astype(BF16)
    hsre_ref[...] = g_re
    hsim_ref[...] = g_im
    yt_ref[0:N_SAMPLE_ROWS, :] = (lax.dot_general(hs_ref[...], ct_neg, _NT, preferred_element_type=F32)
                                  + d_ref[...] * u)
    yt_ref[N_SAMPLE_ROWS:, :] = jnp.zeros((N_META, LANES), F32)


def _s5(u_p, u_t, h0_re, h0_im, b_t, ct_t, pwl, d_skip):
    c0 = W_POOL // LANES
    ns = SLAB_STATE
    state = lambda rows: pl.BlockSpec((rows, ns), lambda i: (0, i))
    return pl.pallas_call(
        _s5_kernel,
        grid=(N_SLABS,),
        in_specs=[
            pl.BlockSpec((N_PROMPT_ROWS, LANES), lambda i: (0, c0 + i)),
            pl.BlockSpec((N_META, LANES), lambda i: (N_SAMPLE_ROWS // N_META, c0 + i)),
            pl.BlockSpec((N_SAMPLE_ROWS, LANES), lambda i: (0, c0 + i)),
            state(DEC_BATCH), state(DEC_BATCH),
            pl.BlockSpec((1, LANES, 2 * SSM_STATE), lambda i: (i, 0, 0)),
            pl.BlockSpec((1, LANES, 2 * SSM_STATE), lambda i: (i, 0, 0)),
            pl.BlockSpec((1, CHUNK + 1, 2 * ns), lambda i: (i, 0, 0)),
            pl.BlockSpec((1, LANES), lambda i: (0, i)),
        ],
        out_specs=[
            pl.BlockSpec((N_PROMPT_ROWS, LANES), lambda i: (0, i)),
            pl.BlockSpec((N_TAIL_ROWS, LANES), lambda i: (0, i)),
            state(BATCH), state(BATCH), state(DEC_BATCH), state(DEC_BATCH),
        ],
        out_shape=[
            jax.ShapeDtypeStruct((N_PROMPT_ROWS, W_SSM), F32),
            jax.ShapeDtypeStruct((N_TAIL_ROWS, W_SSM), F32),
            jax.ShapeDtypeStruct((BATCH, N_GROUPS * SSM_STATE), F32),
            jax.ShapeDtypeStruct((BATCH, N_GROUPS * SSM_STATE), F32),
            jax.ShapeDtypeStruct((DEC_BATCH, N_GROUPS * SSM_STATE), F32),
            jax.ShapeDtypeStruct((DEC_BATCH, N_GROUPS * SSM_STATE), F32),
        ],
        scratch_shapes=[
            pltpu.VMEM((FOLD_ROWS, CHUNK * LANES), F32),
            pltpu.VMEM((CHUNK * LANES, 2 * ns), BF16),
            pltpu.VMEM((CHUNK * LANES, CHUNK * LANES // 2), BF16),
            pltpu.VMEM((CHUNK * LANES, 2 * ns), BF16),
            pltpu.VMEM((ns // LANES, FOLD_ROWS, LANES), F32),
            pltpu.VMEM((ns // LANES, FOLD_ROWS, LANES), F32),
            pltpu.VMEM((FOLD_ROWS, CHUNK * LANES), F32),
            pltpu.VMEM((N_SAMPLE_ROWS, 2 * ns), BF16),
        ],
        compiler_params=_params(1),
        name="s5",
    )(u_p, u_t, u_t, h0_re, h0_im, b_t, ct_t, pwl, d_skip)


def _out_kernel(x_ref, xn_ref, a_ref, sy_ref, wga_ref, wgb_ref, bga_ref, bgb_ref, gluw_ref, glub_ref,
                wa_ref, wb_ref, wo_ref, o_ref, s_ref):
    j = pl.program_id(1)

    @pl.when(j == 0)
    def _():
        o_ref[...] = x_ref[...]
        s = jax.nn.gelu(sy_ref[...])
        z = jnp.dot(s.astype(BF16), gluw_ref[...], preferred_element_type=F32) + glub_ref[...]
        s_ref[...] = (s * jax.nn.sigmoid(z)).astype(BF16)

    xn = xn_ref[...]
    ga = jax.nn.sigmoid(jnp.dot(xn, wga_ref[...], preferred_element_type=F32) + bga_ref[...])
    gb = jax.nn.sigmoid(jnp.dot(xn, wgb_ref[...], preferred_element_type=F32) + bgb_ref[...])
    merged = (ga * jnp.dot(a_ref[...], wa_ref[...], preferred_element_type=F32)
              + gb * jnp.dot(s_ref[...], wb_ref[...], preferred_element_type=F32))
    o_ref[...] += jnp.dot(merged.astype(BF16), wo_ref[...], preferred_element_type=F32)


def _out_proj(x, xn, a, sy, w_in, b_gate, glu_w, glu_b, w_a, w_b, w_out, tm):
    rows = x.shape[0]
    nb = D_MODEL // TN_OUT
    g0 = (W_POOL + W_SSM) // TN_OUT
    blk = lambda i, j: _serpentine(i, j, nb)
    return pl.pallas_call(
        _out_kernel,
        grid=(rows // tm, nb),
        in_specs=[
            pl.BlockSpec((tm, D_MODEL), lambda i, j: (i, 0)),
            pl.BlockSpec((tm, D_MODEL), lambda i, j: (i, 0)),
            pl.BlockSpec((tm, W_POOL), lambda i, j: (i, 0)),
            pl.BlockSpec((tm, W_SSM), lambda i, j: (i, 0)),
            pl.BlockSpec((D_MODEL, TN_OUT), lambda i, j: (0, g0 + blk(i, j))),
            pl.BlockSpec((D_MODEL, TN_OUT), lambda i, j: (0, g0 + nb + blk(i, j))),
            pl.BlockSpec((1, TN_OUT), lambda i, j: (0, blk(i, j))),
            pl.BlockSpec((1, TN_OUT), lambda i, j: (0, nb + blk(i, j))),
            pl.BlockSpec((W_SSM, W_SSM), lambda i, j: (0, 0)),
            pl.BlockSpec((1, W_SSM), lambda i, j: (0, 0)),
            pl.BlockSpec((W_POOL, TN_OUT), lambda i, j: (0, blk(i, j))),
            pl.BlockSpec((W_SSM, TN_OUT), lambda i, j: (0, blk(i, j))),
            pl.BlockSpec((TN_OUT, D_MODEL), lambda i, j: (blk(i, j), 0)),
        ],
        out_specs=pl.BlockSpec((tm, D_MODEL), lambda i, j: (i, 0)),
        out_shape=jax.ShapeDtypeStruct((rows, D_MODEL), F32),
        scratch_shapes=[pltpu.VMEM((tm, W_SSM), BF16)],
        compiler_params=_params(2),
        name="out_proj",
    )(x, xn, a, sy, w_in, w_in, b_gate, b_gate, glu_w, glu_b, w_a, w_b, w_out)


def kernel(x_prompt, x_sample, state_pool, state_ssm_re, state_ssm_im, meta_tokens, norm_ffn1, ffn1_w_gate, ffn1_w_up, ffn1_w_down, norm_mix, w_in, b_gate, pool_w, pool_scale, ssm_lambda_re, ssm_lambda_im, ssm_log_dt, ssm_b_re, ssm_b_im, ssm_c_re, ssm_c_im, ssm_d, glu_w, glu_b, w_branch_a, w_branch_b, w_out, norm_ffn2, ffn2_w_gate, ffn2_w_up, ffn2_w_down, final_norm):
    l = 0
    bf = lambda w: w.astype(BF16)
    row = lambda v: v.reshape(1, -1).astype(F32)
    x_p = x_prompt.reshape(N_PROMPT_ROWS, D_MODEL)
    x_t = jnp.concatenate([jnp.transpose(x_sample, (1, 0, 2)).reshape(N_SAMPLE_ROWS, D_MODEL),
                           meta_tokens.astype(F32)], axis=0)

    def ffn_both(xp, xt, g, wg, wu, wd, final):
        op, *w_bf = _ffn(xp, g, wg, wu, wd, fg, TM_FFN, TF, final, emit_bf16=True)
        return op, _ffn(xt, g, *w_bf, fg, N_TAIL_ROWS, TF_TAIL, final)

    fg = row(final_norm)
    w_in_b = bf(w_in[l])
    x1 = ffn_both(x_p, x_t, row(norm_ffn1[l]), ffn1_w_gate[l], ffn1_w_up[l], ffn1_w_down[l], False)

    pool_wb, pool_sc = bf(pool_w[l]), row(pool_scale[l])
    hist_t = jnp.transpose(state_pool[l], (1, 0, 2))
    u_t, xn_t = _inproj(x1[1], row(norm_mix[l]), w_in_b, N_TAIL_ROWS)
    a_t = _pool_tail(u_t, hist_t, pool_wb, pool_sc)
    u_p, a_p, xn_p = _inproj_pool_prompt(x1[0], row(norm_mix[l]), w_in_b, u_t, pool_wb, pool_sc)

    b_t, ct_t, pwl = _s5_tables(ssm_lambda_re[l], ssm_lambda_im[l], ssm_log_dt[l], ssm_b_re[l],
                                ssm_b_im[l], ssm_c_re[l], ssm_c_im[l])
    sy_p, sy_t, hp_re, hp_im, hs_re, hs_im = _s5(
        u_p, u_t, state_ssm_re[l].reshape(DEC_BATCH, -1), state_ssm_im[l].reshape(DEC_BATCH, -1),
        b_t, ct_t, pwl, row(ssm_d[l]))

    mix_w = (bf(glu_w[l]), row(glu_b[l]), bf(w_branch_a[l]), bf(w_branch_b[l]), bf(w_out[l]))
    x2 = [_out_proj(x, xn, a, sy, w_in_b, row(b_gate[l]), *mix_w, tm)
          for x, xn, a, sy, tm in zip(x1, (xn_p, xn_t), (a_p, a_t), (sy_p, sy_t), (TM, N_TAIL_ROWS))]
    y_p, y_t = ffn_both(*x2, row(norm_ffn2[l]), ffn2_w_gate[l], ffn2_w_up[l], ffn2_w_down[l], True)

    y_prompt = y_p.reshape(BATCH, SEQ, D_MODEL)
    y_sample = jnp.transpose(y_t[:N_SAMPLE_ROWS].reshape(DEC_SEQ, DEC_BATCH, D_MODEL), (1, 0, 2))
    pool_p = jnp.stack([u_p[(b + 1) * SEQ - POOL_HIST:(b + 1) * SEQ, :W_POOL] for b in range(BATCH)])[None]
    u_pool_s = jnp.transpose(u_t[:N_SAMPLE_ROWS, :W_POOL].reshape(DEC_SEQ, DEC_BATCH, W_POOL), (1, 0, 2))
    pool_s = jnp.concatenate([state_pool[l][:, DEC_SEQ:], u_pool_s], axis=1)[None]
    shp_p = (1, BATCH, N_GROUPS, SSM_STATE)
    shp_s = (1, DEC_BATCH, N_GROUPS, SSM_STATE)
    return (y_prompt, y_sample, pool_p, pool_s,
            hp_re.reshape(shp_p), hp_im.reshape(shp_p), hs_re.reshape(shp_s), hs_im.reshape(shp_s))
```

```python
import functools

import jax
import jax.numpy as jnp
from jax import lax
from jax.experimental import pallas as pl
from jax.experimental.pallas import tpu as pltpu

F32 = jnp.float32
BF16 = jnp.bfloat16

D_MODEL = 2048
BATCH = 4
SEQ = 2048
DEC_BATCH = 128
DEC_SEQ = 4
N_META = 16
D_FF = 5632
W_POOL = 1024
W_SSM = 1024
POOL_WINDOWS = (2, 4, 8, 16)
POOL_GW = 256
POOL_HIST = 15
SSM_GS = 16
N_GROUPS = 64
SSM_STATE = 64
RMS_EPS = 1e-6

N_PROMPT_ROWS = BATCH * SEQ
N_SAMPLE_ROWS = DEC_BATCH * DEC_SEQ
N_TAIL_ROWS = N_SAMPLE_ROWS + N_META
TM_FFN = 1024
TM = 512
TF = 256
TF_TAIL = 512
TN_OUT = 512

LANES = 128
N_SLABS = W_SSM // LANES
SLAB_STATE = (LANES // SSM_GS) * SSM_STATE
CHUNK = 8
N_META_CHUNKS = N_META // CHUNK
N_CHUNKS = (SEQ + N_META) // CHUNK
FOLD_ROW0 = 8
SEQ_STRIDE = FOLD_ROW0 + SEQ // CHUNK
FOLD_ROWS = BATCH * SEQ_STRIDE

V7X_VMEM_BYTES = 64 * 1024 * 1024
VMEM_LIMIT = V7X_VMEM_BYTES - 4 * 1024 * 1024


def _rms(x, g):
    r = lax.rsqrt(jnp.mean(x * x, axis=-1, keepdims=True) + RMS_EPS)
    return x * r * g


def _serpentine(i, j, n, forward_tile=-1):
    return jnp.where((i % 2 == 0) | (i == forward_tile), j, n - 1 - j)


def _params(n_axes):
    return pltpu.CompilerParams(dimension_semantics=("arbitrary",) * n_axes, vmem_limit_bytes=VMEM_LIMIT)


def _ffn_kernel(x_ref, g_ref, wg_ref, wu_ref, wd_ref, fg_ref, o_ref, *rest, final_norm, emit_bf16):
    xn_ref = rest[-1]
    j = pl.program_id(1)

    @pl.when(j == 0)
    def _():
        x = x_ref[...]
        xn_ref[...] = _rms(x, g_ref[...]).astype(BF16)
        o_ref[...] = x

    xn = xn_ref[...]
    wg = wg_ref[...].astype(BF16)
    wu = wu_ref[...].astype(BF16)
    wd = wd_ref[...].astype(BF16)
    if emit_bf16:
        wgb_ref, wub_ref, wdb_ref = rest[:3]
        wgb_ref[...] = wg
        wub_ref[...] = wu
        wdb_ref[...] = wd
    gate = jnp.dot(xn, wg, preferred_element_type=F32)
    up = jnp.dot(xn, wu, preferred_element_type=F32)
    h = (gate * jax.nn.sigmoid(gate) * up * 0.5).astype(BF16)
    o_ref[...] += jnp.dot(h, wd, preferred_element_type=F32)

    if final_norm:
        @pl.when(j == pl.num_programs(1) - 1)
        def _():
            o_ref[...] = _rms(o_ref[...], fg_ref[...])


def _ffn(x, g, wg, wu, wd, fg, tm, tf, final_norm, emit_bf16=False):
    rows = x.shape[0]
    n_steps = D_FF // tf
    row_spec = pl.BlockSpec((tm, D_MODEL), lambda i, j: (i, 0))
    out_specs, out_shape = row_spec, jax.ShapeDtypeStruct((rows, D_MODEL), F32)
    blk = lambda i, j: _serpentine(i, j, n_steps, forward_tile=rows // tm - 1)
    if emit_bf16:
        once = lambda i, j: jnp.where(i == 0, j, n_steps - 1)
        out_specs = [row_spec,
                     pl.BlockSpec((D_MODEL, tf), lambda i, j: (0, once(i, j))),
                     pl.BlockSpec((D_MODEL, tf), lambda i, j: (0, once(i, j))),
                     pl.BlockSpec((tf, D_MODEL), lambda i, j: (once(i, j), 0))]
        out_shape = [out_shape] + [jax.ShapeDtypeStruct(w.shape, BF16) for w in (wg, wu, wd)]
    return pl.pallas_call(
        functools.partial(_ffn_kernel, final_norm=final_norm, emit_bf16=emit_bf16),
        grid=(rows // tm, n_steps),
        in_specs=[
            row_spec,
            pl.BlockSpec((1, D_MODEL), lambda i, j: (0, 0)),
            pl.BlockSpec((D_MODEL, tf), lambda i, j: (0, blk(i, j))),
            pl.BlockSpec((D_MODEL, tf), lambda i, j: (0, blk(i, j))),
            pl.BlockSpec((tf, D_MODEL), lambda i, j: (blk(i, j), 0)),
            pl.BlockSpec((1, D_MODEL), lambda i, j: (0, 0)),
        ],
        out_specs=out_specs,
        out_shape=out_shape,
        scratch_shapes=[pltpu.VMEM((tm, D_MODEL), BF16)],
        compiler_params=_params(2),
        name="ffn_final" if final_norm else "ffn",
    )(x, g, wg, wu, wd, fg)


def _inproj_kernel(x_ref, g_ref, w_ref, o_ref, xn_ref):
    xn = _rms(x_ref[...], g_ref[...]).astype(BF16)
    xn_ref[...] = xn
    o_ref[...] = jnp.dot(xn, w_ref[...], preferred_element_type=F32)


def _inproj(x, g, w_in, tm):
    rows = x.shape[0]
    n = W_POOL + W_SSM
    return pl.pallas_call(
        _inproj_kernel,
        grid=(rows // tm,),
        in_specs=[
            pl.BlockSpec((tm, D_MODEL), lambda i: (i, 0)),
            pl.BlockSpec((1, D_MODEL), lambda i: (0, 0)),
            pl.BlockSpec((D_MODEL, n), lambda i: (0, 0)),
        ],
        out_specs=[pl.BlockSpec((tm, n), lambda i: (i, 0)), pl.BlockSpec((tm, D_MODEL), lambda i: (i, 0))],
        out_shape=[jax.ShapeDtypeStruct((rows, n), F32), jax.ShapeDtypeStruct((rows, D_MODEL), BF16)],
        compiler_params=_params(1),
        name="inproj",
    )(x, g, w_in)


TILES_PER_SEQ = SEQ // TM
HALO = 16


def _pool_project(mean_ref, u_ref, pw_ref, scale_ref, o_ref):
    for g in range(len(POOL_WINDOWS)):
        cols = slice(g * POOL_GW, (g + 1) * POOL_GW)
        d = (mean_ref[:, cols] - u_ref[:, cols]).astype(BF16)
        y = jnp.dot(d, pw_ref[g], preferred_element_type=F32) * scale_ref[:, cols]
        o_ref[:, cols] = y.astype(BF16)


def _inproj_pool_kernel(x_ref, g_ref, w_ref, meta_ref, pw_ref, scale_ref, u_ref, o_ref, xn_ref, full_ref):
    @pl.when(pl.program_id(0) % TILES_PER_SEQ == 0)
    def _():
        full_ref[0:HALO, :] = meta_ref[...]

    xn = _rms(x_ref[...], g_ref[...]).astype(BF16)
    xn_ref[...] = xn
    u_pool = jnp.dot(xn, w_ref[:, :W_POOL], preferred_element_type=F32)
    u_ref[:, :W_POOL] = u_pool
    full_ref[HALO:HALO + TM, :] = u_pool
    for g, w in enumerate(POOL_WINDOWS):
        cols = slice(g * POOL_GW, (g + 1) * POOL_GW)
        ucols = slice(W_POOL + g * POOL_GW, W_POOL + (g + 1) * POOL_GW)
        u_ref[:, ucols] = jnp.dot(xn, w_ref[:, ucols], preferred_element_type=F32)
        acc = full_ref[HALO:HALO + TM, cols]
        for k in range(1, w):
            acc = acc + full_ref[HALO - k:HALO - k + TM, cols]
        d = (acc * (1.0 / w) - full_ref[HALO:HALO + TM, cols]).astype(BF16)
        y = jnp.dot(d, pw_ref[g], preferred_element_type=F32) * scale_ref[:, cols]
        o_ref[:, cols] = y.astype(BF16)
    full_ref[0:HALO, :] = full_ref[TM:TM + HALO, :]


def _pool_tail_kernel(u_ref, hist_ref, pw_ref, scale_ref, o_ref, mean_ref):
    for g, w in enumerate(POOL_WINDOWS):
        cols = slice(g * POOL_GW, (g + 1) * POOL_GW)
        for t in range(DEC_SEQ):
            acc = None
            for k in range(w):
                p = POOL_HIST + t - k
                if p >= POOL_HIST:
                    q = p - POOL_HIST
                    term = u_ref[q * DEC_BATCH:(q + 1) * DEC_BATCH, cols]
                else:
                    term = hist_ref[p, :, cols]
                acc = term if acc is None else acc + term
            mean_ref[t * DEC_BATCH:(t + 1) * DEC_BATCH, cols] = acc * (1.0 / w)
    mean_ref[N_SAMPLE_ROWS:, :] = u_ref[N_SAMPLE_ROWS:, :]
    _pool_project(mean_ref, u_ref, pw_ref, scale_ref, o_ref)


_PW_SPEC = pl.BlockSpec((len(POOL_WINDOWS), POOL_GW, POOL_GW), lambda i: (0, 0, 0))
_SCALE_SPEC = pl.BlockSpec((1, W_POOL), lambda i: (0, 0))


def _inproj_pool_prompt(x, g, w_in, u_t, pool_w, pool_scale):
    n = W_POOL + W_SSM
    return pl.pallas_call(
        _inproj_pool_kernel,
        grid=(N_PROMPT_ROWS // TM,),
        in_specs=[
            pl.BlockSpec((TM, D_MODEL), lambda i: (i, 0)),
            pl.BlockSpec((1, D_MODEL), lambda i: (0, 0)),
            pl.BlockSpec((D_MODEL, n), lambda i: (0, 0)),
            pl.BlockSpec((N_META, W_POOL), lambda i: (N_SAMPLE_ROWS // N_META, 0)),
            _PW_SPEC, _SCALE_SPEC,
        ],
        out_specs=[pl.BlockSpec((TM, n), lambda i: (i, 0)), pl.BlockSpec((TM, W_POOL), lambda i: (i, 0)),
                   pl.BlockSpec((TM, D_MODEL), lambda i: (i, 0))],
        out_shape=[jax.ShapeDtypeStruct((N_PROMPT_ROWS, n), F32),
                   jax.ShapeDtypeStruct((N_PROMPT_ROWS, W_POOL), BF16),
                   jax.ShapeDtypeStruct((N_PROMPT_ROWS, D_MODEL), BF16)],
        scratch_shapes=[pltpu.VMEM((HALO + TM, W_POOL), F32)],
        compiler_params=_params(1),
        name="inproj_pool",
    )(x, g, w_in, u_t, pool_w, pool_scale)


def _pool_tail(u_t, hist_t, pool_w, pool_scale):
    return pl.pallas_call(
        _pool_tail_kernel,
        grid=(1,),
        in_specs=[
            pl.BlockSpec((N_TAIL_ROWS, W_POOL), lambda i: (0, 0)),
            pl.BlockSpec((POOL_HIST, DEC_BATCH, W_POOL), lambda i: (0, 0, 0)),
            _PW_SPEC, _SCALE_SPEC,
        ],
        out_specs=pl.BlockSpec((N_TAIL_ROWS, W_POOL), lambda i: (0, 0)),
        out_shape=jax.ShapeDtypeStruct((N_TAIL_ROWS, W_POOL), BF16),
        scratch_shapes=[pltpu.VMEM((N_TAIL_ROWS, W_POOL), F32)],
        compiler_params=_params(1),
        name="pool_tail",
    )(u_t, hist_t, pool_w, pool_scale)


def _s5_tables(lam_re, lam_im, log_dt, b_re, b_im, c_re, c_im):
    dt = jnp.exp(log_dt)[:, None]
    k = jnp.arange(CHUNK + 1, dtype=F32)[:, None, None]
    mag = jnp.exp(k * (lam_re * dt)[None])
    ang = k * (lam_im * dt)[None]
    pw_re, pw_im = mag * jnp.cos(ang), mag * jnp.sin(ang)
    lb_re, lb_im = pw_re[1], pw_im[1]
    den = lam_re * lam_re + lam_im * lam_im
    q_re = ((lb_re - 1.0) * lam_re + lb_im * lam_im) / den
    q_im = (lb_im * lam_re - (lb_re - 1.0) * lam_im) / den
    bt_re, bt_im = jnp.transpose(b_re, (0, 2, 1)), jnp.transpose(b_im, (0, 2, 1))
    bb_re = q_re[:, None, :] * bt_re - q_im[:, None, :] * bt_im
    bb_im = q_re[:, None, :] * bt_im + q_im[:, None, :] * bt_re
    slab = lambda t: t.reshape(N_SLABS, LANES, SSM_STATE)
    b_c = jnp.concatenate([slab(bb_re), slab(bb_im)], axis=2)
    c_c = jnp.concatenate([slab(c_re), slab(c_im)], axis=2)
    pr = pw_re.reshape(CHUNK + 1, N_SLABS, SLAB_STATE)
    pi = pw_im.reshape(CHUNK + 1, N_SLABS, SLAB_STATE)
    pwl = jnp.concatenate([jnp.transpose(pr, (1, 0, 2)), jnp.transpose(pi, (1, 0, 2))], axis=2)
    return b_c, c_c, pwl


_NT = (((1,), (1,)), ((), ()))


def _block_diag(x):
    tiled = jnp.concatenate([x] * (LANES // SSM_GS), axis=1)
    shift = lambda n: n.bit_length() - 1
    row_group = lax.shift_right_logical(lax.broadcasted_iota(jnp.int32, tiled.shape, 0), shift(SSM_GS))
    lane_group = lax.shift_right_logical(lax.broadcasted_iota(jnp.int32, tiled.shape, 1), shift(SSM_STATE))
    return jnp.where(row_group == lane_group, tiled, 0.0)


def _s5_kernel(up_ref, um_ref, us_ref, h0re_ref, h0im_ref, b_ref, ct_ref, pwl_ref, d_ref,
               wga_ref, wgb_ref, wo_ref,
               yp_ref, yt_ref, hpre_ref, hpim_ref, hsre_ref, hsim_ref, wgab_ref, wgbb_ref, wob_ref,
               l_ref, f_ref, m_ref, et_ref, vre_ref, vim_ref, yf_ref, hs_ref):
    ns = SLAB_STATE
    wgab_ref[...] = wga_ref[...].astype(BF16)
    wgbb_ref[...] = wgb_ref[...].astype(BF16)
    wob_ref[...] = wo_ref[...].astype(BF16)
    b_re = _block_diag(b_ref[0][:, :SSM_STATE])
    b_im = _block_diag(b_ref[0][:, SSM_STATE:])
    ct_re = _block_diag(ct_ref[0][:, :SSM_STATE])
    ct_im = _block_diag(ct_ref[0][:, SSM_STATE:])
    ct_neg = jnp.concatenate([ct_re, -ct_im], axis=1).astype(BF16)

    for b in range(BATCH):
        r0 = b * SEQ_STRIDE + FOLD_ROW0
        l_ref[r0 - FOLD_ROW0:r0, :] = jnp.zeros((FOLD_ROW0, CHUNK * LANES), F32)
        for s in range(CHUNK):
            cols = slice(s * LANES, (s + 1) * LANES)
            l_ref[r0:r0 + SEQ // CHUNK, cols] = up_ref[pl.ds(b * SEQ + s, SEQ // CHUNK, stride=CHUNK), :]
            for m in range(N_META_CHUNKS):
                r = r0 - N_META_CHUNKS + m
                l_ref[r:r + 1, cols] = um_ref[m * CHUNK + s:m * CHUNK + s + 1, :]
    lb = l_ref[...].astype(BF16)

    for s in range(CHUNK):
        p = pwl_ref[0, CHUNK - 1 - s:CHUNK - s, :]
        p_re, p_im = p[:, :ns], p[:, ns:]
        rows = slice(s * LANES, (s + 1) * LANES)
        f_ref[rows, :ns] = (b_re * p_re - b_im * p_im).astype(BF16)
        f_ref[rows, ns:] = (b_re * p_im + b_im * p_re).astype(BF16)
    v = jnp.dot(lb, f_ref[...], preferred_element_type=F32)
    nk = ns // LANES
    for k in range(nk):
        vre_ref[k] = v[:, k * LANES:(k + 1) * LANES]
        vim_ref[k] = v[:, ns + k * LANES:ns + (k + 1) * LANES]

    a = pwl_ref[0, CHUNK:CHUNK + 1, :]
    a_re = [jnp.broadcast_to(a[:, k * LANES:(k + 1) * LANES], (BATCH, LANES)) for k in range(nk)]
    a_im = [jnp.broadcast_to(a[:, ns + k * LANES:ns + (k + 1) * LANES], (BATCH, LANES)) for k in range(nk)]
    h_re = [jnp.zeros((BATCH, LANES), F32) for _ in range(nk)]
    h_im = [jnp.zeros((BATCH, LANES), F32) for _ in range(nk)]
    for j in range(N_CHUNKS):
        rows = pl.ds(FOLD_ROW0 - N_META_CHUNKS + j, BATCH, stride=SEQ_STRIDE)
        for k in range(nk):
            v_re = vre_ref[k, rows, :]
            v_im = vim_ref[k, rows, :]
            vre_ref[k, rows, :] = h_re[k]
            vim_ref[k, rows, :] = h_im[k]
            h_re[k], h_im[k] = (a_re[k] * h_re[k] - a_im[k] * h_im[k] + v_re,
                                a_re[k] * h_im[k] + a_im[k] * h_re[k] + v_im)
    for k in range(nk):
        hpre_ref[:, k * LANES:(k + 1) * LANES] = h_re[k]
        hpim_ref[:, k * LANES:(k + 1) * LANES] = h_im[k]

    kb_all = lax.dot_general(f_ref[...], ct_neg, _NT, preferred_element_type=F32)
    ri = lax.broadcasted_iota(jnp.int32, (LANES, LANES), 0)
    ci = lax.broadcasted_iota(jnp.int32, (LANES, LANES), 1)
    d_diag = jnp.where(ri == ci, jnp.broadcast_to(d_ref[...], (LANES, LANES)), 0.0)
    kb = [None] * CHUNK
    for s in range(CHUNK):
        blk = kb_all[s * LANES:(s + 1) * LANES, :]
        if s == CHUNK - 1:
            blk = blk + d_diag
        kb[CHUNK - 1 - s] = blk.astype(BF16)

    half = CHUNK // 2
    for s in range(CHUNK):
        for t in range(half, CHUNK):
            blk = kb[t - s] if t >= s else jnp.zeros((LANES, LANES), BF16)
            m_ref[s * LANES:(s + 1) * LANES, (t - half) * LANES:(t - half + 1) * LANES] = blk
    y_left = jnp.dot(lb[:, :half * LANES], m_ref[half * LANES:, :], preferred_element_type=F32)
    y_right = jnp.dot(lb, m_ref[...], preferred_element_type=F32)

    for t in range(CHUNK):
        p = pwl_ref[0, t + 1:t + 2, :]
        p_re, p_im = p[:, :ns], p[:, ns:]
        rows = slice(t * LANES, (t + 1) * LANES)
        et_ref[rows, :ns] = (ct_re * p_re - ct_im * p_im).astype(BF16)
        et_ref[rows, ns:] = (-(ct_re * p_im + ct_im * p_re)).astype(BF16)
    h_in = jnp.concatenate([vre_ref[k] for k in range(nk)] + [vim_ref[k] for k in range(nk)],
                           axis=1).astype(BF16)
    ys = lax.dot_general(h_in, et_ref[...], _NT, preferred_element_type=F32)
    yf_ref[:, :half * LANES] = y_left + ys[:, :half * LANES]
    yf_ref[:, half * LANES:] = y_right + ys[:, half * LANES:]
    for b in range(BATCH):
        r0 = b * SEQ_STRIDE + FOLD_ROW0
        for t in range(CHUNK):
            yp_ref[pl.ds(b * SEQ + t, SEQ // CHUNK, stride=CHUNK), :] = (
                yf_ref[r0:r0 + SEQ // CHUNK, t * LANES:(t + 1) * LANES])

    u = us_ref[...]
    b_all = jnp.concatenate([b_re, b_im], axis=1).astype(BF16)
    bu = jnp.dot(u.astype(BF16), b_all, preferred_element_type=F32)
    l1 = pwl_ref[0, 1:2, :]
    l_re, l_im = l1[:, :ns], l1[:, ns:]
    g_re = h0re_ref[...]
    g_im = h0im_ref[...]
    for t in range(DEC_SEQ):
        rows = slice(t * DEC_BATCH, (t + 1) * DEC_BATCH)
        g_re, g_im = (l_re * g_re - l_im * g_im + bu[rows, :ns],
                      l_re * g_im + l_im * g_re + bu[rows, ns:])
        hs_ref[rows, :ns] = g_re.astype(BF16)
        hs_ref[rows, ns:] = g_im.astype(BF16)
    hsre_ref[...] = g_re
    hsim_ref[...] = g_im
    yt_ref[0:N_SAMPLE_ROWS, :] = (lax.dot_general(hs_ref[...], ct_neg, _NT, preferred_element_type=F32)
                                  + d_ref[...] * u)
    yt_ref[N_SAMPLE_ROWS:, :] = jnp.zeros((N_META, LANES), F32)


def _s5(u_p, u_t, h0_re, h0_im, b_t, ct_t, pwl, d_skip, w_in, w_out):
    c0 = W_POOL // LANES
    ns = SLAB_STATE
    state = lambda rows: pl.BlockSpec((rows, ns), lambda i: (0, i))
    wr = D_MODEL // N_SLABS
    gate_blk0 = (W_POOL + W_SSM) // D_MODEL
    return pl.pallas_call(
        _s5_kernel,
        grid=(N_SLABS,),
        in_specs=[
            pl.BlockSpec((N_PROMPT_ROWS, LANES), lambda i: (0, c0 + i)),
            pl.BlockSpec((N_META, LANES), lambda i: (N_SAMPLE_ROWS // N_META, c0 + i)),
            pl.BlockSpec((N_SAMPLE_ROWS, LANES), lambda i: (0, c0 + i)),
            state(DEC_BATCH), state(DEC_BATCH),
            pl.BlockSpec((1, LANES, 2 * SSM_STATE), lambda i: (i, 0, 0)),
            pl.BlockSpec((1, LANES, 2 * SSM_STATE), lambda i: (i, 0, 0)),
            pl.BlockSpec((1, CHUNK + 1, 2 * ns), lambda i: (i, 0, 0)),
            pl.BlockSpec((1, LANES), lambda i: (0, i)),
            pl.BlockSpec((wr, D_MODEL), lambda i: (i, gate_blk0)),
            pl.BlockSpec((wr, D_MODEL), lambda i: (i, gate_blk0 + 1)),
            pl.BlockSpec((wr, D_MODEL), lambda i: (i, 0)),
        ],
        out_specs=[
            pl.BlockSpec((N_PROMPT_ROWS, LANES), lambda i: (0, i)),
            pl.BlockSpec((N_TAIL_ROWS, LANES), lambda i: (0, i)),
            state(BATCH), state(BATCH), state(DEC_BATCH), state(DEC_BATCH),
        ] + [pl.BlockSpec((wr, D_MODEL), lambda i: (i, 0))] * 3,
        out_shape=[
            jax.ShapeDtypeStruct((N_PROMPT_ROWS, W_SSM), F32),
            jax.ShapeDtypeStruct((N_TAIL_ROWS, W_SSM), F32),
            jax.ShapeDtypeStruct((BATCH, N_GROUPS * SSM_STATE), F32),
            jax.ShapeDtypeStruct((BATCH, N_GROUPS * SSM_STATE), F32),
            jax.ShapeDtypeStruct((DEC_BATCH, N_GROUPS * SSM_STATE), F32),
            jax.ShapeDtypeStruct((DEC_BATCH, N_GROUPS * SSM_STATE), F32),
        ] + [jax.ShapeDtypeStruct((D_MODEL, D_MODEL), BF16)] * 3,
        scratch_shapes=[
            pltpu.VMEM((FOLD_ROWS, CHUNK * LANES), F32),
            pltpu.VMEM((CHUNK * LANES, 2 * ns), BF16),
            pltpu.VMEM((CHUNK * LANES, CHUNK * LANES // 2), BF16),
            pltpu.VMEM((CHUNK * LANES, 2 * ns), BF16),
            pltpu.VMEM((ns // LANES, FOLD_ROWS, LANES), F32),
            pltpu.VMEM((ns // LANES, FOLD_ROWS, LANES), F32),
            pltpu.VMEM((FOLD_ROWS, CHUNK * LANES), F32),
            pltpu.VMEM((N_SAMPLE_ROWS, 2 * ns), BF16),
        ],
        compiler_params=_params(1),
        name="s5",
    )(u_p, u_t, u_t, h0_re, h0_im, b_t, ct_t, pwl, d_skip, w_in, w_in, w_out)


def _out_kernel(x_ref, xn_ref, a_ref, sy_ref, wga_ref, wgb_ref, bga_ref, bgb_ref, gluw_ref, glub_ref,
                wa_ref, wb_ref, wo_ref, o_ref, s_ref):
    j = pl.program_id(1)

    @pl.when(j == 0)
    def _():
        o_ref[...] = x_ref[...]
        s = jax.nn.gelu(sy_ref[...])
        z = jnp.dot(s.astype(BF16), gluw_ref[...], preferred_element_type=F32) + glub_ref[...]
        s_ref[...] = (s * jax.nn.sigmoid(z)).astype(BF16)

    xn = xn_ref[...]
    ga = jax.nn.sigmoid(jnp.dot(xn, wga_ref[...], preferred_element_type=F32) + bga_ref[...])
    gb = jax.nn.sigmoid(jnp.dot(xn, wgb_ref[...], preferred_element_type=F32) + bgb_ref[...])
    merged = (ga * jnp.dot(a_ref[...], wa_ref[...], preferred_element_type=F32)
              + gb * jnp.dot(s_ref[...], wb_ref[...], preferred_element_type=F32))
    o_ref[...] += jnp.dot(merged.astype(BF16), wo_ref[...], preferred_element_type=F32)


def _out_proj(x, xn, a, sy, w_ga, w_gb, b_gate, glu_w, glu_b, w_a, w_b, w_out, tm):
    rows = x.shape[0]
    nb = D_MODEL // TN_OUT
    blk = lambda i, j: _serpentine(i, j, nb)
    return pl.pallas_call(
        _out_kernel,
        grid=(rows // tm, nb),
        in_specs=[
            pl.BlockSpec((tm, D_MODEL), lambda i, j: (i, 0)),
            pl.BlockSpec((tm, D_MODEL), lambda i, j: (i, 0)),
            pl.BlockSpec((tm, W_POOL), lambda i, j: (i, 0)),
            pl.BlockSpec((tm, W_SSM), lambda i, j: (i, 0)),
            pl.BlockSpec((D_MODEL, TN_OUT), lambda i, j: (0, blk(i, j))),
            pl.BlockSpec((D_MODEL, TN_OUT), lambda i, j: (0, blk(i, j))),
            pl.BlockSpec((1, TN_OUT), lambda i, j: (0, blk(i, j))),
            pl.BlockSpec((1, TN_OUT), lambda i, j: (0, nb + blk(i, j))),
            pl.BlockSpec((W_SSM, W_SSM), lambda i, j: (0, 0)),
            pl.BlockSpec((1, W_SSM), lambda i, j: (0, 0)),
            pl.BlockSpec((W_POOL, TN_OUT), lambda i, j: (0, blk(i, j))),
            pl.BlockSpec((W_SSM, TN_OUT), lambda i, j: (0, blk(i, j))),
            pl.BlockSpec((TN_OUT, D_MODEL), lambda i, j: (blk(i, j), 0)),
        ],
        out_specs=pl.BlockSpec((tm, D_MODEL), lambda i, j: (i, 0)),
        out_shape=jax.ShapeDtypeStruct((rows, D_MODEL), F32),
        scratch_shapes=[pltpu.VMEM((tm, W_SSM), BF16)],
        compiler_params=_params(2),
        name="out_proj",
    )(x, xn, a, sy, w_ga, w_gb, b_gate, b_gate, glu_w, glu_b, w_a, w_b, w_out)


def kernel(x_prompt, x_sample, state_pool, state_ssm_re, state_ssm_im, meta_tokens, norm_ffn1, ffn1_w_gate, ffn1_w_up, ffn1_w_down, norm_mix, w_in, b_gate, pool_w, pool_scale, ssm_lambda_re, ssm_lambda_im, ssm_log_dt, ssm_b_re, ssm_b_im, ssm_c_re, ssm_c_im, ssm_d, glu_w, glu_b, w_branch_a, w_branch_b, w_out, norm_ffn2, ffn2_w_gate, ffn2_w_up, ffn2_w_down, final_norm):
    l = 0
    bf = lambda w: w.astype(BF16)
    row = lambda v: v.reshape(1, -1).astype(F32)
    x_p = x_prompt.reshape(N_PROMPT_ROWS, D_MODEL)
    x_t = jnp.concatenate([jnp.transpose(x_sample, (1, 0, 2)).reshape(N_SAMPLE_ROWS, D_MODEL),
                           meta_tokens.astype(F32)], axis=0)

    def ffn_both(xp, xt, g, wg, wu, wd, final):
        op, *w_bf = _ffn(xp, g, wg, wu, wd, fg, TM_FFN, TF, final, emit_bf16=True)
        return op, _ffn(xt, g, *w_bf, fg, N_TAIL_ROWS, TF_TAIL, final)

    fg = row(final_norm)
    w_in_b = bf(w_in[l][:, :W_POOL + W_SSM])
    x1 = ffn_both(x_p, x_t, row(norm_ffn1[l]), ffn1_w_gate[l], ffn1_w_up[l], ffn1_w_down[l], False)

    pool_wb, pool_sc = bf(pool_w[l]), row(pool_scale[l])
    hist_t = jnp.transpose(state_pool[l], (1, 0, 2))
    u_t, xn_t = _inproj(x1[1], row(norm_mix[l]), w_in_b, N_TAIL_ROWS)
    a_t = _pool_tail(u_t, hist_t, pool_wb, pool_sc)
    u_p, a_p, xn_p = _inproj_pool_prompt(x1[0], row(norm_mix[l]), w_in_b, u_t, pool_wb, pool_sc)

    b_t, ct_t, pwl = _s5_tables(ssm_lambda_re[l], ssm_lambda_im[l], ssm_log_dt[l], ssm_b_re[l],
                                ssm_b_im[l], ssm_c_re[l], ssm_c_im[l])
    sy_p, sy_t, hp_re, hp_im, hs_re, hs_im, w_ga, w_gb, w_out_b = _s5(
        u_p, u_t, state_ssm_re[l].reshape(DEC_BATCH, -1), state_ssm_im[l].reshape(DEC_BATCH, -1),
        b_t, ct_t, pwl, row(ssm_d[l]), w_in[l], w_out[l])

    mix_w = (bf(glu_w[l]), row(glu_b[l]), bf(w_branch_a[l]), bf(w_branch_b[l]), w_out_b)
    x2 = [_out_proj(x, xn, a, sy, w_ga, w_gb, row(b_gate[l]), *mix_w, tm)
          for x, xn, a, sy, tm in zip(x1, (xn_p, xn_t), (a_p, a_t), (sy_p, sy_t), (TM, N_TAIL_ROWS))]
    y_p, y_t = ffn_both(*x2, row(norm_ffn2[l]), ffn2_w_gate[l], ffn2_w_up[l], ffn2_w_down[l], True)

    y_prompt = y_p.reshape(BATCH, SEQ, D_MODEL)
    y_sample = jnp.transpose(y_t[:N_SAMPLE_ROWS].reshape(DEC_SEQ, DEC_BATCH, D_MODEL), (1, 0, 2))
    pool_p = jnp.stack([u_p[(b + 1) * SEQ - POOL_HIST:(b + 1) * SEQ, :W_POOL] for b in range(BATCH)])[None]
    u_pool_s = jnp.transpose(u_t[:N_SAMPLE_ROWS, :W_POOL].reshape(DEC_SEQ, DEC_BATCH, W_POOL), (1, 0, 2))
    pool_s = jnp.concatenate([state_pool[l][:, DEC_SEQ:], u_pool_s], axis=1)[None]
    shp_p = (1, BATCH, N_GROUPS, SSM_STATE)
    shp_s = (1, DEC_BATCH, N_GROUPS, SSM_STATE)
    return (y_prompt, y_sample, pool_p, pool_s,
            hp_re.reshape(shp_p), hp_im.reshape(shp_p), hs_re.reshape(shp_s), hs_im.reshape(shp_s))
```

```python
import functools

import jax
import jax.numpy as jnp
from jax import lax
from jax.experimental import pallas as pl
from jax.experimental.pallas import tpu as pltpu

F32 = jnp.float32
BF16 = jnp.bfloat16

D_MODEL = 2048
BATCH = 4
SEQ = 2048
DEC_BATCH = 128
DEC_SEQ = 4
N_META = 16
D_FF = 5632
W_POOL = 1024
W_SSM = 1024
POOL_WINDOWS = (2, 4, 8, 16)
POOL_GW = 256
POOL_HIST = 15
SSM_GS = 16
N_GROUPS = 64
SSM_STATE = 64
RMS_EPS = 1e-6

N_PROMPT_ROWS = BATCH * SEQ
N_SAMPLE_ROWS = DEC_BATCH * DEC_SEQ
N_TAIL_ROWS = N_SAMPLE_ROWS + N_META
TM_FFN = 1024
TM = 512
TF = 256
TF_TAIL = 512
TN_OUT = 512

LANES = 128
N_SLABS = W_SSM // LANES
SLAB_STATE = (LANES // SSM_GS) * SSM_STATE
CHUNK = 8
N_META_CHUNKS = N_META // CHUNK
N_CHUNKS = (SEQ + N_META) // CHUNK
FOLD_ROW0 = 8
SEQ_STRIDE = FOLD_ROW0 + SEQ // CHUNK
FOLD_ROWS = BATCH * SEQ_STRIDE

V7X_VMEM_BYTES = 64 * 1024 * 1024
VMEM_LIMIT = V7X_VMEM_BYTES - 4 * 1024 * 1024


def _rms(x, g):
    r = lax.rsqrt(jnp.mean(x * x, axis=-1, keepdims=True) + RMS_EPS)
    return x * r * g


def _serpentine(i, j, n, forward_tile=-1):
    return jnp.where((i % 2 == 0) | (i == forward_tile), j, n - 1 - j)


def _params(n_axes):
    return pltpu.CompilerParams(dimension_semantics=("arbitrary",) * n_axes, vmem_limit_bytes=VMEM_LIMIT)


def _ffn_kernel(x_ref, g_ref, wg_ref, wu_ref, wd_ref, fg_ref, o_ref, *rest, final_norm, emit_bf16):
    xn_ref = rest[-1]
    j = pl.program_id(1)

    @pl.when(j == 0)
    def _():
        x = x_ref[...]
        xn_ref[...] = _rms(x, g_ref[...]).astype(BF16)
        o_ref[...] = x

    xn = xn_ref[...]
    wg = wg_ref[...].astype(BF16)
    wu = wu_ref[...].astype(BF16)
    wd = wd_ref[...].astype(BF16)
    if emit_bf16:
        wgb_ref, wub_ref, wdb_ref = rest[:3]
        wgb_ref[...] = wg
        wub_ref[...] = wu
        wdb_ref[...] = wd
    gate = jnp.dot(xn, wg, preferred_element_type=F32)
    up = jnp.dot(xn, wu, preferred_element_type=F32)
    h = (gate * jax.nn.sigmoid(gate) * up * 0.5).astype(BF16)
    o_ref[...] += jnp.dot(h, wd, preferred_element_type=F32)

    if final_norm:
        @pl.when(j == pl.num_programs(1) - 1)
        def _():
            o_ref[...] = _rms(o_ref[...], fg_ref[...])


def _ffn(x, g, wg, wu, wd, fg, tm, tf, final_norm, emit_bf16=False):
    rows = x.shape[0]
    n_steps = D_FF // tf
    row_spec = pl.BlockSpec((tm, D_MODEL), lambda i, j: (i, 0))
    out_specs, out_shape = row_spec, jax.ShapeDtypeStruct((rows, D_MODEL), F32)
    blk = lambda i, j: _serpentine(i, j, n_steps, forward_tile=rows // tm - 1)
    if emit_bf16:
        once = lambda i, j: jnp.where(i == 0, j, n_steps - 1)
        out_specs = [row_spec,
                     pl.BlockSpec((D_MODEL, tf), lambda i, j: (0, once(i, j))),
                     pl.BlockSpec((D_MODEL, tf), lambda i, j: (0, once(i, j))),
                     pl.BlockSpec((tf, D_MODEL), lambda i, j: (once(i, j), 0))]
        out_shape = [out_shape] + [jax.ShapeDtypeStruct(w.shape, BF16) for w in (wg, wu, wd)]
    return pl.pallas_call(
        functools.partial(_ffn_kernel, final_norm=final_norm, emit_bf16=emit_bf16),
        grid=(rows // tm, n_steps),
        in_specs=[
            row_spec,
            pl.BlockSpec((1, D_MODEL), lambda i, j: (0, 0)),
            pl.BlockSpec((D_MODEL, tf), lambda i, j: (0, blk(i, j))),
            pl.BlockSpec((D_MODEL, tf), lambda i, j: (0, blk(i, j))),
            pl.BlockSpec((tf, D_MODEL), lambda i, j: (blk(i, j), 0)),
            pl.BlockSpec((1, D_MODEL), lambda i, j: (0, 0)),
        ],
        out_specs=out_specs,
        out_shape=out_shape,
        scratch_shapes=[pltpu.VMEM((tm, D_MODEL), BF16)],
        compiler_params=_params(2),
        name="ffn_final" if final_norm else "ffn",
    )(x, g, wg, wu, wd, fg)


def _inproj_kernel(x_ref, g_ref, w_ref, o_ref, xn_ref, wb_ref):
    wb = w_ref[...].astype(BF16)
    wb_ref[...] = wb
    xn = _rms(x_ref[...], g_ref[...]).astype(BF16)
    xn_ref[...] = xn
    o_ref[...] = jnp.dot(xn, wb, preferred_element_type=F32)


def _inproj_tail(x, g, w_in):
    rows = x.shape[0]
    n = W_POOL + W_SSM
    return pl.pallas_call(
        _inproj_kernel,
        grid=(1,),
        in_specs=[
            pl.BlockSpec((rows, D_MODEL), lambda i: (0, 0)),
            pl.BlockSpec((1, D_MODEL), lambda i: (0, 0)),
            pl.BlockSpec((D_MODEL, n), lambda i: (0, 0)),
        ],
        out_specs=[pl.BlockSpec((rows, n), lambda i: (0, 0)), pl.BlockSpec((rows, D_MODEL), lambda i: (0, 0)),
                   pl.BlockSpec((D_MODEL, n), lambda i: (0, 0))],
        out_shape=[jax.ShapeDtypeStruct((rows, n), F32), jax.ShapeDtypeStruct((rows, D_MODEL), BF16),
                   jax.ShapeDtypeStruct((D_MODEL, n), BF16)],
        compiler_params=_params(1),
        name="inproj",
    )(x, g, w_in)


TILES_PER_SEQ = SEQ // TM
HALO = 16


def _pool_project(mean_ref, u_ref, pw_ref, scale_ref, o_ref):
    for g in range(len(POOL_WINDOWS)):
        cols = slice(g * POOL_GW, (g + 1) * POOL_GW)
        d = (mean_ref[:, cols] - u_ref[:, cols]).astype(BF16)
        y = jnp.dot(d, pw_ref[g], preferred_element_type=F32) * scale_ref[:, cols]
        o_ref[:, cols] = y.astype(BF16)


def _inproj_pool_kernel(x_ref, g_ref, w_ref, meta_ref, pw_ref, scale_ref, u_ref, o_ref, xn_ref, full_ref):
    @pl.when(pl.program_id(0) % TILES_PER_SEQ == 0)
    def _():
        full_ref[0:HALO, :] = meta_ref[...]

    xn = _rms(x_ref[...], g_ref[...]).astype(BF16)
    xn_ref[...] = xn
    u_pool = jnp.dot(xn, w_ref[:, :W_POOL], preferred_element_type=F32)
    u_ref[:, :W_POOL] = u_pool
    full_ref[HALO:HALO + TM, :] = u_pool
    for g, w in enumerate(POOL_WINDOWS):
        cols = slice(g * POOL_GW, (g + 1) * POOL_GW)
        ucols = slice(W_POOL + g * POOL_GW, W_POOL + (g + 1) * POOL_GW)
        u_ref[:, ucols] = jnp.dot(xn, w_ref[:, ucols], preferred_element_type=F32)
        acc = full_ref[HALO:HALO + TM, cols]
        for k in range(1, w):
            acc = acc + full_ref[HALO - k:HALO - k + TM, cols]
        d = (acc * (1.0 / w) - full_ref[HALO:HALO + TM, cols]).astype(BF16)
        y = jnp.dot(d, pw_ref[g], preferred_element_type=F32) * scale_ref[:, cols]
        o_ref[:, cols] = y.astype(BF16)
    full_ref[0:HALO, :] = full_ref[TM:TM + HALO, :]


def _pool_tail_kernel(u_ref, hist_ref, pw_ref, scale_ref, o_ref, mean_ref):
    for g, w in enumerate(POOL_WINDOWS):
        cols = slice(g * POOL_GW, (g + 1) * POOL_GW)
        for t in range(DEC_SEQ):
            acc = None
            for k in range(w):
                p = POOL_HIST + t - k
                if p >= POOL_HIST:
                    q = p - POOL_HIST
                    term = u_ref[q * DEC_BATCH:(q + 1) * DEC_BATCH, cols]
                else:
                    term = hist_ref[p, :, cols]
                acc = term if acc is None else acc + term
            mean_ref[t * DEC_BATCH:(t + 1) * DEC_BATCH, cols] = acc * (1.0 / w)
    mean_ref[N_SAMPLE_ROWS:, :] = u_ref[N_SAMPLE_ROWS:, :]
    _pool_project(mean_ref, u_ref, pw_ref, scale_ref, o_ref)


_PW_SPEC = pl.BlockSpec((len(POOL_WINDOWS), POOL_GW, POOL_GW), lambda i: (0, 0, 0))
_SCALE_SPEC = pl.BlockSpec((1, W_POOL), lambda i: (0, 0))


def _inproj_pool_prompt(x, g, w_in, u_t, pool_w, pool_scale):
    n = W_POOL + W_SSM
    return pl.pallas_call(
        _inproj_pool_kernel,
        grid=(N_PROMPT_ROWS // TM,),
        in_specs=[
            pl.BlockSpec((TM, D_MODEL), lambda i: (i, 0)),
            pl.BlockSpec((1, D_MODEL), lambda i: (0, 0)),
            pl.BlockSpec((D_MODEL, n), lambda i: (0, 0)),
            pl.BlockSpec((N_META, W_POOL), lambda i: (N_SAMPLE_ROWS // N_META, 0)),
            _PW_SPEC, _SCALE_SPEC,
        ],
        out_specs=[pl.BlockSpec((TM, n), lambda i: (i, 0)), pl.BlockSpec((TM, W_POOL), lambda i: (i, 0)),
                   pl.BlockSpec((TM, D_MODEL), lambda i: (i, 0))],
        out_shape=[jax.ShapeDtypeStruct((N_PROMPT_ROWS, n), F32),
                   jax.ShapeDtypeStruct((N_PROMPT_ROWS, W_POOL), BF16),
                   jax.ShapeDtypeStruct((N_PROMPT_ROWS, D_MODEL), BF16)],
        scratch_shapes=[pltpu.VMEM((HALO + TM, W_POOL), F32)],
        compiler_params=_params(1),
        name="inproj_pool",
    )(x, g, w_in, u_t, pool_w, pool_scale)


def _pool_tail(u_t, hist_t, pool_w, pool_scale):
    return pl.pallas_call(
        _pool_tail_kernel,
        grid=(1,),
        in_specs=[
            pl.BlockSpec((N_TAIL_ROWS, W_POOL), lambda i: (0, 0)),
            pl.BlockSpec((POOL_HIST, DEC_BATCH, W_POOL), lambda i: (0, 0, 0)),
            _PW_SPEC, _SCALE_SPEC,
        ],
        out_specs=pl.BlockSpec((N_TAIL_ROWS, W_POOL), lambda i: (0, 0)),
        out_shape=jax.ShapeDtypeStruct((N_TAIL_ROWS, W_POOL), BF16),
        scratch_shapes=[pltpu.VMEM((N_TAIL_ROWS, W_POOL), F32)],
        compiler_params=_params(1),
        name="pool_tail",
    )(u_t, hist_t, pool_w, pool_scale)


def _s5_tables(lam_re, lam_im, log_dt, b_re, b_im, c_re, c_im):
    dt = jnp.exp(log_dt)[:, None]
    k = jnp.arange(CHUNK + 1, dtype=F32)[:, None, None]
    mag = jnp.exp(k * (lam_re * dt)[None])
    ang = k * (lam_im * dt)[None]
    pw_re, pw_im = mag * jnp.cos(ang), mag * jnp.sin(ang)
    lb_re, lb_im = pw_re[1], pw_im[1]
    den = lam_re * lam_re + lam_im * lam_im
    q_re = ((lb_re - 1.0) * lam_re + lb_im * lam_im) / den
    q_im = (lb_im * lam_re - (lb_re - 1.0) * lam_im) / den
    bt_re, bt_im = jnp.transpose(b_re, (0, 2, 1)), jnp.transpose(b_im, (0, 2, 1))
    bb_re = q_re[:, None, :] * bt_re - q_im[:, None, :] * bt_im
    bb_im = q_re[:, None, :] * bt_im + q_im[:, None, :] * bt_re
    slab = lambda t: t.reshape(N_SLABS, LANES, SSM_STATE)
    b_c = jnp.concatenate([slab(bb_re), slab(bb_im)], axis=2)
    c_c = jnp.concatenate([slab(c_re), slab(c_im)], axis=2)
    pr = pw_re.reshape(CHUNK + 1, N_SLABS, SLAB_STATE)
    pi = pw_im.reshape(CHUNK + 1, N_SLABS, SLAB_STATE)
    pwl = jnp.concatenate([jnp.transpose(pr, (1, 0, 2)), jnp.transpose(pi, (1, 0, 2))], axis=2)
    return b_c, c_c, pwl


_NT = (((1,), (1,)), ((), ()))


def _block_diag(x):
    tiled = jnp.concatenate([x] * (LANES // SSM_GS), axis=1)
    shift = lambda n: n.bit_length() - 1
    row_group = lax.shift_right_logical(lax.broadcasted_iota(jnp.int32, tiled.shape, 0), shift(SSM_GS))
    lane_group = lax.shift_right_logical(lax.broadcasted_iota(jnp.int32, tiled.shape, 1), shift(SSM_STATE))
    return jnp.where(row_group == lane_group, tiled, 0.0)


def _s5_kernel(up_ref, um_ref, us_ref, h0re_ref, h0im_ref, b_ref, ct_ref, pwl_ref, d_ref,
               wga_ref, wgb_ref, wo_ref,
               yp_ref, yt_ref, hpre_ref, hpim_ref, hsre_ref, hsim_ref, wgab_ref, wgbb_ref, wob_ref,
               l_ref, f_ref, m_ref, et_ref, vre_ref, vim_ref, yf_ref, hs_ref):
    ns = SLAB_STATE
    wgab_ref[...] = wga_ref[...].astype(BF16)
    wgbb_ref[...] = wgb_ref[...].astype(BF16)
    wob_ref[...] = wo_ref[...].astype(BF16)
    b_re = _block_diag(b_ref[0][:, :SSM_STATE])
    b_im = _block_diag(b_ref[0][:, SSM_STATE:])
    ct_re = _block_diag(ct_ref[0][:, :SSM_STATE])
    ct_im = _block_diag(ct_ref[0][:, SSM_STATE:])
    ct_neg = jnp.concatenate([ct_re, -ct_im], axis=1).astype(BF16)

    for b in range(BATCH):
        r0 = b * SEQ_STRIDE + FOLD_ROW0
        l_ref[r0 - FOLD_ROW0:r0, :] = jnp.zeros((FOLD_ROW0, CHUNK * LANES), F32)
        for s in range(CHUNK):
            cols = slice(s * LANES, (s + 1) * LANES)
            l_ref[r0:r0 + SEQ // CHUNK, cols] = up_ref[pl.ds(b * SEQ + s, SEQ // CHUNK, stride=CHUNK), :]
            for m in range(N_META_CHUNKS):
                r = r0 - N_META_CHUNKS + m
                l_ref[r:r + 1, cols] = um_ref[m * CHUNK + s:m * CHUNK + s + 1, :]
    lb = l_ref[...].astype(BF16)

    for s in range(CHUNK):
        p = pwl_ref[0, CHUNK - 1 - s:CHUNK - s, :]
        p_re, p_im = p[:, :ns], p[:, ns:]
        rows = slice(s * LANES, (s + 1) * LANES)
        f_ref[rows, :ns] = (b_re * p_re - b_im * p_im).astype(BF16)
        f_ref[rows, ns:] = (b_re * p_im + b_im * p_re).astype(BF16)
    v = jnp.dot(lb, f_ref[...], preferred_element_type=F32)
    nk = ns // LANES
    for k in range(nk):
        vre_ref[k] = v[:, k * LANES:(k + 1) * LANES]
        vim_ref[k] = v[:, ns + k * LANES:ns + (k + 1) * LANES]

    a = pwl_ref[0, CHUNK:CHUNK + 1, :]
    a_re = [jnp.broadcast_to(a[:, k * LANES:(k + 1) * LANES], (BATCH, LANES)) for k in range(nk)]
    a_im = [jnp.broadcast_to(a[:, ns + k * LANES:ns + (k + 1) * LANES], (BATCH, LANES)) for k in range(nk)]
    h_re = [jnp.zeros((BATCH, LANES), F32) for _ in range(nk)]
    h_im = [jnp.zeros((BATCH, LANES), F32) for _ in range(nk)]
    for j in range(N_CHUNKS):
        rows = pl.ds(FOLD_ROW0 - N_META_CHUNKS + j, BATCH, stride=SEQ_STRIDE)
        for k in range(nk):
            v_re = vre_ref[k, rows, :]
            v_im = vim_ref[k, rows, :]
            vre_ref[k, rows, :] = h_re[k]
            vim_ref[k, rows, :] = h_im[k]
            h_re[k], h_im[k] = (a_re[k] * h_re[k] - a_im[k] * h_im[k] + v_re,
                                a_re[k] * h_im[k] + a_im[k] * h_re[k] + v_im)
    for k in range(nk):
        hpre_ref[:, k * LANES:(k + 1) * LANES] = h_re[k]
        hpim_ref[:, k * LANES:(k + 1) * LANES] = h_im[k]

    kb_all = lax.dot_general(f_ref[...], ct_neg, _NT, preferred_element_type=F32)
    ri = lax.broadcasted_iota(jnp.int32, (LANES, LANES), 0)
    ci = lax.broadcasted_iota(jnp.int32, (LANES, LANES), 1)
    d_diag = jnp.where(ri == ci, jnp.broadcast_to(d_ref[...], (LANES, LANES)), 0.0)
    kb = [None] * CHUNK
    for s in range(CHUNK):
        blk = kb_all[s * LANES:(s + 1) * LANES, :]
        if s == CHUNK - 1:
            blk = blk + d_diag
        kb[CHUNK - 1 - s] = blk.astype(BF16)

    half = CHUNK // 2
    for s in range(CHUNK):
        for t in range(half, CHUNK):
            blk = kb[t - s] if t >= s else jnp.zeros((LANES, LANES), BF16)
            m_ref[s * LANES:(s + 1) * LANES, (t - half) * LANES:(t - half + 1) * LANES] = blk
    y_left = jnp.dot(lb[:, :half * LANES], m_ref[half * LANES:, :], preferred_element_type=F32)
    y_right = jnp.dot(lb, m_ref[...], preferred_element_type=F32)

    for t in range(CHUNK):
        p = pwl_ref[0, t + 1:t + 2, :]
        p_re, p_im = p[:, :ns], p[:, ns:]
        rows = slice(t * LANES, (t + 1) * LANES)
        et_ref[rows, :ns] = (ct_re * p_re - ct_im * p_im).astype(BF16)
        et_ref[rows, ns:] = (-(ct_re * p_im + ct_im * p_re)).astype(BF16)
    h_in = jnp.concatenate([vre_ref[k] for k in range(nk)] + [vim_ref[k] for k in range(nk)],
                           axis=1).astype(BF16)
    ys = lax.dot_general(h_in, et_ref[...], _NT, preferred_element_type=F32)
    yf_ref[:, :half * LANES] = y_left + ys[:, :half * LANES]
    yf_ref[:, half * LANES:] = y_right + ys[:, half * LANES:]
    for b in range(BATCH):
        r0 = b * SEQ_STRIDE + FOLD_ROW0
        for t in range(CHUNK):
            yp_ref[pl.ds(b * SEQ + t, SEQ // CHUNK, stride=CHUNK), :] = (
                yf_ref[r0:r0 + SEQ // CHUNK, t * LANES:(t + 1) * LANES])

    u = us_ref[...]
    b_all = jnp.concatenate([b_re, b_im], axis=1).astype(BF16)
    bu = jnp.dot(u.astype(BF16), b_all, preferred_element_type=F32)
    l1 = pwl_ref[0, 1:2, :]
    l_re, l_im = l1[:, :ns], l1[:, ns:]
    g_re = h0re_ref[...]
    g_im = h0im_ref[...]
    for t in range(DEC_SEQ):
        rows = slice(t * DEC_BATCH, (t + 1) * DEC_BATCH)
        g_re, g_im = (l_re * g_re - l_im * g_im + bu[rows, :ns],
                      l_re * g_im + l_im * g_re + bu[rows, ns:])
        hs_ref[rows, :ns] = g_re.astype(BF16)
        hs_ref[rows, ns:] = g_im.astype(BF16)
    hsre_ref[...] = g_re
    hsim_ref[...] = g_im
    yt_ref[0:N_SAMPLE_ROWS, :] = (lax.dot_general(hs_ref[...], ct_neg, _NT, preferred_element_type=F32)
                                  + d_ref[...] * u)
    yt_ref[N_SAMPLE_ROWS:, :] = jnp.zeros((N_META, LANES), F32)


def _s5(u_p, u_t, h0_re, h0_im, b_t, ct_t, pwl, d_skip, w_in, w_out):
    c0 = W_POOL // LANES
    ns = SLAB_STATE
    state = lambda rows: pl.BlockSpec((rows, ns), lambda i: (0, i))
    wr = D_MODEL // N_SLABS
    gate_blk0 = (W_POOL + W_SSM) // D_MODEL
    return pl.pallas_call(
        _s5_kernel,
        grid=(N_SLABS,),
        in_specs=[
            pl.BlockSpec((N_PROMPT_ROWS, LANES), lambda i: (0, c0 + i)),
            pl.BlockSpec((N_META, LANES), lambda i: (N_SAMPLE_ROWS // N_META, c0 + i)),
            pl.BlockSpec((N_SAMPLE_ROWS, LANES), lambda i: (0, c0 + i)),
            state(DEC_BATCH), state(DEC_BATCH),
            pl.BlockSpec((1, LANES, 2 * SSM_STATE), lambda i: (i, 0, 0)),
            pl.BlockSpec((1, LANES, 2 * SSM_STATE), lambda i: (i, 0, 0)),
            pl.BlockSpec((1, CHUNK + 1, 2 * ns), lambda i: (i, 0, 0)),
            pl.BlockSpec((1, LANES), lambda i: (0, i)),
            pl.BlockSpec((wr, D_MODEL), lambda i: (i, gate_blk0)),
            pl.BlockSpec((wr, D_MODEL), lambda i: (i, gate_blk0 + 1)),
            pl.BlockSpec((wr, D_MODEL), lambda i: (i, 0)),
        ],
        out_specs=[
            pl.BlockSpec((N_PROMPT_ROWS, LANES), lambda i: (0, i)),
            pl.BlockSpec((N_TAIL_ROWS, LANES), lambda i: (0, i)),
            state(BATCH), state(BATCH), state(DEC_BATCH), state(DEC_BATCH),
        ] + [pl.BlockSpec((wr, D_MODEL), lambda i: (i, 0))] * 3,
        out_shape=[
            jax.ShapeDtypeStruct((N_PROMPT_ROWS, W_SSM), F32),
            jax.ShapeDtypeStruct((N_TAIL_ROWS, W_SSM), F32),
            jax.ShapeDtypeStruct((BATCH, N_GROUPS * SSM_STATE), F32),
            jax.ShapeDtypeStruct((BATCH, N_GROUPS * SSM_STATE), F32),
            jax.ShapeDtypeStruct((DEC_BATCH, N_GROUPS * SSM_STATE), F32),
            jax.ShapeDtypeStruct((DEC_BATCH, N_GROUPS * SSM_STATE), F32),
        ] + [jax.ShapeDtypeStruct((D_MODEL, D_MODEL), BF16)] * 3,
        scratch_shapes=[
            pltpu.VMEM((FOLD_ROWS, CHUNK * LANES), F32),
            pltpu.VMEM((CHUNK * LANES, 2 * ns), BF16),
            pltpu.VMEM((CHUNK * LANES, CHUNK * LANES // 2), BF16),
            pltpu.VMEM((CHUNK * LANES, 2 * ns), BF16),
            pltpu.VMEM((ns // LANES, FOLD_ROWS, LANES), F32),
            pltpu.VMEM((ns // LANES, FOLD_ROWS, LANES), F32),
            pltpu.VMEM((FOLD_ROWS, CHUNK * LANES), F32),
            pltpu.VMEM((N_SAMPLE_ROWS, 2 * ns), BF16),
        ],
        compiler_params=_params(1),
        name="s5",
    )(u_p, u_t, u_t, h0_re, h0_im, b_t, ct_t, pwl, d_skip, w_in, w_in, w_out)


def _out_kernel(x_ref, xn_ref, a_ref, sy_ref, wga_ref, wgb_ref, bga_ref, bgb_ref, gluw_ref, glub_ref,
                wa_ref, wb_ref, wo_ref, o_ref, s_ref):
    j = pl.program_id(1)

    @pl.when(j == 0)
    def _():
        o_ref[...] = x_ref[...]
        s = jax.nn.gelu(sy_ref[...])
        z = jnp.dot(s.astype(BF16), gluw_ref[...], preferred_element_type=F32) + glub_ref[...]
        s_ref[...] = (s * jax.nn.sigmoid(z)).astype(BF16)

    xn = xn_ref[...]
    ga = jax.nn.sigmoid(jnp.dot(xn, wga_ref[...], preferred_element_type=F32) + bga_ref[...])
    gb = jax.nn.sigmoid(jnp.dot(xn, wgb_ref[...], preferred_element_type=F32) + bgb_ref[...])
    merged = (ga * jnp.dot(a_ref[...], wa_ref[...], preferred_element_type=F32)
              + gb * jnp.dot(s_ref[...], wb_ref[...], preferred_element_type=F32))
    o_ref[...] += jnp.dot(merged.astype(BF16), wo_ref[...], preferred_element_type=F32)


def _out_proj(x, xn, a, sy, w_ga, w_gb, b_gate, glu_w, glu_b, w_a, w_b, w_out, tm):
    rows = x.shape[0]
    nb = D_MODEL // TN_OUT
    blk = lambda i, j: _serpentine(i, j, nb)
    return pl.pallas_call(
        _out_kernel,
        grid=(rows // tm, nb),
        in_specs=[
            pl.BlockSpec((tm, D_MODEL), lambda i, j: (i, 0)),
            pl.BlockSpec((tm, D_MODEL), lambda i, j: (i, 0)),
            pl.BlockSpec((tm, W_POOL), lambda i, j: (i, 0)),
            pl.BlockSpec((tm, W_SSM), lambda i, j: (i, 0)),
            pl.BlockSpec((D_MODEL, TN_OUT), lambda i, j: (0, blk(i, j))),
            pl.BlockSpec((D_MODEL, TN_OUT), lambda i, j: (0, blk(i, j))),
            pl.BlockSpec((1, TN_OUT), lambda i, j: (0, blk(i, j))),
            pl.BlockSpec((1, TN_OUT), lambda i, j: (0, nb + blk(i, j))),
            pl.BlockSpec((W_SSM, W_SSM), lambda i, j: (0, 0)),
            pl.BlockSpec((1, W_SSM), lambda i, j: (0, 0)),
            pl.BlockSpec((W_POOL, TN_OUT), lambda i, j: (0, blk(i, j))),
            pl.BlockSpec((W_SSM, TN_OUT), lambda i, j: (0, blk(i, j))),
            pl.BlockSpec((TN_OUT, D_MODEL), lambda i, j: (blk(i, j), 0)),
        ],
        out_specs=pl.BlockSpec((tm, D_MODEL), lambda i, j: (i, 0)),
        out_shape=jax.ShapeDtypeStruct((rows, D_MODEL), F32),
        scratch_shapes=[pltpu.VMEM((tm, W_SSM), BF16)],
        compiler_params=_params(2),
        name="out_proj",
    )(x, xn, a, sy, w_ga, w_gb, b_gate, b_gate, glu_w, glu_b, w_a, w_b, w_out)


def kernel(x_prompt, x_sample, state_pool, state_ssm_re, state_ssm_im, meta_tokens, norm_ffn1, ffn1_w_gate, ffn1_w_up, ffn1_w_down, norm_mix, w_in, b_gate, pool_w, pool_scale, ssm_lambda_re, ssm_lambda_im, ssm_log_dt, ssm_b_re, ssm_b_im, ssm_c_re, ssm_c_im, ssm_d, glu_w, glu_b, w_branch_a, w_branch_b, w_out, norm_ffn2, ffn2_w_gate, ffn2_w_up, ffn2_w_down, final_norm):
    l = 0
    bf = lambda w: w.astype(BF16)
    row = lambda v: v.reshape(1, -1).astype(F32)
    x_p = x_prompt.reshape(N_PROMPT_ROWS, D_MODEL)
    x_t = jnp.concatenate([jnp.transpose(x_sample, (1, 0, 2)).reshape(N_SAMPLE_ROWS, D_MODEL),
                           meta_tokens.astype(F32)], axis=0)

    def ffn_both(xp, xt, g, wg, wu, wd, final):
        op, *w_bf = _ffn(xp, g, wg, wu, wd, fg, TM_FFN, TF, final, emit_bf16=True)
        return op, _ffn(xt, g, *w_bf, fg, N_TAIL_ROWS, TF_TAIL, final)

    fg = row(final_norm)
    x1 = ffn_both(x_p, x_t, row(norm_ffn1[l]), ffn1_w_gate[l], ffn1_w_up[l], ffn1_w_down[l], False)

    pool_wb, pool_sc = bf(pool_w[l]), row(pool_scale[l])
    hist_t = jnp.transpose(state_pool[l], (1, 0, 2))
    u_t, xn_t, w_in_b = _inproj_tail(x1[1], row(norm_mix[l]), w_in[l])
    a_t = _pool_tail(u_t, hist_t, pool_wb, pool_sc)
    u_p, a_p, xn_p = _inproj_pool_prompt(x1[0], row(norm_mix[l]), w_in_b, u_t, pool_wb, pool_sc)

    b_t, ct_t, pwl = _s5_tables(ssm_lambda_re[l], ssm_lambda_im[l], ssm_log_dt[l], ssm_b_re[l],
                                ssm_b_im[l], ssm_c_re[l], ssm_c_im[l])
    sy_p, sy_t, hp_re, hp_im, hs_re, hs_im, w_ga, w_gb, w_out_b = _s5(
        u_p, u_t, state_ssm_re[l].reshape(DEC_BATCH, -1), state_ssm_im[l].reshape(DEC_BATCH, -1),
        b_t, ct_t, pwl, row(ssm_d[l]), w_in[l], w_out[l])

    mix_w = (bf(glu_w[l]), row(glu_b[l]), bf(w_branch_a[l]), bf(w_branch_b[l]), w_out_b)
    x2 = [_out_proj(x, xn, a, sy, w_ga, w_gb, row(b_gate[l]), *mix_w, tm)
          for x, xn, a, sy, tm in zip(x1, (xn_p, xn_t), (a_p, a_t), (sy_p, sy_t), (TM, N_TAIL_ROWS))]
    y_p, y_t = ffn_both(*x2, row(norm_ffn2[l]), ffn2_w_gate[l], ffn2_w_up[l], ffn2_w_down[l], True)

    y_prompt = y_p.reshape(BATCH, SEQ, D_MODEL)
    y_sample = jnp.transpose(y_t[:N_SAMPLE_ROWS].reshape(DEC_SEQ, DEC_BATCH, D_MODEL), (1, 0, 2))
    pool_p = jnp.stack([u_p[(b + 1) * SEQ - POOL_HIST:(b + 1) * SEQ, :W_POOL] for b in range(BATCH)])[None]
    u_pool_s = jnp.transpose(u_t[:N_SAMPLE_ROWS, :W_POOL].reshape(DEC_SEQ, DEC_BATCH, W_POOL), (1, 0, 2))
    pool_s = jnp.concatenate([state_pool[l][:, DEC_SEQ:], u_pool_s], axis=1)[None]
    shp_p = (1, BATCH, N_GROUPS, SSM_STATE)
    shp_s = (1, DEC_BATCH, N_GROUPS, SSM_STATE)
    return (y_prompt, y_sample, pool_p, pool_s,
            hp_re.reshape(shp_p), hp_im.reshape(shp_p), hs_re.reshape(shp_s), hs_im.reshape(shp_s))
```
